```python
import jax, jax.numpy as jnp
from jax import lax
import numpy as np

D_MODEL = 1024
BATCH = 2
SEQ = 8192
DEPTH = 2

GRID_W = 64
CTX_LEN = 256
HEAD_DIM = 64
NA_HEADS = 4
NA_WIN_R = 8
NA_WIN_C = 16
NA_QCB = 16
NA_BAND = 32
FNET_GROUPS = 4
FNET_GROUP_DIM = 64
SWA_HEADS = 4
SWA_KV_HEADS = 2
SWA_WINDOW = 128
SWA_BLOCK = 128
HGRN_HEADS = 4
HGRN_DK = 64
HGRN_DV = 64
HGRN_CHUNK = 64
ROPE_THETA = 10000.0
EPS = 1e-6
W_A = NA_HEADS * HEAD_DIM
W_B = FNET_GROUPS * FNET_GROUP_DIM
W_C = SWA_HEADS * HEAD_DIM
W_C_KV = SWA_KV_HEADS * HEAD_DIM
W_DK = HGRN_HEADS * HGRN_DK
W_DV = HGRN_HEADS * HGRN_DV
D_MIX = W_A + W_B + W_C + W_DV
IN_SPLITS = (W_A, W_A, W_A, W_B, W_C, W_C_KV, W_C_KV, W_DK, W_DK, W_DK, W_DV, W_DV)
D_IN = sum(IN_SPLITS)
D_FF = -(-8 * D_MODEL // (3 * 256)) * 256

kernel_name = "hybrid_natten_fnet_swa_hgrn2_dit"

F32 = jnp.float32


def rms_norm(x, g):
    xf = x.astype(F32)
    y = xf * lax.rsqrt(jnp.mean(xf * xf, axis=-1, keepdims=True) + EPS)
    return (y * g.astype(F32)).astype(x.dtype)


def modulate(h, shift, scale):
    return h * (1 + scale) + shift


def split_in(y):
    out, start = [], 0
    for w in IN_SPLITS:
        out.append(y[..., start:start + w])
        start += w
    return out


def heads(t, n):
    b, l, _ = t.shape
    return t.reshape(b, l, n, -1).transpose(0, 2, 1, 3)


def merge(t):
    b, h, l, d = t.shape
    return t.transpose(0, 2, 1, 3).reshape(b, l, h * d)


def rope_1d(x, pos):
    half = x.shape[-1] // 2
    freqs = ROPE_THETA ** (-jnp.arange(half, dtype=F32) / half)
    ang = jnp.asarray(pos, F32)[:, None] * freqs
    cos, sin = jnp.cos(ang).astype(x.dtype), jnp.sin(ang).astype(x.dtype)
    x1, x2 = x[..., :half], x[..., half:]
    return jnp.concatenate([x1 * cos - x2 * sin, x1 * sin + x2 * cos], axis=-1)


def axial_rope(x, rows, cols):
    half = x.shape[-1] // 2
    return jnp.concatenate([rope_1d(x[..., :half], rows), rope_1d(x[..., half:], cols)], axis=-1)


def dense_attention(q, k, v, sink=None):
    b, hq, l, d = q.shape
    g = k.shape[1]
    qg = q.reshape(b, g, hq // g, l, d)
    s = jnp.einsum('bgrqd,bgkd->bgrqk', qg, k).astype(F32) * HEAD_DIM ** -0.5
    if sink is not None:
        s_sink = jnp.broadcast_to(sink.astype(F32).reshape(1, g, hq // g, 1, 1), s.shape[:-1] + (1,))
        s = jnp.concatenate([s, s_sink], axis=-1)
    p = jax.nn.softmax(s, axis=-1)[..., :l].astype(v.dtype)
    return jnp.einsum('bgrqk,bgkd->bgrqd', p, v).reshape(b, hq, l, d)


def neighbourhood_attention(q, k, v, kc, vc, rpb):
    b, h, n, d = q.shape
    rows = n // GRID_W
    kr = min(NA_WIN_R, rows)
    ncb = GRID_W // NA_QCB
    r = np.arange(rows)
    row_idx = np.clip(r - kr // 2, 0, rows - kr)[:, None] + np.arange(kr)
    qcol = np.arange(GRID_W).reshape(ncb, NA_QCB)
    cs = np.clip(qcol - NA_WIN_C // 2, 0, GRID_W - NA_WIN_C)
    band = np.clip(np.arange(ncb) * NA_QCB - NA_WIN_C // 2, 0, GRID_W - NA_BAND)[:, None] + np.arange(NA_BAND)
    col_ok = (band[:, None, :] >= cs[:, :, None]) & (band[:, None, :] < cs[:, :, None] + NA_WIN_C)
    dr = row_idx - r[:, None] + NA_WIN_R - 1
    dc = np.clip(band[:, None, :] - qcol[:, :, None], -(NA_WIN_C - 1), NA_WIN_C - 1) + NA_WIN_C - 1
    bias = rpb[:, dr[:, None, None, :, None], dc[None, :, :, None, :]].astype(F32)
    bias = jnp.where(col_ok[None, None, :, :, None, :], bias, -jnp.inf)
    qg = q.reshape(b, h, rows, ncb, NA_QCB, d)
    gi_r, gi_c = row_idx[:, :, None, None], band[None, None, :, :]
    kg = k.reshape(b, h, rows, GRID_W, d)[:, :, gi_r, gi_c]
    vg = v.reshape(b, h, rows, GRID_W, d)[:, :, gi_r, gi_c]
    scale = HEAD_DIM ** -0.5
    s_nb = jnp.einsum('bhrcqd,bhrwckd->bhrcqwk', qg, kg).astype(F32) * scale + bias
    s_nb = s_nb.reshape(b, h, rows, ncb, NA_QCB, kr * NA_BAND)
    s_ctx = jnp.einsum('bhrcqd,bhjd->bhrcqj', qg, kc).astype(F32) * scale
    p = jax.nn.softmax(jnp.concatenate([s_nb, s_ctx], axis=-1), axis=-1)
    p_nb = p[..., :kr * NA_BAND].reshape(b, h, rows, ncb, NA_QCB, kr, NA_BAND).astype(v.dtype)
    p_ctx = p[..., kr * NA_BAND:].astype(v.dtype)
    o = jnp.einsum('bhrcqwk,bhrwckd->bhrcqd', p_nb, vg) + jnp.einsum('bhrcqj,bhjd->bhrcqd', p_ctx, vc)
    return o.reshape(b, h, n, d)


def fourier_mix(u):
    b, l, _ = u.shape
    z = jnp.fft.fft2(u.astype(F32).reshape(b, l, FNET_GROUPS, FNET_GROUP_DIM), axes=(1, 3), norm='ortho')
    return jnp.real(z).reshape(b, l, W_B).astype(u.dtype)


def sliding_window_gqa(q, k, v, kc, vc, sink):
    b, hq, n, d = q.shape
    g = k.shape[1]
    rep = hq // g
    blk = SWA_BLOCK
    nb = n // blk
    lc = kc.shape[2]
    qb = q.reshape(b, g, rep, nb, blk, d)

    def banded(t):
        tp = jnp.pad(t, ((0, 0), (0, 0), (blk, blk), (0, 0))).reshape(b, g, nb + 2, blk, d)
        return jnp.concatenate([tp[:, :, :-2], tp[:, :, 1:-1], tp[:, :, 2:]], axis=3)

    kb, vb = banded(k), banded(v)
    i = np.arange(blk)
    m = np.arange(3 * blk)
    rel = m[None, :] - blk - i[:, None]
    s_pos = np.arange(nb)[:, None, None] * blk - blk + m[None, None, :]
    ok = (np.abs(rel)[None] <= SWA_WINDOW) & (s_pos >= 0) & (s_pos < n)
    scale = HEAD_DIM ** -0.5
    s_loc = jnp.einsum('bgrnqd,bgnkd->bgrnqk', qb, kb).astype(F32) * scale
    s_loc = jnp.where(ok, s_loc, -jnp.inf)
    s_ctx = jnp.einsum('bgrnqd,bgjd->bgrnqj', qb, kc).astype(F32) * scale
    s_sink = jnp.broadcast_to(sink.astype(F32).reshape(1, g, rep, 1, 1, 1), s_loc.shape[:-1] + (1,))
    p = jax.nn.softmax(jnp.concatenate([s_loc, s_ctx, s_sink], axis=-1), axis=-1)
    p_loc = p[..., :3 * blk].astype(v.dtype)
    p_ctx = p[..., 3 * blk:3 * blk + lc].astype(v.dtype)
    o = jnp.einsum('bgrnqk,bgnkd->bgrnqd', p_loc, vb) + jnp.einsum('bgrnqj,bgjd->bgrnqd', p_ctx, vc)
    return o.reshape(b, hq, n, d)


def hgrn2_gates(fz, lb):
    lb = lb.reshape(HGRN_HEADS, 1, HGRN_DK)
    logf = jnp.logaddexp(jnp.log(lb), jnp.log1p(-lb) + jax.nn.log_sigmoid(fz.astype(F32)))
    return logf, -jnp.expm1(logf)


def chunk_gated_scan(q, k, v, logf, s0):
    b, h, l, dk = q.shape
    dv = v.shape[-1]
    nc = l // HGRN_CHUNK
    causal = np.tril(np.ones((HGRN_CHUNK, HGRN_CHUNK), dtype=bool))

    def to_chunks(t):
        return jnp.moveaxis(t.reshape(b, h, nc, HGRN_CHUNK, t.shape[-1]), 2, 0)

    def step(state, inp):
        qc, kc, vc, gc = inp
        cum = jnp.cumsum(gc, axis=2)
        decay = jnp.exp(jnp.where(causal[:, :, None], cum[:, :, :, None, :] - cum[:, :, None, :, :], -jnp.inf))
        att = jnp.einsum('bhtk,bhsk,bhtsk->bhts', qc, kc, decay)
        o = jnp.einsum('bhts,bhsv->bhtv', att, vc) + jnp.einsum('bhtk,bhkv->bhtv', qc * jnp.exp(cum), state)
        last = cum[:, :, -1]
        state = jnp.exp(last)[..., None] * state + jnp.einsum('bhsk,bhsv->bhkv', kc * jnp.exp(last[:, :, None, :] - cum), vc)
        return state, o

    s_fin, o = lax.scan(step, s0, (to_chunks(q), to_chunks(k), to_chunks(v), to_chunks(logf)))
    return jnp.moveaxis(o, 0, 2).reshape(b, h, l, dv), s_fin


def hgrn2_bidir(q, fz_f, fz_b, v, lb_f, lb_b, s0_f, s0_b):
    dt = v.dtype
    q32, v32 = q.astype(F32), v.astype(F32)
    logf_f, k_f = hgrn2_gates(fz_f, lb_f)
    logf_b, k_b = hgrn2_gates(fz_b, lb_b)
    o_f, s_f = chunk_gated_scan(q32, k_f, v32, logf_f, s0_f)
    rev = lambda t: jnp.flip(t, axis=2)
    o_b, s_b = chunk_gated_scan(rev(q32), rev(k_b), rev(v32), rev(logf_b), s0_b)
    return (o_f + rev(o_b)).astype(dt), s_f, s_b


def hgrn_out(o, g, gain):
    om = merge(o)
    b, l, _ = om.shape
    on = rms_norm(om.reshape(b, l, HGRN_HEADS, HGRN_DV), gain.reshape(HGRN_HEADS, HGRN_DV)).reshape(b, l, W_DV)
    return on * jax.nn.silu(g)


def swiglu(h, w_in, w_out):
    a, gt = jnp.split(h @ w_in, 2, axis=-1)
    return (jax.nn.silu(a) * gt) @ w_out


def setup_inputs(seed: int = 0) -> dict:
    key = jax.random.key(seed)
    ks = jax.random.split(key, 20)
    nrm = lambda k, shape, s: jax.random.normal(k, shape, F32) * s
    return {
        'x': nrm(ks[0], (BATCH, SEQ, D_MODEL), 1.0),
        'c': nrm(ks[1], (BATCH, D_MODEL), 1.0),
        'ctx': nrm(ks[2], (BATCH, CTX_LEN, D_MODEL), 1.0),
        'c_ctx': nrm(ks[3], (D_MODEL,), 1.0),
        'w_ada': nrm(ks[4], (DEPTH, D_MODEL, 6 * D_MODEL), 0.5 * D_MODEL ** -0.5),
        'b_ada': nrm(ks[5], (DEPTH, 6 * D_MODEL), 0.02),
        'g_pre_mix': 1.0 + nrm(ks[6], (DEPTH, D_MODEL), 0.1),
        'g_post_mix': 1.0 + nrm(ks[7], (DEPTH, D_MODEL), 0.1),
        'g_pre_ffn': 1.0 + nrm(ks[8], (DEPTH, D_MODEL), 0.1),
        'g_post_ffn': 1.0 + nrm(ks[9], (DEPTH, D_MODEL), 0.1),
        'w_in': nrm(ks[10], (DEPTH, D_MODEL, D_IN), D_MODEL ** -0.5),
        'w_out': nrm(ks[11], (DEPTH, D_MIX, D_MODEL), D_MIX ** -0.5),
        'na_rpb': nrm(ks[12], (DEPTH, NA_HEADS, 2 * NA_WIN_R - 1, 2 * NA_WIN_C - 1), 0.2),
        'w_fnet': nrm(ks[13], (DEPTH, W_B, W_B), W_B ** -0.5),
        'swa_sink': nrm(ks[14], (DEPTH, SWA_HEADS), 1.0),
        'hgrn_lb_logits': nrm(ks[15], (2, DEPTH, W_DK), 1.0),
        'hgrn_norm_g': 1.0 + nrm(ks[16], (DEPTH, W_DV), 0.1),
        'w_ffn_in': nrm(ks[17], (DEPTH, D_MODEL, 2 * D_FF), D_MODEL ** -0.5),
        'w_ffn_out': nrm(ks[18], (DEPTH, D_FF, D_MODEL), D_FF ** -0.5),
    }


def reference(x, c, ctx, c_ctx, w_ada, b_ada, g_pre_mix, g_post_mix, g_pre_ffn, g_post_ffn, w_in, w_out,
              na_rpb, w_fnet, swa_sink, hgrn_lb_logits, hgrn_norm_g, w_ffn_in, w_ffn_out):
    b, n, _ = x.shape
    pos = np.arange(n)
    rows_pos, cols_pos = pos // GRID_W, pos % GRID_W
    lb_all = jnp.cumsum(jax.nn.softmax(hgrn_lb_logits.astype(F32), axis=1), axis=1)
    lb_all = lb_all - lb_all[:, :1]
    s_zero = jnp.zeros((b, HGRN_HEADS, HGRN_DK, HGRN_DV), F32)
    h_ctx = ctx
    for l in range(DEPTH):
        sh1, sc1, gt1, sh2, sc2, gt2 = jnp.split((jax.nn.silu(c) @ w_ada[l] + b_ada[l])[:, None, :], 6, axis=-1)
        csh1, csc1, cgt1, csh2, csc2, cgt2 = jnp.split((jax.nn.silu(c_ctx) @ w_ada[l] + b_ada[l])[None, None, :], 6, axis=-1)
        u = modulate(rms_norm(x, g_pre_mix[l]), sh1, sc1) @ w_in[l]
        uc = modulate(rms_norm(h_ctx, g_pre_mix[l]), csh1, csc1) @ w_in[l]
        qa, ka, va, ub, qs, ks, vs, qd, fdf, fdb, iv, gd = split_in(u)
        qa_c, ka_c, va_c, ub_c, qs_c, ks_c, vs_c, qd_c, fdf_c, fdb_c, iv_c, gd_c = split_in(uc)
        ka_c, va_c = heads(ka_c, NA_HEADS), heads(va_c, NA_HEADS)
        ks_c, vs_c = heads(ks_c, SWA_KV_HEADS), heads(vs_c, SWA_KV_HEADS)
        od_c, s_f, s_b = hgrn2_bidir(jax.nn.silu(heads(qd_c, HGRN_HEADS)), heads(fdf_c, HGRN_HEADS),
                                     heads(fdb_c, HGRN_HEADS), heads(iv_c, HGRN_HEADS),
                                     lb_all[0, l], lb_all[1, l], s_zero, s_zero)
        o_a = neighbourhood_attention(heads(qa, NA_HEADS), heads(ka, NA_HEADS), heads(va, NA_HEADS), ka_c, va_c, na_rpb[l])
        o_b = fourier_mix(ub) @ w_fnet[l]
        o_c = sliding_window_gqa(axial_rope(heads(qs, SWA_HEADS), rows_pos, cols_pos),
                                 axial_rope(heads(ks, SWA_KV_HEADS), rows_pos, cols_pos),
                                 heads(vs, SWA_KV_HEADS), ks_c, vs_c, swa_sink[l])
        od, _, _ = hgrn2_bidir(jax.nn.silu(heads(qd, HGRN_HEADS)), heads(fdf, HGRN_HEADS), heads(fdb, HGRN_HEADS),
                               heads(iv, HGRN_HEADS), lb_all[0, l], lb_all[1, l], s_f, s_b)
        o_d = hgrn_out(od, gd, hgrn_norm_g[l])
        mix = jnp.concatenate([merge(o_a), o_b, merge(o_c), o_d], axis=-1) @ w_out[l]
        x = x + gt1 * rms_norm(mix, g_post_mix[l])
        ffn = swiglu(modulate(rms_norm(x, g_pre_ffn[l]), sh2, sc2), w_ffn_in[l], w_ffn_out[l])
        x = x + gt2 * rms_norm(ffn, g_post_ffn[l])
        if l < DEPTH - 1:
            o_a_c = dense_attention(heads(qa_c, NA_HEADS), ka_c, va_c)
            o_b_c = fourier_mix(ub_c) @ w_fnet[l]
            o_c_c = dense_attention(heads(qs_c, SWA_HEADS), ks_c, vs_c, swa_sink[l])
            o_d_c = hgrn_out(od_c, gd_c, hgrn_norm_g[l])
            mix_c = jnp.concatenate([merge(o_a_c), o_b_c, merge(o_c_c), o_d_c], axis=-1) @ w_out[l]
            h_ctx = h_ctx + cgt1 * rms_norm(mix_c, g_post_mix[l])
            ffn_c = swiglu(modulate(rms_norm(h_ctx, g_pre_ffn[l]), csh2, csc2), w_ffn_in[l], w_ffn_out[l])
            h_ctx = h_ctx + cgt2 * rms_norm(ffn_c, g_post_ffn[l])
    return x
```

```python
import functools

import numpy as np
import jax
import jax.numpy as jnp
from jax import lax
from jax.experimental import pallas as pl
from jax.experimental.pallas import tpu as pltpu

F32, BF16 = jnp.float32, jnp.bfloat16

D_MODEL = 1024
SEQ = 8192
DEPTH = 2
GRID_W = 64
GRID_ROWS = SEQ // GRID_W
CTX = 256
HEAD_DIM = 64
NA_WIN_R, NA_WIN_C = 8, 16
SWA_WINDOW = 128
ROPE_THETA = 10000.0
EPS = 1e-6
D_FF = 2816
D_IN = 2816

TB = 256
T_ALL = CTX + SEQ
NBLK = T_ALL // TB
NA_KROWS = 12
NA_KEYS = NA_KROWS * GRID_W
SWA_KEYS = TB + 2 * SWA_WINDOW
CHUNK = 64
Q_SCALE = HEAD_DIM ** -0.5
VMEM_LIMIT = 56 * 1024 * 1024

ATT_W, F32_W = 1280, 1536


def _dot(a, b):
    return jnp.dot(a, b, preferred_element_type=F32)


def _dot_nt(a, b):
    return lax.dot_general(a, b, (((1,), (1,)), ((), ())), preferred_element_type=F32)


def _dot_tn(a, b):
    return lax.dot_general(a, b, (((0,), (0,)), ((), ())), preferred_element_type=F32)


def _cparams(sem):
    return pltpu.CompilerParams(dimension_semantics=sem, vmem_limit_bytes=VMEM_LIMIT)


def _const_spec(shape):
    nd = len(shape)
    return pl.BlockSpec(shape, lambda *_: (0,) * nd, pipeline_mode=pl.Buffered(1))


@functools.lru_cache(maxsize=None)
def _tables():
    t = {}
    lane = np.arange(128)
    d = lane % 64
    freq = ROPE_THETA ** (-(d % 16) / 16.0)
    pos = np.arange(SEQ)
    p = np.where(d[None, :] < 32, (pos // GRID_W)[:, None], (pos % GRID_W)[:, None]).astype(np.float64)
    ang = p * freq[None, :]
    sign = np.where((d % 32) < 16, -1.0, 1.0)
    cos = np.concatenate([np.ones((CTX, 128)), np.cos(ang)], axis=0)
    sin = np.concatenate([np.zeros((CTX, 128)), np.sin(ang) * sign[None, :]], axis=0)
    t["rope_cos"], t["rope_sin"] = cos.astype(np.float32), sin.astype(np.float32)

    idxs, valids = [], []
    a, cq = np.arange(4), np.arange(GRID_W)
    rho, ck = np.arange(NA_KROWS), np.arange(GRID_W)
    for r0 in (0, 8, GRID_ROWS - 4):
        kr0 = int(np.clip(r0 - 4, 0, GRID_ROWS - NA_KROWS))
        rq, rk = r0 + a, kr0 + rho
        start = np.clip(rq - NA_WIN_R // 2, 0, GRID_ROWS - NA_WIN_R)
        vr = (rk[None, :] >= start[:, None]) & (rk[None, :] < start[:, None] + NA_WIN_R)
        cs = np.clip(cq - NA_WIN_C // 2, 0, GRID_W - NA_WIN_C)
        vc = (ck[None, :] >= cs[:, None]) & (ck[None, :] < cs[:, None] + NA_WIN_C)
        dr = np.clip(rk[None, :] - rq[:, None] + NA_WIN_R - 1, 0, 2 * NA_WIN_R - 2)
        dc = np.clip(ck[None, :] - cq[:, None], -(NA_WIN_C - 1), NA_WIN_C - 1) + NA_WIN_C - 1
        idx = dr[:, None, :, None] * (2 * NA_WIN_C - 1) + dc[None, :, None, :]
        val = vr[:, None, :, None] & vc[None, :, None, :]
        idxs.append(idx.reshape(TB, NA_KEYS))
        valids.append(val.reshape(TB, NA_KEYS))
    t["na_idx"] = np.stack(idxs).astype(np.int32)
    t["na_valid"] = np.stack(valids)

    def cs_tab(num, den):
        ang = 2.0 * np.pi * (num % den) / den
        return np.cos(ang), np.sin(ang)

    k1, t1 = np.arange(64)[:, None], np.arange(64)[None, :]
    c64, s64 = cs_tab(k1 * t1, 64)
    t["f64"] = np.concatenate([c64, -s64], axis=0).astype(np.float32)
    k2, t2 = np.arange(128)[:, None], np.arange(128)[None, :]
    g = []
    for kk in range(64):
        gc, gs = cs_tab((64 * k2 + kk) * t2, SEQ)
        g.append(np.block([[gc, gs], [-gs, gc]]))
    t["g_stage2"] = np.stack(g).astype(np.float32)
    kc, tc = np.arange(CTX)[:, None], np.arange(CTX)[None, :]
    cc, sc = cs_tab(kc * tc, CTX)
    t["f_ctx"] = np.concatenate([cc, -sc], axis=0).astype(np.float32)
    ch = np.arange(256)
    same = (ch[:, None] // 64) == (ch[None, :] // 64)
    c_ch, s_ch = cs_tab((ch[:, None] % 64) * (ch[None, :] % 64), 64)
    t["chan_cs"] = np.concatenate([np.where(same, c_ch, 0.0), np.where(same, s_ch, 0.0)], axis=0).astype(np.float32)
    t["head_ones"] = same.astype(np.float32)

    balls, lmasks = [], []
    for rev in (False, True):
        tau = (CHUNK - 1 - np.arange(CHUNK)) if rev else np.arange(CHUNK)
        tt, tj = tau[:, None], tau[None, :]
        blocks = [(tj <= tt), (tj > tt)]
        masks = [tt == tj]
        for h in (32, 16, 8, 4, 2, 1):
            up = ((tt // h) % 2) == 1
            mid = (tt // (2 * h)) * (2 * h) + h - 1
            blocks.append(np.where(up, (tj > mid) & (tj <= tt), (tj > tt) & (tj <= mid)))
            up_s = ((tj // h) % 2) == 1
            masks.append((tt // (2 * h) == tj // (2 * h)) & up & ~up_s)
        balls.append(np.concatenate(blocks, axis=0))
        lmasks.append(np.stack([np.tile(m, (4, 1)) for m in masks]))
    t["hgrn_b"] = np.stack(balls).astype(np.float32)
    t["hgrn_mask"] = np.stack(lmasks).astype(np.float32)
    return t


def _ada_kernel(cb_ref, w_ref, b_ref, o_ref):
    tn = w_ref.shape[1]
    nrep = tn // 128

    def body(kc, accs):
        k0 = pl.multiple_of(kc * 8, 8)
        wt = w_ref[pl.ds(k0, 8), :]
        out = []
        for r in range(3):
            s = cb_ref[r, pl.ds(k0, 8), :]
            s = s * jax.nn.sigmoid(s)
            out.append(accs[r] + wt * jnp.concatenate([s] * nrep, axis=1))
        return tuple(out)

    accs = lax.fori_loop(0, D_MODEL // 8, body, tuple(jnp.zeros((8, tn), F32) for _ in range(3)))
    for r in range(3):
        o_ref[r] = jnp.sum(accs[r], axis=0, keepdims=True) + b_ref[...]


def _ada_call(cb, w_ada, b_ada):
    tn = 1536
    return pl.pallas_call(
        _ada_kernel,
        grid=(DEPTH, 6 * D_MODEL // tn),
        in_specs=[
            pl.BlockSpec((3, D_MODEL, 128), lambda l, j: (0, 0, 0)),
            pl.BlockSpec((None, D_MODEL, tn), lambda l, j: (l, 0, j)),
            pl.BlockSpec((None, 1, tn), lambda l, j: (l, 0, j)),
        ],
        out_specs=pl.BlockSpec((None, 3, 1, tn), lambda l, j: (l, 0, 0, j)),
        out_shape=jax.ShapeDtypeStruct((DEPTH, 3, 1, 6 * D_MODEL), F32),
        compiler_params=_cparams(("arbitrary", "arbitrary")),
        name="adaln",
    )(cb, w_ada, b_ada.reshape(DEPTH, 1, 6 * D_MODEL))


def _mod_spec(col, off):
    return pl.BlockSpec((None, 1, D_MODEL), lambda b, i: (jnp.where(i + off == 0, 2, b), 0, col))


def _rms(x):
    return x * lax.rsqrt(jnp.mean(x * x, axis=-1, keepdims=True) + EPS)


def _proj_kernel(x_ref, g_ref, sh_ref, sc_ref, w_ref, cos_ref, sin_ref, att_ref, f32_ref):
    h = ((_rms(x_ref[...]) * g_ref[...]) * (1.0 + sc_ref[...]) + sh_ref[...]).astype(BF16)
    cos, sin = cos_ref[...], sin_ref[...]
    first = (lax.broadcasted_iota(jnp.int32, (TB, 128), 1) % 32) < 16

    def rope(v):
        return v * cos + jnp.where(first, pltpu.roll(v, 112, 1), pltpu.roll(v, 16, 1)) * sin

    a = _dot(h, w_ref[:, 0:768])
    att_ref[:, 0:256] = (a[:, 0:256] * Q_SCALE).astype(BF16)
    att_ref[:, 256:768] = a[:, 256:768].astype(BF16)
    f32_ref[:, 0:256] = _dot(h, w_ref[:, 768:1024])
    s = _dot(h, w_ref[:, 1024:1536])
    att_ref[:, 768:896] = (rope(s[:, 0:128]) * Q_SCALE).astype(BF16)
    att_ref[:, 896:1024] = (rope(s[:, 128:256]) * Q_SCALE).astype(BF16)
    att_ref[:, 1024:1152] = rope(s[:, 256:384]).astype(BF16)
    att_ref[:, 1152:1280] = s[:, 384:512].astype(BF16)
    f32_ref[:, 256:1536] = _dot(h, w_ref[:, 1536:2816])


def _proj_call(xs, g, mods, w_in, cos, sin):
    b = xs.shape[0]
    return pl.pallas_call(
        _proj_kernel,
        grid=(b, NBLK),
        in_specs=[
            pl.BlockSpec((None, TB, D_MODEL), lambda b, i: (b, i, 0)),
            _const_spec((1, D_MODEL)),
            _mod_spec(0, 0), _mod_spec(1, 0),
            _const_spec((D_MODEL, D_IN)),
            pl.BlockSpec((TB, 128), lambda b, i: (i, 0)),
            pl.BlockSpec((TB, 128), lambda b, i: (i, 0)),
        ],
        out_specs=[
            pl.BlockSpec((None, TB, ATT_W), lambda b, i: (b, i, 0)),
            pl.BlockSpec((None, TB, F32_W), lambda b, i: (b, i, 0)),
        ],
        out_shape=[jax.ShapeDtypeStruct((b, T_ALL, ATT_W), BF16), jax.ShapeDtypeStruct((b, T_ALL, F32_W), F32)],
        compiler_params=_cparams(("arbitrary", "arbitrary")),
        name="proj_in",
    )(xs, g, mods, mods, w_in, cos, sin)


def _na_kernel(q_ref, k_ref, v_ref, bias_ref, o_ref, *, off):
    i = pl.program_id(1) + off
    kr0 = jnp.clip((i - 1) * 4 - 4, 0, GRID_ROWS - NA_KROWS)
    kstart = pl.multiple_of(CTX + kr0 * GRID_W, GRID_W)
    lane = lax.broadcasted_iota(jnp.int32, (TB, 128), 1)
    for hp in range(2):
        cs = slice(hp * 128, (hp + 1) * 128)
        q2 = q_ref[:, cs]
        kw, vw = k_ref[pl.ds(kstart, NA_KEYS), cs], v_ref[pl.ds(kstart, NA_KEYS), cs]
        kc, vc = k_ref[0:CTX, cs], v_ref[0:CTX, cs]
        outs = []
        for j in range(2):
            qm = jnp.where((lane >= 64 * j) & (lane < 64 * (j + 1)), q2, jnp.zeros_like(q2))
            s_nb = _dot_nt(qm, kw) + bias_ref[2 * hp + j]
            s_c = _dot_nt(qm, kc)
            m = jnp.maximum(jnp.max(s_nb, axis=-1, keepdims=True), jnp.max(s_c, axis=-1, keepdims=True))
            p_nb, p_c = jnp.exp(s_nb - m), jnp.exp(s_c - m)
            l = jnp.sum(p_nb, axis=-1, keepdims=True) + jnp.sum(p_c, axis=-1, keepdims=True)
            o = _dot(p_nb.astype(BF16), vw) + _dot(p_c.astype(BF16), vc)
            outs.append(o * (1.0 / l))
        o_ref[:, cs] = jnp.where(lane < 64, outs[0], outs[1]).astype(BF16)


def _na_call(u_att, bias, with_ctx):
    b = u_att.shape[0]
    off = 0 if with_ctx else 1
    nb = NBLK - off

    def variant(b, i):
        ib = i + off
        return (jnp.where(ib == 0, 3, jnp.where(ib == 1, 0, jnp.where(ib == NBLK - 1, 2, 1))), 0, 0, 0)

    return pl.pallas_call(
        functools.partial(_na_kernel, off=off),
        grid=(b, nb),
        in_specs=[
            pl.BlockSpec((None, TB, 256), lambda b, i: (b, i + off, 0)),
            pl.BlockSpec((None, T_ALL, 256), lambda b, i: (b, 0, 1)),
            pl.BlockSpec((None, T_ALL, 256), lambda b, i: (b, 0, 2)),
            pl.BlockSpec((None, 4, TB, NA_KEYS), variant),
        ],
        out_specs=pl.BlockSpec((None, TB, 256), lambda b, i: (b, i, 0)),
        out_shape=jax.ShapeDtypeStruct((b, nb * TB, 256), BF16),
        compiler_params=_cparams(("arbitrary", "arbitrary")),
        name="na_attn",
    )(u_att, u_att, u_att, bias)


def _swa_kernel(sink_ref, q_ref, k_ref, v_ref, o_ref, *, off):
    i = pl.program_id(1) + off
    q0 = (i - 1) * TB
    ks0 = jnp.clip(q0 - SWA_WINDOW, 0, SEQ - SWA_KEYS)
    kstart = pl.multiple_of(CTX + ks0, SWA_WINDOW)
    rel = (lax.broadcasted_iota(jnp.int32, (TB, SWA_KEYS), 1)
           - lax.broadcasted_iota(jnp.int32, (TB, SWA_KEYS), 0) + (ks0 - q0))
    ok = (jnp.abs(rel) <= SWA_WINDOW) & (i > 0)
    kw, vw = k_ref[pl.ds(kstart, SWA_KEYS), :], v_ref[pl.ds(kstart, SWA_KEYS), :]
    kc, vc = k_ref[0:CTX, :], v_ref[0:CTX, :]
    lane = lax.broadcasted_iota(jnp.int32, (TB, 128), 1)
    for g in range(2):
        q2 = q_ref[:, g * 128:(g + 1) * 128].astype(F32)
        gm = (lane >= 64 * g) & (lane < 64 * (g + 1))
        outs = []
        for j in range(2):
            qa = q2 if j == g else pltpu.roll(q2, 64, 1)
            qa = jnp.where(gm, qa, 0.0).astype(BF16)
            s_l = jnp.where(ok, _dot_nt(qa, kw), -jnp.inf)
            s_c = _dot_nt(qa, kc)
            sink = sink_ref[2 * g + j]
            m = jnp.maximum(jnp.maximum(jnp.max(s_l, axis=-1, keepdims=True),
                                        jnp.max(s_c, axis=-1, keepdims=True)), sink)
            p_l, p_c = jnp.exp(s_l - m), jnp.exp(s_c - m)
            l = jnp.sum(p_l, axis=-1, keepdims=True) + jnp.sum(p_c, axis=-1, keepdims=True) + jnp.exp(sink - m)
            o = (_dot(p_l.astype(BF16), vw) + _dot(p_c.astype(BF16), vc)) * (1.0 / l)
            outs.append(o if j == g else pltpu.roll(o, 64, 1))
        o_ref[:, g * 128:(g + 1) * 128] = jnp.where(lane < 64, outs[0], outs[1]).astype(BF16)


def _swa_call(u_att, sink, with_ctx):
    b = u_att.shape[0]
    off = 0 if with_ctx else 1
    nb = NBLK - off
    return pl.pallas_call(
        functools.partial(_swa_kernel, off=off),
        grid=(b, nb),
        in_specs=[
            pl.BlockSpec(memory_space=pltpu.SMEM),
            pl.BlockSpec((None, TB, 256), lambda b, i: (b, i + off, 3)),
            pl.BlockSpec((None, T_ALL, 128), lambda b, i: (b, 0, 8)),
            pl.BlockSpec((None, T_ALL, 128), lambda b, i: (b, 0, 9)),
        ],
        out_specs=pl.BlockSpec((None, TB, 256), lambda b, i: (b, i, 0)),
        out_shape=jax.ShapeDtypeStruct((b, nb * TB, 256), BF16),
        compiler_params=_cparams(("arbitrary", "arbitrary")),
        name="swa_attn",
    )(sink, u_att, u_att, u_att)


def _fnet_kernel(x_ref, f64_ref, g_ref, fc_ref, p_ref, q_ref, y_ref):
    sc_lat, sc_ctx = (64.0 * SEQ) ** -0.5, (64.0 * CTX) ** -0.5
    pq = _dot(fc_ref[...], x_ref[0:CTX, :].astype(BF16))
    p_ref[0:CTX, :] = pq[0:CTX] * sc_ctx
    q_ref[0:CTX, :] = pq[CTX:2 * CTX] * sc_ctx
    f64 = f64_ref[...]

    def stage1(t2, carry):
        xs = x_ref[pl.ds(CTX + t2, 64, stride=128), :].astype(BF16)
        y_ref[pl.ds(pl.multiple_of(t2 * 128, 128), 128), :] = _dot(f64, xs)
        return carry

    lax.fori_loop(0, 128, stage1, 0)

    def stage2(k1, carry):
        yre = y_ref[pl.ds(k1, 128, stride=128), :]
        yim = y_ref[pl.ds(64 + k1, 128, stride=128), :]
        ycat = jnp.concatenate([yre, yim], axis=0).astype(BF16)
        pq = _dot(g_ref[k1], ycat)
        p_ref[pl.ds(CTX + k1, 128, stride=64), :] = pq[0:128] * sc_lat
        q_ref[pl.ds(CTX + k1, 128, stride=64), :] = pq[128:256] * sc_lat
        return carry

    lax.fori_loop(0, 64, stage2, 0)


def _fnet_call(u_f32, f64, g2, fc):
    b = u_f32.shape[0]
    spec = pl.BlockSpec((None, T_ALL, 128), lambda b, j: (b, 0, j))
    return pl.pallas_call(
        _fnet_kernel,
        grid=(b, 2),
        in_specs=[spec, _const_spec((128, 64)), _const_spec((64, 256, 256)), _const_spec((2 * CTX, CTX))],
        out_specs=[spec, spec],
        out_shape=[jax.ShapeDtypeStruct((b, T_ALL, 256), F32)] * 2,
        scratch_shapes=[pltpu.VMEM((128 * 128, 128), F32)],
        compiler_params=_cparams(("arbitrary", "arbitrary")),
        name="fnet_dft",
    )(u_f32, f64, g2, fc)


def _hgrn_chunk(qd, fz, v, st_ref, gl, ball, lmask_ref, d, rev):
    q = qd * jax.nn.sigmoid(qd)
    sp = jnp.log1p(jnp.exp(-jnp.abs(fz)))
    b_ = gl[1:2] + jnp.minimum(fz, 0.0) - sp
    a_ = gl[0:1]
    logf = jnp.maximum(a_, b_) + jnp.log1p(jnp.exp(-jnp.abs(a_ - b_)))
    kk = jnp.exp(gl[1:2] + jnp.minimum(-fz, 0.0) - sp)

    h1 = logf.astype(BF16)
    r1 = logf - h1.astype(F32)
    h2 = r1.astype(BF16)
    h3 = (r1 - h2.astype(F32)).astype(BF16)
    e = jnp.exp(_dot(ball, h1) + _dot(ball, h2) + _dot(ball, h3))

    lane = lax.broadcasted_iota(jnp.int32, (CHUNK, 256), 1)
    row = lax.broadcasted_iota(jnp.int32, (CHUNK, 256), 0)
    tau = (CHUNK - 1 - row) if rev else row
    hms = [(lane >= 64 * h) & (lane < 64 * (h + 1)) for h in range(4)]

    def stack(w):
        return jnp.concatenate([jnp.where(hm, w, jnp.zeros_like(w)) for hm in hms], axis=0)

    att = _dot_nt(stack(q.astype(BF16)), kk.astype(BF16)) * lmask_ref[d, 0]
    for li, sh in enumerate((5, 4, 3, 2, 1, 0)):
        up = ((tau >> sh) & 1) == 1
        w = (jnp.where(up, q, kk) * e[128 + 64 * li:192 + 64 * li]).astype(BF16)
        att = att + _dot_nt(stack(w), w) * lmask_ref[d, 1 + li]
    vb = v.astype(BF16)
    ost = _dot(att.astype(BF16), vb)
    o = jnp.zeros((CHUNK, 256), F32)
    for h in range(4):
        o = o + jnp.where(hms[h], ost[64 * h:64 * (h + 1)], 0.0)

    st = st_ref[...]
    o = o + _dot_nt((q * e[0:64]).astype(BF16), st.astype(BF16))
    tl = 0 if rev else CHUNK - 1
    e_last = e[tl:tl + 1]
    ktv = _dot_tn(vb, (kk * e[64:128]).astype(BF16))
    r2 = lax.broadcasted_iota(jnp.int32, (256, 256), 0) // 64
    c2 = lax.broadcasted_iota(jnp.int32, (256, 256), 1) // 64
    st_ref[...] = st * e_last + jnp.where(r2 == c2, ktv, 0.0)
    return o


def _hgrn_kernel(qf_ref, ff_ref, vf_ref, qb_ref, fb_ref, vb_ref, gl_ref, ball_ref, lmask_ref,
                 of_ref, ob_ref, sf_ref, sb_ref):
    @pl.when(pl.program_id(1) == 0)
    def _():
        sf_ref[...] = jnp.zeros_like(sf_ref)
        sb_ref[...] = jnp.zeros_like(sb_ref)

    def body(c, carry):
        rf = pl.ds(pl.multiple_of(c * CHUNK, CHUNK), CHUNK)
        rb = pl.ds(pl.multiple_of((TB // CHUNK - 1 - c) * CHUNK, CHUNK), CHUNK)
        of_ref[rf, :] = _hgrn_chunk(qf_ref[rf, :], ff_ref[rf, :], vf_ref[rf, :], sf_ref,
                                    gl_ref[0], ball_ref[0], lmask_ref, 0, False)
        ob_ref[rb, :] = _hgrn_chunk(qb_ref[rb, :], fb_ref[rb, :], vb_ref[rb, :], sb_ref,
                                    gl_ref[1], ball_ref[1], lmask_ref, 1, True)
        return carry

    lax.fori_loop(0, TB // CHUNK, body, 0)


def _hgrn_call(u_f32, gl, ball, lmask):
    b = u_f32.shape[0]

    def fwd(col):
        return pl.BlockSpec((None, TB, 256), lambda b, i: (b, i, col))

    def bwd(col):
        return pl.BlockSpec((None, TB, 256), lambda b, i: (b, jnp.where(i == 0, 0, NBLK - i), col))

    return pl.pallas_call(
        _hgrn_kernel,
        grid=(b, NBLK),
        in_specs=[fwd(1), fwd(2), fwd(4), bwd(1), bwd(3), bwd(4),
                  _const_spec((2, 2, 256)), _const_spec((2, 8 * CHUNK, CHUNK)), _const_spec((2, 7, 4 * CHUNK, CHUNK))],
        out_specs=[fwd(0), bwd(0)],
        out_shape=[jax.ShapeDtypeStruct((b, T_ALL, 256), F32)] * 2,
        scratch_shapes=[pltpu.VMEM((256, 256), F32), pltpu.VMEM((256, 256), F32)],
        compiler_params=_cparams(("arbitrary", "arbitrary")),
        name="hgrn_scan",
    )(u_f32, u_f32, u_f32, u_f32, u_f32, u_f32, gl, ball, lmask)


def _outproj_kernel(x_ref, oa_ref, p_ref, q_ref, oc_ref, of_ref, ob_ref, gd_ref, hg_ref, hh_ref, cs_ref,
                    wf_ref, wo_ref, gp_ref, gt_ref, o_ref):
    z = _dot(p_ref[...].astype(BF16), cs_ref[0:256, :]) + _dot(q_ref[...].astype(BF16), cs_ref[256:512, :])
    o_b = _dot(z.astype(BF16), wf_ref[...])
    od = of_ref[...] + ob_ref[...]
    sq = od * od
    hi = sq.astype(BF16)
    lo = (sq - hi.astype(F32)).astype(BF16)
    ms = (_dot(hi, hh_ref[...]) + _dot(lo, hh_ref[...])) * (1.0 / HEAD_DIM)
    gd = gd_ref[...]
    o_d = od * lax.rsqrt(ms + EPS) * hg_ref[...] * (gd * jax.nn.sigmoid(gd))
    y = (_dot(oa_ref[...], wo_ref[0:256, :]) + _dot(o_b.astype(BF16), wo_ref[256:512, :])
         + _dot(oc_ref[...], wo_ref[512:768, :]) + _dot(o_d.astype(BF16), wo_ref[768:1024, :]))
    o_ref[...] = x_ref[...] + gt_ref[...] * (_rms(y) * gp_ref[...])


def _outproj_call(xs, o_a, p, q, o_c, o_f, o_bk, u_f32, hg, hh, cs, w_fnet, w_out, g_post, mods, with_ctx):
    b = xs.shape[0]
    off = 0 if with_ctx else 1
    nb = NBLK - off
    full = pl.BlockSpec((None, TB, 256), lambda b, i: (b, i + off, 0))
    part = pl.BlockSpec((None, TB, 256), lambda b, i: (b, i, 0))
    return pl.pallas_call(
        _outproj_kernel,
        grid=(b, nb),
        in_specs=[
            pl.BlockSpec((None, TB, D_MODEL), lambda b, i: (b, i + off, 0)),
            part, full, full, part, full, full,
            pl.BlockSpec((None, TB, 256), lambda b, i: (b, i + off, 5)),
            _const_spec((1, 256)), _const_spec((256, 256)), _const_spec((512, 256)),
            _const_spec((256, 256)), _const_spec((D_MODEL, D_MODEL)), _const_spec((1, D_MODEL)),
            _mod_spec(2, off),
        ],
        out_specs=pl.BlockSpec((None, TB, D_MODEL), lambda b, i: (b, i, 0)),
        out_shape=jax.ShapeDtypeStruct((b, nb * TB, D_MODEL), F32),
        compiler_params=_cparams(("arbitrary", "arbitrary")),
        name="out_proj",
    )(xs, o_a, p, q, o_c, o_f, o_bk, u_f32, hg, hh, cs, w_fnet, w_out, g_post, mods)


def _ffn_kernel(x_ref, g_ref, sh_ref, sc_ref, w1_ref, w2_ref, gp_ref, gt_ref, o_ref):
    x = x_ref[...]
    h = ((_rms(x) * g_ref[...]) * (1.0 + sc_ref[...]) + sh_ref[...]).astype(BF16)
    acc = jnp.zeros((TB, D_MODEL), F32)
    fc = 256
    for c in range(D_FF // fc):
        a = _dot(h, w1_ref[:, c * fc:(c + 1) * fc])
        g = _dot(h, w1_ref[:, D_FF + c * fc:D_FF + (c + 1) * fc])
        act = (a * jax.nn.sigmoid(a) * g).astype(BF16)
        acc = acc + _dot(act, w2_ref[c * fc:(c + 1) * fc, :])
    o_ref[...] = x + gt_ref[...] * (_rms(acc) * gp_ref[...])


def _ffn_call(xs, g_pre, w1, w2, g_post, mods, with_ctx):
    b, t, _ = xs.shape
    off = 0 if with_ctx else 1
    return pl.pallas_call(
        _ffn_kernel,
        grid=(b, t // TB),
        in_specs=[
            pl.BlockSpec((None, TB, D_MODEL), lambda b, i: (b, i, 0)),
            _const_spec((1, D_MODEL)),
            _mod_spec(3, off), _mod_spec(4, off),
            _const_spec((D_MODEL, 2 * D_FF)), _const_spec((D_FF, D_MODEL)), _const_spec((1, D_MODEL)),
            _mod_spec(5, off),
        ],
        out_specs=pl.BlockSpec((None, TB, D_MODEL), lambda b, i: (b, i, 0)),
        out_shape=jax.ShapeDtypeStruct(xs.shape, F32),
        compiler_params=_cparams(("arbitrary", "arbitrary")),
        name="ffn",
    )(xs, g_pre, mods, mods, w1, w2, g_post, mods)


def kernel(x, c, ctx, c_ctx, w_ada, b_ada, g_pre_mix, g_post_mix, g_pre_ffn, g_post_ffn, w_in, w_out,
           na_rpb, w_fnet, swa_sink, hgrn_lb_logits, hgrn_norm_g, w_ffn_in, w_ffn_out):
    tb = _tables()
    bf = lambda name: jnp.asarray(tb[name]).astype(BF16)
    cos, sin = jnp.asarray(tb["rope_cos"]), jnp.asarray(tb["rope_sin"])
    f64, g2, fc, chan_cs, hh = bf("f64"), bf("g_stage2"), bf("f_ctx"), bf("chan_cs"), bf("head_ones")
    ball, lmask = bf("hgrn_b"), jnp.asarray(tb["hgrn_mask"])
    na_idx, na_valid = jnp.asarray(tb["na_idx"]), jnp.asarray(tb["na_valid"])

    cc = jnp.concatenate([c, c_ctx[None, :]], axis=0)
    mods = _ada_call(jnp.broadcast_to(cc[:, :, None], (3, D_MODEL, 128)), w_ada, b_ada)

    lb = jnp.cumsum(jax.nn.softmax(hgrn_lb_logits.astype(F32), axis=1), axis=1)
    lb = lb - lb[:, :1]
    gl_all = jnp.stack([jnp.log(lb), jnp.log1p(-lb)], axis=2)

    xs = jnp.concatenate([ctx, x], axis=1)
    for l in range(DEPTH):
        with_ctx = l < DEPTH - 1
        row = lambda a: a[l][None, :]
        u_att, u_f32 = _proj_call(xs, row(g_pre_mix), mods[l], w_in[l].astype(BF16), cos, sin)

        rpb = na_rpb[l].reshape(4, -1)
        bias = jnp.where(na_valid[None], rpb[:, na_idx], -jnp.inf)
        bias = jnp.concatenate([bias, jnp.full((4, 1, TB, NA_KEYS), -jnp.inf, F32)], axis=1)
        o_a = _na_call(u_att, jnp.transpose(bias, (1, 0, 2, 3)), with_ctx)
        o_c = _swa_call(u_att, swa_sink[l], with_ctx)
        p, q = _fnet_call(u_f32, f64, g2, fc)
        o_f, o_bk = _hgrn_call(u_f32, gl_all[:, l], ball, lmask)

        xs = _outproj_call(xs, o_a, p, q, o_c, o_f, o_bk, u_f32, row(hgrn_norm_g), hh, chan_cs,
                           w_fnet[l].astype(BF16), w_out[l].astype(BF16), row(g_post_mix), mods[l], with_ctx)
        xs = _ffn_call(xs, row(g_pre_ffn), w_ffn_in[l].astype(BF16), w_ffn_out[l].astype(BF16),
                       row(g_post_ffn), mods[l], with_ctx)
    return xs
```

```python
import functools

import numpy as np
import jax
import jax.numpy as jnp
from jax import lax
from jax.experimental import pallas as pl
from jax.experimental.pallas import tpu as pltpu

F32, BF16 = jnp.float32, jnp.bfloat16

D_MODEL = 1024
SEQ = 8192
DEPTH = 2
GRID_W = 64
GRID_ROWS = SEQ // GRID_W
CTX = 256
HEAD_DIM = 64
NA_WIN_R, NA_WIN_C = 8, 16
SWA_WINDOW = 128
ROPE_THETA = 10000.0
EPS = 1e-6
D_FF = 2816
D_IN = 2816

TB = 256
T_ALL = SEQ + CTX
NBLK = T_ALL // TB
CTX_BLK = SEQ // TB
TM_ALL = 768
TM_LAT = 1024
NA_KROWS = 12
NA_KEYS = NA_KROWS * GRID_W
NA_DR_PAD = 4
NA_DR_ROWS = 24
SWA_KEYS = TB + 2 * SWA_WINDOW
CHUNK = 64
NCHUNK = TB // CHUNK
HGRN_FAST_RANGE = 60.0
Q_SCALE = HEAD_DIM ** -0.5
VMEM_LIMIT = 56 * 1024 * 1024

ATT_W, F32_W = 1280, 1536


def _dot(a, b):
    return jnp.dot(a, b, preferred_element_type=F32)


def _dot_nt(a, b):
    return lax.dot_general(a, b, (((1,), (1,)), ((), ())), preferred_element_type=F32)


def _dot_tn(a, b):
    return lax.dot_general(a, b, (((0,), (0,)), ((), ())), preferred_element_type=F32)


def _cparams(sem):
    return pltpu.CompilerParams(dimension_semantics=sem, vmem_limit_bytes=VMEM_LIMIT)


def _const_spec(shape):
    nd = len(shape)
    return pl.BlockSpec(shape, lambda *_: (0,) * nd, pipeline_mode=pl.Buffered(1))


@functools.lru_cache(maxsize=None)
def _tables():
    t = {}
    lane = np.arange(128)
    d = lane % 64
    freq = ROPE_THETA ** (-(d % 16) / 16.0)
    pos = np.arange(SEQ)
    p = np.where(d[None, :] < 32, (pos // GRID_W)[:, None], (pos % GRID_W)[:, None]).astype(np.float64)
    ang = p * freq[None, :]
    sign = np.where((d % 32) < 16, -1.0, 1.0)
    cos = np.concatenate([np.cos(ang), np.ones((CTX, 128))], axis=0)
    sin = np.concatenate([np.sin(ang) * sign[None, :], np.zeros((CTX, 128))], axis=0)
    t["rope_cos"], t["rope_sin"] = cos.astype(np.float32), sin.astype(np.float32)

    masks = []
    a, cq = np.arange(4), np.arange(GRID_W)
    rho, ck = np.arange(NA_KROWS), np.arange(GRID_W)
    for r0 in (0, 8, GRID_ROWS - 4):
        kr0 = int(np.clip(r0 - 4, 0, GRID_ROWS - NA_KROWS))
        rq, rk = r0 + a, kr0 + rho
        start = np.clip(rq - NA_WIN_R // 2, 0, GRID_ROWS - NA_WIN_R)
        vr = (rk[None, :] >= start[:, None]) & (rk[None, :] < start[:, None] + NA_WIN_R)
        cs = np.clip(cq - NA_WIN_C // 2, 0, GRID_W - NA_WIN_C)
        vc = (ck[None, :] >= cs[:, None]) & (ck[None, :] < cs[:, None] + NA_WIN_C)
        val = vr[:, None, :, None] & vc[None, :, None, :]
        masks.append(np.where(val, 0.0, -np.inf).reshape(TB, NA_KEYS))
    masks.append(np.full((TB, NA_KEYS), -np.inf))
    t["na_mask"] = np.stack(masks).astype(np.float32)
    n = np.arange(128)
    clipc = lambda m: np.clip(m, -(NA_WIN_C - 1), NA_WIN_C - 1) + NA_WIN_C - 1
    t["na_lane_l"] = clipc(np.where(n < 64, n, n - 128)).astype(np.int32)
    t["na_lane_r"] = clipc(n - 64).astype(np.int32)

    def cs_tab(num, den):
        ang = 2.0 * np.pi * (num % den) / den
        return np.cos(ang), np.sin(ang)

    k1, t1 = np.arange(64)[:, None], np.arange(64)[None, :]
    c64, s64 = cs_tab(k1 * t1, 64)
    t["f64"] = np.concatenate([c64, -s64], axis=0).astype(np.float32)
    k2, t2 = np.arange(128)[:, None], np.arange(128)[None, :]
    g = []
    for kk in range(64):
        gc, gs = cs_tab((64 * k2 + kk) * t2, SEQ)
        g.append(np.block([[gc, gs], [-gs, gc]]))
    t["g_stage2"] = np.stack(g).astype(np.float32)
    kc, tc = np.arange(CTX)[:, None], np.arange(CTX)[None, :]
    cc, sc = cs_tab(kc * tc, CTX)
    t["f_ctx"] = np.concatenate([cc, -sc], axis=0).astype(np.float32)
    ch = np.arange(256)
    same = (ch[:, None] // 64) == (ch[None, :] // 64)
    c_ch, s_ch = cs_tab((ch[:, None] % 64) * (ch[None, :] % 64), 64)
    t["chan_cs"] = np.concatenate([np.where(same, c_ch, 0.0), np.where(same, s_ch, 0.0)], axis=0).astype(np.float32)
    t["head_ones"] = same.astype(np.float32)

    balls, lmasks, bblk = [], [], []
    for rev in (False, True):
        tau = (CHUNK - 1 - np.arange(CHUNK)) if rev else np.arange(CHUNK)
        tt, tj = tau[:, None], tau[None, :]
        blocks = [(tj <= tt), (tj > tt)]
        masks = [tt == tj]
        for h in (32, 16, 8, 4, 2, 1):
            up = ((tt // h) % 2) == 1
            mid = (tt // (2 * h)) * (2 * h) + h - 1
            blocks.append(np.where(up, (tj > mid) & (tj <= tt), (tj > tt) & (tj <= mid)))
            up_s = ((tj // h) % 2) == 1
            masks.append((tt // (2 * h) == tj // (2 * h)) & up & ~up_s)
        balls.append(np.concatenate(blocks, axis=0))
        lmasks.append(np.stack([np.tile(m, (4, 1)) for m in masks]))
        bblk.append(np.kron(np.eye(NCHUNK), blocks[0]))
    t["hgrn_b"] = np.stack(balls).astype(np.float32)
    t["hgrn_mask"] = np.stack(lmasks).astype(np.float32)
    t["hgrn_bblk"] = np.stack(bblk).astype(np.float32)
    return t


def _ada_kernel(cb_ref, w_ref, b_ref, o_ref):
    tn = w_ref.shape[1]
    nrep = tn // 128

    def body(kc, accs):
        k0 = pl.multiple_of(kc * 8, 8)
        wt = w_ref[pl.ds(k0, 8), :]
        out = []
        for r in range(3):
            s = cb_ref[r, pl.ds(k0, 8), :]
            s = s * jax.nn.sigmoid(s)
            out.append(accs[r] + wt * jnp.concatenate([s] * nrep, axis=1))
        return tuple(out)

    accs = lax.fori_loop(0, D_MODEL // 8, body, tuple(jnp.zeros((8, tn), F32) for _ in range(3)))
    for r in range(3):
        o_ref[r] = jnp.sum(accs[r], axis=0, keepdims=True) + b_ref[...]


def _ada_call(cb, w_ada, b_ada):
    tn = 1536
    return pl.pallas_call(
        _ada_kernel,
        grid=(DEPTH, 6 * D_MODEL // tn),
        in_specs=[
            pl.BlockSpec((3, D_MODEL, 128), lambda l, j: (0, 0, 0)),
            pl.BlockSpec((None, D_MODEL, tn), lambda l, j: (l, 0, j)),
            pl.BlockSpec((None, 1, tn), lambda l, j: (l, 0, j)),
        ],
        out_specs=pl.BlockSpec((None, 3, 1, tn), lambda l, j: (l, 0, 0, j)),
        out_shape=jax.ShapeDtypeStruct((DEPTH, 3, 1, 6 * D_MODEL), F32),
        compiler_params=_cparams(("arbitrary", "arbitrary")),
        name="adaln",
    )(cb, w_ada, b_ada.reshape(DEPTH, 1, 6 * D_MODEL))


def _mod_specs(col):
    return [pl.BlockSpec((None, 1, D_MODEL), lambda b, i: (2, 0, col)),
            pl.BlockSpec((None, 1, D_MODEL), lambda b, i: (b, 0, col))]


def _mod(mc_ref, mb_ref, tm, with_ctx):
    if not with_ctx:
        return mb_ref[...]
    rows = pl.program_id(1) * tm + lax.broadcasted_iota(jnp.int32, (tm, 1), 0)
    return jnp.where(rows >= SEQ, mc_ref[...], mb_ref[...])


def _rms(x):
    return x * lax.rsqrt(jnp.mean(x * x, axis=-1, keepdims=True) + EPS)


def _proj_kernel(x_ref, g_ref, shc_ref, shb_ref, scc_ref, scb_ref, w_ref, cos_ref, sin_ref, att_ref, f32_ref):
    tm = x_ref.shape[0]
    sh, sc = _mod(shc_ref, shb_ref, tm, True), _mod(scc_ref, scb_ref, tm, True)
    h = ((_rms(x_ref[...]) * g_ref[...]) * (1.0 + sc) + sh).astype(BF16)
    cos, sin = cos_ref[...], sin_ref[...]
    first = (lax.broadcasted_iota(jnp.int32, (tm, 128), 1) % 32) < 16

    def rope(v):
        return v * cos + jnp.where(first, pltpu.roll(v, 112, 1), pltpu.roll(v, 16, 1)) * sin

    a = _dot(h, w_ref[:, 0:768])
    att_ref[:, 0:256] = (a[:, 0:256] * Q_SCALE).astype(BF16)
    att_ref[:, 256:768] = a[:, 256:768].astype(BF16)
    f32_ref[:, 0:256] = _dot(h, w_ref[:, 768:1024])
    s = _dot(h, w_ref[:, 1024:1536])
    att_ref[:, 768:896] = (rope(s[:, 0:128]) * Q_SCALE).astype(BF16)
    att_ref[:, 896:1024] = (rope(s[:, 128:256]) * Q_SCALE).astype(BF16)
    att_ref[:, 1024:1152] = rope(s[:, 256:384]).astype(BF16)
    att_ref[:, 1152:1280] = s[:, 384:512].astype(BF16)
    f32_ref[:, 256:1536] = _dot(h, w_ref[:, 1536:2816])


def _proj_call(xs, g, mods, w_in, cos, sin):
    b = xs.shape[0]
    tm = TM_ALL
    return pl.pallas_call(
        _proj_kernel,
        grid=(b, T_ALL // tm),
        in_specs=[
            pl.BlockSpec((None, tm, D_MODEL), lambda b, i: (b, i, 0)),
            _const_spec((1, D_MODEL)),
            *_mod_specs(0), *_mod_specs(1),
            _const_spec((D_MODEL, D_IN)),
            pl.BlockSpec((tm, 128), lambda b, i: (i, 0)),
            pl.BlockSpec((tm, 128), lambda b, i: (i, 0)),
        ],
        out_specs=[
            pl.BlockSpec((None, tm, ATT_W), lambda b, i: (b, i, 0)),
            pl.BlockSpec((None, tm, F32_W), lambda b, i: (b, i, 0)),
        ],
        out_shape=[jax.ShapeDtypeStruct((b, T_ALL, ATT_W), BF16), jax.ShapeDtypeStruct((b, T_ALL, F32_W), F32)],
        compiler_params=_cparams(("arbitrary", "arbitrary")),
        name="proj_in",
    )(xs, g, mods, mods, mods, mods, w_in, cos, sin)


def _na_bias_kernel(vl_ref, vr_ref, o_ref):
    lane = lax.broadcasted_iota(jnp.int32, (GRID_W, 128), 1)
    for e in range(NA_DR_ROWS - 1):
        left = pltpu.roll(jnp.broadcast_to(vl_ref[e:e + 1, :], (GRID_W, 128)), 0, 1, stride=1, stride_axis=0)
        right = pltpu.roll(jnp.broadcast_to(vr_ref[e + 1:e + 2, :], (GRID_W, 128)), 0, 1, stride=1, stride_axis=0)
        o_ref[e] = jnp.where(lane < 64, left, right)


def _na_bias_call(vl, vr):
    spec = pl.BlockSpec((None, NA_DR_ROWS, 128), lambda h: (h, 0, 0))
    return pl.pallas_call(
        _na_bias_kernel,
        grid=(4,),
        in_specs=[spec, spec],
        out_specs=pl.BlockSpec((None, NA_DR_ROWS - 1, GRID_W, 128), lambda h: (h, 0, 0, 0)),
        out_shape=jax.ShapeDtypeStruct((4, NA_DR_ROWS - 1, GRID_W, 128), F32),
        compiler_params=_cparams(("arbitrary",)),
        name="na_bias",
    )(vl, vr)


def _na_kernel(q_ref, k_ref, v_ref, mask_ref, t2_ref, o_ref):
    i = pl.program_id(1)
    is_ctx = i == CTX_BLK
    r0 = i * 4
    kr0 = jnp.where(is_ctx, 0, jnp.clip(r0 - 4, 0, GRID_ROWS - NA_KROWS))
    kstart = pl.multiple_of(kr0 * GRID_W, GRID_W)
    delta = jnp.where(is_ctx, 3, kr0 - r0 + NA_WIN_R - 1) + NA_DR_PAD
    mask = mask_ref[...]
    lane = lax.broadcasted_iota(jnp.int32, (TB, 128), 1)
    for hp in range(2):
        cs = slice(hp * 128, (hp + 1) * 128)
        q2 = q_ref[:, cs]
        kw, vw = k_ref[pl.ds(kstart, NA_KEYS), cs], v_ref[pl.ds(kstart, NA_KEYS), cs]
        kc, vc = k_ref[SEQ:T_ALL, cs], v_ref[SEQ:T_ALL, cs]
        outs = []
        for j in range(2):
            h = 2 * hp + j
            bias = jnp.concatenate(
                [jnp.concatenate([t2_ref[h, delta + 2 * rp - a] for rp in range(NA_KROWS // 2)], axis=1)
                 for a in range(4)], axis=0) + mask
            qm = jnp.where((lane >= 64 * j) & (lane < 64 * (j + 1)), q2, jnp.zeros_like(q2))
            s_nb = _dot_nt(qm, kw) + bias
            s_c = _dot_nt(qm, kc)
            m = jnp.maximum(jnp.max(s_nb, axis=-1, keepdims=True), jnp.max(s_c, axis=-1, keepdims=True))
            p_nb, p_c = jnp.exp(s_nb - m), jnp.exp(s_c - m)
            l = jnp.sum(p_nb, axis=-1, keepdims=True) + jnp.sum(p_c, axis=-1, keepdims=True)
            o = _dot(p_nb.astype(BF16), vw) + _dot(p_c.astype(BF16), vc)
            outs.append(o * (1.0 / l))
        o_ref[:, cs] = jnp.where(lane < 64, outs[0], outs[1]).astype(BF16)


def _na_call(u_att, mask, t2, with_ctx):
    b = u_att.shape[0]
    nb = NBLK if with_ctx else CTX_BLK

    def variant(b, i):
        return (jnp.where(i == CTX_BLK, 3, jnp.where(i == 0, 0, jnp.where(i == CTX_BLK - 1, 2, 1))), 0, 0)

    return pl.pallas_call(
        _na_kernel,
        grid=(b, nb),
        in_specs=[
            pl.BlockSpec((None, TB, 256), lambda b, i: (b, i, 0)),
            pl.BlockSpec((None, T_ALL, 256), lambda b, i: (b, 0, 1)),
            pl.BlockSpec((None, T_ALL, 256), lambda b, i: (b, 0, 2)),
            pl.BlockSpec((None, TB, NA_KEYS), variant),
            _const_spec((4, NA_DR_ROWS - 1, GRID_W, 128)),
        ],
        out_specs=pl.BlockSpec((None, TB, 256), lambda b, i: (b, i, 0)),
        out_shape=jax.ShapeDtypeStruct((b, nb * TB, 256), BF16),
        compiler_params=_cparams(("arbitrary", "arbitrary")),
        name="na_attn",
    )(u_att, u_att, u_att, mask, t2)


def _swa_kernel(sink_ref, q_ref, k_ref, v_ref, o_ref):
    i = pl.program_id(1)
    q0 = i * TB
    ks0 = jnp.clip(q0 - SWA_WINDOW, 0, SEQ - SWA_KEYS)
    kstart = pl.multiple_of(ks0, SWA_WINDOW)
    rel = (lax.broadcasted_iota(jnp.int32, (TB, SWA_KEYS), 1)
           - lax.broadcasted_iota(jnp.int32, (TB, SWA_KEYS), 0) + (ks0 - q0))
    ok = (jnp.abs(rel) <= SWA_WINDOW) & (i < CTX_BLK)
    kw, vw = k_ref[pl.ds(kstart, SWA_KEYS), :], v_ref[pl.ds(kstart, SWA_KEYS), :]
    kc, vc = k_ref[SEQ:T_ALL, :], v_ref[SEQ:T_ALL, :]
    lane = lax.broadcasted_iota(jnp.int32, (TB, 128), 1)
    for g in range(2):
        q2 = q_ref[:, g * 128:(g + 1) * 128].astype(F32)
        gm = (lane >= 64 * g) & (lane < 64 * (g + 1))
        outs = []
        for j in range(2):
            qa = q2 if j == g else pltpu.roll(q2, 64, 1)
            qa = jnp.where(gm, qa, 0.0).astype(BF16)
            s_l = jnp.where(ok, _dot_nt(qa, kw), -jnp.inf)
            s_c = _dot_nt(qa, kc)
            sink = sink_ref[2 * g + j]
            m = jnp.maximum(jnp.maximum(jnp.max(s_l, axis=-1, keepdims=True),
                                        jnp.max(s_c, axis=-1, keepdims=True)), sink)
            p_l, p_c = jnp.exp(s_l - m), jnp.exp(s_c - m)
            l = jnp.sum(p_l, axis=-1, keepdims=True) + jnp.sum(p_c, axis=-1, keepdims=True) + jnp.exp(sink - m)
            o = (_dot(p_l.astype(BF16), vw) + _dot(p_c.astype(BF16), vc)) * (1.0 / l)
            outs.append(o if j == g else pltpu.roll(o, 64, 1))
        o_ref[:, g * 128:(g + 1) * 128] = jnp.where(lane < 64, outs[0], outs[1]).astype(BF16)


def _swa_call(u_att, sink, with_ctx):
    b = u_att.shape[0]
    nb = NBLK if with_ctx else CTX_BLK
    return pl.pallas_call(
        _swa_kernel,
        grid=(b, nb),
        in_specs=[
            pl.BlockSpec(memory_space=pltpu.SMEM),
            pl.BlockSpec((None, TB, 256), lambda b, i: (b, i, 3)),
            pl.BlockSpec((None, T_ALL, 128), lambda b, i: (b, 0, 8)),
            pl.BlockSpec((None, T_ALL, 128), lambda b, i: (b, 0, 9)),
        ],
        out_specs=pl.BlockSpec((None, TB, 256), lambda b, i: (b, i, 0)),
        out_shape=jax.ShapeDtypeStruct((b, nb * TB, 256), BF16),
        compiler_params=_cparams(("arbitrary", "arbitrary")),
        name="swa_attn",
    )(sink, u_att, u_att, u_att)


def _fnet_kernel(x_ref, f64_ref, g_ref, fc_ref, p_ref, q_ref, y_ref):
    sc_lat, sc_ctx = (64.0 * SEQ) ** -0.5, (64.0 * CTX) ** -0.5
    pq = _dot(fc_ref[...], x_ref[SEQ:T_ALL, :].astype(BF16))
    p_ref[SEQ:T_ALL, :] = pq[0:CTX] * sc_ctx
    q_ref[SEQ:T_ALL, :] = pq[CTX:2 * CTX] * sc_ctx
    f64 = f64_ref[...]

    def stage1(t2, carry):
        xs = x_ref[pl.ds(t2, 64, stride=128), :].astype(BF16)
        y_ref[pl.ds(pl.multiple_of(t2 * 128, 128), 128), :] = _dot(f64, xs)
        return carry

    lax.fori_loop(0, 128, stage1, 0, unroll=8)

    def stage2(k1, carry):
        yre = y_ref[pl.ds(k1, 128, stride=128), :]
        yim = y_ref[pl.ds(64 + k1, 128, stride=128), :]
        ycat = jnp.concatenate([yre, yim], axis=0).astype(BF16)
        pq = _dot(g_ref[k1], ycat)
        p_ref[pl.ds(k1, 128, stride=64), :] = pq[0:128] * sc_lat
        q_ref[pl.ds(k1, 128, stride=64), :] = pq[128:256] * sc_lat
        return carry

    lax.fori_loop(0, 64, stage2, 0, unroll=4)


def _fnet_call(u_f32, f64, g2, fc):
    b = u_f32.shape[0]
    spec = pl.BlockSpec((None, T_ALL, 128), lambda b, j: (b, 0, j))
    return pl.pallas_call(
        _fnet_kernel,
        grid=(b, 2),
        in_specs=[spec, _const_spec((128, 64)), _const_spec((64, 256, 256)), _const_spec((2 * CTX, CTX))],
        out_specs=[spec, spec],
        out_shape=[jax.ShapeDtypeStruct((b, T_ALL, 256), F32)] * 2,
        scratch_shapes=[pltpu.VMEM((128 * 128, 128), F32)],
        compiler_params=_cparams(("arbitrary", "arbitrary")),
        name="fnet_dft",
    )(u_f32, f64, g2, fc)


def _hgrn_gates(qd, fz, gl):
    q = qd * jax.nn.sigmoid(qd)
    sp = jnp.log1p(jnp.exp(-jnp.abs(fz)))
    b_ = gl[1:2] + jnp.minimum(fz, 0.0) - sp
    a_ = gl[0:1]
    logf = jnp.maximum(a_, b_) + jnp.log1p(jnp.exp(-jnp.abs(a_ - b_)))
    kk = jnp.exp(gl[1:2] + jnp.minimum(-fz, 0.0) - sp)
    return q, kk, logf


def _dot01(b01, x):
    h1 = x.astype(BF16)
    r1 = x - h1.astype(F32)
    h2 = r1.astype(BF16)
    h3 = (r1 - h2.astype(F32)).astype(BF16)
    return _dot(b01, h1) + _dot(b01, h2) + _dot(b01, h3)


def _head_masks(rows):
    lane = lax.broadcasted_iota(jnp.int32, (rows, 256), 1)
    return [(lane >= 64 * h) & (lane < 64 * (h + 1)) for h in range(4)]


def _stack_heads(w, hms):
    return jnp.concatenate([jnp.where(hm, w, jnp.zeros_like(w)) for hm in hms], axis=0)


def _state_update(st_ref, st, e_last, vb, ks):
    ktv = _dot_tn(vb, ks)
    r2 = lax.broadcasted_iota(jnp.int32, (256, 256), 0) // 64
    c2 = lax.broadcasted_iota(jnp.int32, (256, 256), 1) // 64
    st_ref[...] = st * e_last + jnp.where(r2 == c2, ktv, 0.0)


def _hgrn_chunk_exact(qd, fz, v, st_ref, gl, ball, lmask_ref, d, rev):
    q, kk, logf = _hgrn_gates(qd, fz, gl)
    e = jnp.exp(_dot01(ball, logf))
    row = lax.broadcasted_iota(jnp.int32, (CHUNK, 256), 0)
    tau = (CHUNK - 1 - row) if rev else row
    hms = _head_masks(CHUNK)
    att = _dot_nt(_stack_heads(q.astype(BF16), hms), kk.astype(BF16)) * lmask_ref[d, 0]
    for li, sh in enumerate((5, 4, 3, 2, 1, 0)):
        up = ((tau >> sh) & 1) == 1
        w = (jnp.where(up, q, kk) * e[128 + 64 * li:192 + 64 * li]).astype(BF16)
        att = att + _dot_nt(_stack_heads(w, hms), w) * lmask_ref[d, 1 + li]
    vb = v.astype(BF16)
    st = st_ref[...]
    o = _dot_nt((q * e[0:64]).astype(BF16), st.astype(BF16))
    for h in range(4):
        o = o + _dot(att[64 * h:64 * (h + 1)].astype(BF16), jnp.where(hms[h], vb, jnp.zeros_like(vb)))
    tl = 0 if rev else CHUNK - 1
    _state_update(st_ref, st, e[tl:tl + 1], vb, (kk * e[64:128]).astype(BF16))
    return o


def _hgrn_block_fast(q, kk, a, cum_mid, cum_last, v, st_ref, o_ref, rev):
    eq, ek = jnp.exp(a), jnp.exp(-a)
    hms = _head_masks(CHUNK)
    r = lax.broadcasted_iota(jnp.int32, (4 * CHUNK, CHUNK), 0) & (CHUNK - 1)
    s = lax.broadcasted_iota(jnp.int32, (4 * CHUNK, CHUNK), 1)
    causal = (s >= r) if rev else (s <= r)
    for c in (range(NCHUNK - 1, -1, -1) if rev else range(NCHUNK)):
        sl = slice(c * CHUNK, (c + 1) * CHUNK)
        qt, kt = q[sl] * eq[sl], kk[sl] * ek[sl]
        vb = v[sl].astype(BF16)
        att = jnp.where(causal, _dot_nt(_stack_heads(qt.astype(BF16), hms), kt.astype(BF16)), 0.0)
        st = st_ref[...]
        o = _dot_nt((qt * jnp.exp(cum_mid[c])).astype(BF16), st.astype(BF16))
        for h in range(4):
            o = o + _dot(att[64 * h:64 * (h + 1)].astype(BF16), jnp.where(hms[h], vb, jnp.zeros_like(vb)))
        o_ref[sl, :] = o
        _state_update(st_ref, st, jnp.exp(cum_last[c]), vb, (kt * jnp.exp(cum_last[c] - cum_mid[c])).astype(BF16))


def _hgrn_kernel(qf_ref, ff_ref, vf_ref, qb_ref, fb_ref, vb_ref, gl_ref, ball_ref, lmask_ref, bblk_ref,
                 of_ref, ob_ref, sf_ref, sb_ref):
    @pl.when(pl.program_id(1) == 0)
    def _():
        sf_ref[...] = jnp.zeros_like(sf_ref)
        sb_ref[...] = jnp.zeros_like(sb_ref)

    dirs = ((qf_ref, ff_ref, vf_ref, sf_ref, of_ref, False), (qb_ref, fb_ref, vb_ref, sb_ref, ob_ref, True))
    pre, rng = [], 0.0
    for d, (q_ref, f_ref, v_ref, st_ref, o_ref, rev) in enumerate(dirs):
        q, kk, logf = _hgrn_gates(q_ref[...], f_ref[...], gl_ref[d])
        cum = _dot01(bblk_ref[d], logf)
        mid_row, last_row = (CHUNK // 2, 0) if rev else (CHUNK // 2 - 1, CHUNK - 1)
        cum_mid = [cum[c * CHUNK + mid_row:c * CHUNK + mid_row + 1] for c in range(NCHUNK)]
        cum_last = [cum[c * CHUNK + last_row:c * CHUNK + last_row + 1] for c in range(NCHUNK)]
        a = jnp.concatenate([cum[c * CHUNK:(c + 1) * CHUNK] - cum_mid[c] for c in range(NCHUNK)], axis=0)
        rng = jnp.maximum(rng, jnp.max(jnp.abs(a)))
        pre.append((q, kk, a, cum_mid, cum_last))
    fast = rng <= HGRN_FAST_RANGE

    @pl.when(fast)
    def _():
        for d, (q_ref, f_ref, v_ref, st_ref, o_ref, rev) in enumerate(dirs):
            q, kk, a, cum_mid, cum_last = pre[d]
            _hgrn_block_fast(q, kk, a, cum_mid, cum_last, v_ref[...], st_ref, o_ref, rev)

    @pl.when(jnp.logical_not(fast))
    def _():
        def body(c, carry):
            rf = pl.ds(pl.multiple_of(c * CHUNK, CHUNK), CHUNK)
            rb = pl.ds(pl.multiple_of((NCHUNK - 1 - c) * CHUNK, CHUNK), CHUNK)
            of_ref[rf, :] = _hgrn_chunk_exact(qf_ref[rf, :], ff_ref[rf, :], vf_ref[rf, :], sf_ref,
                                              gl_ref[0], ball_ref[0], lmask_ref, 0, False)
            ob_ref[rb, :] = _hgrn_chunk_exact(qb_ref[rb, :], fb_ref[rb, :], vb_ref[rb, :], sb_ref,
                                              gl_ref[1], ball_ref[1], lmask_ref, 1, True)
            return carry

        lax.fori_loop(0, NCHUNK, body, 0)


def _hgrn_call(u_f32, gl, ball, lmask, bblk):
    b = u_f32.shape[0]

    def fwd(col):
        return pl.BlockSpec((None, TB, 256), lambda b, i: (b, jnp.where(i == 0, CTX_BLK, i - 1), col))

    def bwd(col):
        return pl.BlockSpec((None, TB, 256), lambda b, i: (b, jnp.where(i == 0, CTX_BLK, CTX_BLK - i), col))

    return pl.pallas_call(
        _hgrn_kernel,
        grid=(b, NBLK),
        in_specs=[fwd(1), fwd(2), fwd(4), bwd(1), bwd(3), bwd(4),
                  _const_spec((2, 2, 256)), _const_spec((2, 8 * CHUNK, CHUNK)),
                  _const_spec((2, 7, 4 * CHUNK, CHUNK)), _const_spec((2, TB, TB))],
        out_specs=[fwd(0), bwd(0)],
        out_shape=[jax.ShapeDtypeStruct((b, T_ALL, 256), F32)] * 2,
        scratch_shapes=[pltpu.VMEM((256, 256), F32), pltpu.VMEM((256, 256), F32)],
        compiler_params=_cparams(("arbitrary", "arbitrary")),
        name="hgrn_scan",
    )(u_f32, u_f32, u_f32, u_f32, u_f32, u_f32, gl, ball, lmask, bblk)


def _outproj_kernel(x_ref, oa_ref, p_ref, q_ref, oc_ref, of_ref, ob_ref, gd_ref, hg_ref, hh_ref, cs_ref,
                    wf_ref, wo_ref, gp_ref, gtc_ref, gtb_ref, o_ref, *, with_ctx):
    z = _dot(p_ref[...].astype(BF16), cs_ref[0:256, :]) + _dot(q_ref[...].astype(BF16), cs_ref[256:512, :])
    o_b = _dot(z.astype(BF16), wf_ref[...])
    od = of_ref[...] + ob_ref[...]
    sq = od * od
    hi = sq.astype(BF16)
    lo = (sq - hi.astype(F32)).astype(BF16)
    ms = (_dot(hi, hh_ref[...]) + _dot(lo, hh_ref[...])) * (1.0 / HEAD_DIM)
    gd = gd_ref[...]
    o_d = od * lax.rsqrt(ms + EPS) * hg_ref[...] * (gd * jax.nn.sigmoid(gd))
    y = (_dot(oa_ref[...], wo_ref[0:256, :]) + _dot(o_b.astype(BF16), wo_ref[256:512, :])
         + _dot(oc_ref[...], wo_ref[512:768, :]) + _dot(o_d.astype(BF16), wo_ref[768:1024, :]))
    gt = _mod(gtc_ref, gtb_ref, x_ref.shape[0], with_ctx)
    o_ref[...] = x_ref[...] + gt * (_rms(y) * gp_ref[...])


def _outproj_call(xs, o_a, p, q, o_c, o_f, o_bk, u_f32, hg, hh, cs, w_fnet, w_out, g_post, mods, with_ctx):
    b = xs.shape[0]
    tm, t = (TM_ALL, T_ALL) if with_ctx else (TM_LAT, SEQ)
    col = lambda c: pl.BlockSpec((None, tm, 256), lambda b, i: (b, i, c))
    return pl.pallas_call(
        functools.partial(_outproj_kernel, with_ctx=with_ctx),
        grid=(b, t // tm),
        in_specs=[
            pl.BlockSpec((None, tm, D_MODEL), lambda b, i: (b, i, 0)),
            col(0), col(0), col(0), col(0), col(0), col(0), col(5),
            _const_spec((1, 256)), _const_spec((256, 256)), _const_spec((512, 256)),
            _const_spec((256, 256)), _const_spec((D_MODEL, D_MODEL)), _const_spec((1, D_MODEL)),
            *_mod_specs(2),
        ],
        out_specs=pl.BlockSpec((None, tm, D_MODEL), lambda b, i: (b, i, 0)),
        out_shape=jax.ShapeDtypeStruct((b, t, D_MODEL), F32),
        compiler_params=_cparams(("arbitrary", "arbitrary")),
        name="out_proj",
    )(xs, o_a, p, q, o_c, o_f, o_bk, u_f32, hg, hh, cs, w_fnet, w_out, g_post, mods, mods)


def _ffn_kernel(x_ref, g_ref, shc_ref, shb_ref, scc_ref, scb_ref, w1_ref, w2_ref, gp_ref, gtc_ref, gtb_ref,
                o_ref, *, with_ctx):
    tm = x_ref.shape[0]
    x = x_ref[...]
    sh, sc = _mod(shc_ref, shb_ref, tm, with_ctx), _mod(scc_ref, scb_ref, tm, with_ctx)
    h = ((_rms(x) * g_ref[...]) * (1.0 + sc) + sh).astype(BF16)
    acc = jnp.zeros((tm, D_MODEL), F32)
    fc = 256
    for c in range(D_FF // fc):
        a = _dot(h, w1_ref[:, c * fc:(c + 1) * fc])
        g = _dot(h, w1_ref[:, D_FF + c * fc:D_FF + (c + 1) * fc])
        act = (a * jax.nn.sigmoid(a) * g).astype(BF16)
        acc = acc + _dot(act, w2_ref[c * fc:(c + 1) * fc, :])
    gt = _mod(gtc_ref, gtb_ref, tm, with_ctx)
    o_ref[...] = x + gt * (_rms(acc) * gp_ref[...])


def _ffn_call(xs, g_pre, w1, w2, g_post, mods, with_ctx):
    b, t, _ = xs.shape
    tm = TM_ALL if with_ctx else TM_LAT
    return pl.pallas_call(
        functools.partial(_ffn_kernel, with_ctx=with_ctx),
        grid=(b, t // tm),
        in_specs=[
            pl.BlockSpec((None, tm, D_MODEL), lambda b, i: (b, i, 0)),
            _const_spec((1, D_MODEL)),
            *_mod_specs(3), *_mod_specs(4),
            _const_spec((D_MODEL, 2 * D_FF)), _const_spec((D_FF, D_MODEL)), _const_spec((1, D_MODEL)),
            *_mod_specs(5),
        ],
        out_specs=pl.BlockSpec((None, tm, D_MODEL), lambda b, i: (b, i, 0)),
        out_shape=jax.ShapeDtypeStruct(xs.shape, F32),
        compiler_params=_cparams(("arbitrary", "arbitrary")),
        name="ffn",
    )(xs, g_pre, mods, mods, mods, mods, w1, w2, g_post, mods, mods)


def kernel(x, c, ctx, c_ctx, w_ada, b_ada, g_pre_mix, g_post_mix, g_pre_ffn, g_post_ffn, w_in, w_out,
           na_rpb, w_fnet, swa_sink, hgrn_lb_logits, hgrn_norm_g, w_ffn_in, w_ffn_out):
    tb = _tables()
    bf = lambda name: jnp.asarray(tb[name]).astype(BF16)
    cos, sin = jnp.asarray(tb["rope_cos"]), jnp.asarray(tb["rope_sin"])
    f64, g2, fc, chan_cs, hh = bf("f64"), bf("g_stage2"), bf("f_ctx"), bf("chan_cs"), bf("head_ones")
    ball, bblk, lmask = bf("hgrn_b"), bf("hgrn_bblk"), jnp.asarray(tb["hgrn_mask"])
    na_mask = jnp.asarray(tb["na_mask"])

    cc = jnp.concatenate([c, c_ctx[None, :]], axis=0)
    mods = _ada_call(jnp.broadcast_to(cc[:, :, None], (3, D_MODEL, 128)), w_ada, b_ada)

    lb = jnp.cumsum(jax.nn.softmax(hgrn_lb_logits.astype(F32), axis=1), axis=1)
    lb = lb - lb[:, :1]
    gl_all = jnp.stack([jnp.log(lb), jnp.log1p(-lb)], axis=2)

    xs = jnp.concatenate([x, ctx], axis=1)
    for l in range(DEPTH):
        with_ctx = l < DEPTH - 1
        row = lambda a: a[l][None, :]
        u_att, u_f32 = _proj_call(xs, row(g_pre_mix), mods[l], w_in[l].astype(BF16), cos, sin)

        rpb = jnp.pad(na_rpb[l], ((0, 0), (NA_DR_PAD, NA_DR_ROWS - NA_DR_PAD - (2 * NA_WIN_R - 1)), (0, 0)))
        t2 = _na_bias_call(rpb[:, :, tb["na_lane_l"]], rpb[:, :, tb["na_lane_r"]])
        o_a = _na_call(u_att, na_mask, t2, with_ctx)
        o_c = _swa_call(u_att, swa_sink[l], with_ctx)
        p, q = _fnet_call(u_f32, f64, g2, fc)
        o_f, o_bk = _hgrn_call(u_f32, gl_all[:, l], ball, lmask, bblk)

        xs = _outproj_call(xs, o_a, p, q, o_c, o_f, o_bk, u_f32, row(hgrn_norm_g), hh, chan_cs,
                           w_fnet[l].astype(BF16), w_out[l].astype(BF16), row(g_post_mix), mods[l], with_ctx)
        xs = _ffn_call(xs, row(g_pre_ffn), w_ffn_in[l].astype(BF16), w_ffn_out[l].astype(BF16),
                       row(g_post_ffn), mods[l], with_ctx)
    return xs
```

```python
import functools

import numpy as np
import jax
import jax.numpy as jnp
from jax import lax
from jax.experimental import pallas as pl
from jax.experimental.pallas import tpu as pltpu

F32, BF16 = jnp.float32, jnp.bfloat16

D_MODEL = 1024
SEQ = 8192
DEPTH = 2
GRID_W = 64
GRID_ROWS = SEQ // GRID_W
CTX = 256
HEAD_DIM = 64
NA_WIN_R, NA_WIN_C = 8, 16
SWA_WINDOW = 128
ROPE_THETA = 10000.0
EPS = 1e-6
D_FF = 2816
D_IN = 2816

TB = 256
T_ALL = SEQ + CTX
NBLK = T_ALL // TB
CTX_BLK = SEQ // TB
TM_ALL = 768
TM_LAT = 1024
NA_KROWS = 12
NA_KEYS = NA_KROWS * GRID_W
NA_DR_PAD = 4
NA_DR_ROWS = 24
SWA_KEYS = TB + 2 * SWA_WINDOW
CHUNK = 64
NCHUNK = TB // CHUNK
HGRN_FAST_RANGE = 60.0
Q_SCALE = HEAD_DIM ** -0.5
VMEM_LIMIT = 56 * 1024 * 1024

ATT_W, F32_W = 1280, 1536


def _dot(a, b):
    return jnp.dot(a, b, preferred_element_type=F32)


def _dot_nt(a, b):
    return lax.dot_general(a, b, (((1,), (1,)), ((), ())), preferred_element_type=F32)


def _dot_tn(a, b):
    return lax.dot_general(a, b, (((0,), (0,)), ((), ())), preferred_element_type=F32)


def _cparams(sem):
    return pltpu.CompilerParams(dimension_semantics=sem, vmem_limit_bytes=VMEM_LIMIT)


def _const_spec(shape):
    nd = len(shape)
    return pl.BlockSpec(shape, lambda *_: (0,) * nd, pipeline_mode=pl.Buffered(1))


@functools.lru_cache(maxsize=None)
def _tables():
    t = {}
    lane = np.arange(128)
    d = lane % 64
    freq = ROPE_THETA ** (-(d % 16) / 16.0)
    pos = np.arange(SEQ)
    p = np.where(d[None, :] < 32, (pos // GRID_W)[:, None], (pos % GRID_W)[:, None]).astype(np.float64)
    ang = p * freq[None, :]
    sign = np.where((d % 32) < 16, -1.0, 1.0)
    cos = np.concatenate([np.cos(ang), np.ones((CTX, 128))], axis=0)
    sin = np.concatenate([np.sin(ang) * sign[None, :], np.zeros((CTX, 128))], axis=0)
    t["rope_cos"], t["rope_sin"] = cos.astype(np.float32), sin.astype(np.float32)

    masks = []
    a, cq = np.arange(4), np.arange(GRID_W)
    rho, ck = np.arange(NA_KROWS), np.arange(GRID_W)
    for r0 in (0, 8, GRID_ROWS - 4):
        kr0 = int(np.clip(r0 - 4, 0, GRID_ROWS - NA_KROWS))
        rq, rk = r0 + a, kr0 + rho
        start = np.clip(rq - NA_WIN_R // 2, 0, GRID_ROWS - NA_WIN_R)
        vr = (rk[None, :] >= start[:, None]) & (rk[None, :] < start[:, None] + NA_WIN_R)
        cs = np.clip(cq - NA_WIN_C // 2, 0, GRID_W - NA_WIN_C)
        vc = (ck[None, :] >= cs[:, None]) & (ck[None, :] < cs[:, None] + NA_WIN_C)
        val = vr[:, None, :, None] & vc[None, :, None, :]
        masks.append(np.where(val, 0.0, -np.inf).reshape(TB, NA_KEYS))
    masks.append(np.full((TB, NA_KEYS), -np.inf))
    t["na_mask"] = np.stack(masks).astype(np.float32)
    n = np.arange(128)
    clipc = lambda m: np.clip(m, -(NA_WIN_C - 1), NA_WIN_C - 1) + NA_WIN_C - 1
    t["na_lane_l"] = clipc(np.where(n < 64, n, n - 128)).astype(np.int32)
    t["na_lane_r"] = clipc(n - 64).astype(np.int32)

    def cs_tab(num, den):
        ang = 2.0 * np.pi * (num % den) / den
        return np.cos(ang), np.sin(ang)

    k1, t1 = np.arange(64)[:, None], np.arange(64)[None, :]
    c64, s64 = cs_tab(k1 * t1, 64)
    t["f64"] = np.concatenate([c64, -s64], axis=0).astype(np.float32)
    k2, t2 = np.arange(128)[:, None], np.arange(128)[None, :]
    g = []
    for kk in range(64):
        gc, gs = cs_tab((64 * k2 + kk) * t2, SEQ)
        g.append(np.block([[gc, gs], [-gs, gc]]))
    t["g_stage2"] = np.stack(g).astype(np.float32)
    kc, tc = np.arange(CTX)[:, None], np.arange(CTX)[None, :]
    cc, sc = cs_tab(kc * tc, CTX)
    t["f_ctx"] = np.concatenate([cc, -sc], axis=0).astype(np.float32)
    ch = np.arange(256)
    same = (ch[:, None] // 64) == (ch[None, :] // 64)
    c_ch, s_ch = cs_tab((ch[:, None] % 64) * (ch[None, :] % 64), 64)
    t["chan_cs"] = np.concatenate([np.where(same, c_ch, 0.0), np.where(same, s_ch, 0.0)], axis=0).astype(np.float32)
    t["head_ones"] = same.astype(np.float32)

    balls, lmasks, bblk = [], [], []
    for rev in (False, True):
        tau = (CHUNK - 1 - np.arange(CHUNK)) if rev else np.arange(CHUNK)
        tt, tj = tau[:, None], tau[None, :]
        blocks = [(tj <= tt), (tj > tt)]
        masks = [tt == tj]
        for h in (32, 16, 8, 4, 2, 1):
            up = ((tt // h) % 2) == 1
            mid = (tt // (2 * h)) * (2 * h) + h - 1
            blocks.append(np.where(up, (tj > mid) & (tj <= tt), (tj > tt) & (tj <= mid)))
            up_s = ((tj // h) % 2) == 1
            masks.append((tt // (2 * h) == tj // (2 * h)) & up & ~up_s)
        balls.append(np.concatenate(blocks, axis=0))
        lmasks.append(np.stack([np.tile(m, (4, 1)) for m in masks]))
        bblk.append(np.kron(np.eye(NCHUNK), blocks[0]))
    t["hgrn_b"] = np.stack(balls).astype(np.float32)
    t["hgrn_mask"] = np.stack(lmasks).astype(np.float32)
    t["hgrn_bblk"] = np.stack(bblk).astype(np.float32)
    return t


def _ada_kernel(cb_ref, w_ref, b_ref, o_ref, s_ref):
    tn = w_ref.shape[1]
    nrep = tn // 128

    @pl.when((pl.program_id(0) == 0) & (pl.program_id(1) == 0))
    def _():
        cb = cb_ref[...]
        s_ref[...] = cb * jax.nn.sigmoid(cb)

    def body(kc, accs):
        k0 = pl.multiple_of(kc * 8, 8)
        wt = w_ref[pl.ds(k0, 8), :]
        return tuple(accs[r] + wt * jnp.concatenate([s_ref[r, pl.ds(k0, 8), :]] * nrep, axis=1) for r in range(3))

    accs = lax.fori_loop(0, D_MODEL // 8, body, tuple(jnp.zeros((8, tn), F32) for _ in range(3)), unroll=8)
    for r in range(3):
        o_ref[r] = jnp.sum(accs[r], axis=0, keepdims=True) + b_ref[...]


def _ada_call(cb, w_ada, b_ada):
    tn = 1536
    return pl.pallas_call(
        _ada_kernel,
        grid=(DEPTH, 6 * D_MODEL // tn),
        in_specs=[
            pl.BlockSpec((3, D_MODEL, 128), lambda l, j: (0, 0, 0)),
            pl.BlockSpec((None, D_MODEL, tn), lambda l, j: (l, 0, j)),
            pl.BlockSpec((None, 1, tn), lambda l, j: (l, 0, j)),
        ],
        out_specs=pl.BlockSpec((None, 3, 1, tn), lambda l, j: (l, 0, 0, j)),
        out_shape=jax.ShapeDtypeStruct((DEPTH, 3, 1, 6 * D_MODEL), F32),
        scratch_shapes=[pltpu.VMEM((3, D_MODEL, 128), F32)],
        compiler_params=_cparams(("arbitrary", "arbitrary")),
        name="adaln",
    )(cb, w_ada, b_ada.reshape(DEPTH, 1, 6 * D_MODEL))


def _mod_specs(col):
    return [pl.BlockSpec((None, 1, D_MODEL), lambda b, i: (2, 0, col)),
            pl.BlockSpec((None, 1, D_MODEL), lambda b, i: (b, 0, col))]


def _mod(mc_ref, mb_ref, tm, with_ctx):
    if not with_ctx:
        return mb_ref[...]
    rows = pl.program_id(1) * tm + lax.broadcasted_iota(jnp.int32, (tm, 1), 0)
    return jnp.where(rows >= SEQ, mc_ref[...], mb_ref[...])


def _rms(x):
    return x * lax.rsqrt(jnp.mean(x * x, axis=-1, keepdims=True) + EPS)


def _x_specs(split):
    if not split:
        return [pl.BlockSpec((None, TM_ALL, D_MODEL), lambda b, i: (b, i, 0))]
    nx = SEQ // TB
    return [pl.BlockSpec((None, TB, D_MODEL), lambda b, i, j=j: (b, jnp.minimum(3 * i + j, nx - 1), 0))
            for j in range(TM_ALL // TB)] + [pl.BlockSpec((None, CTX, D_MODEL), lambda b, i: (b, 0, 0))]


def _x_tile(x_refs):
    if len(x_refs) == 1:
        return x_refs[0][...]
    xa, xb, xc, cx = x_refs
    last = pl.program_id(1) == pl.num_programs(1) - 1
    return jnp.concatenate([xa[...], xb[...], jnp.where(last, cx[...], xc[...])], axis=0)


def _proj_kernel(*refs, nx):
    x = _x_tile(refs[:nx])
    g_ref, shc_ref, shb_ref, scc_ref, scb_ref, w_ref, cos_ref, sin_ref, att_ref, f32_ref = refs[nx:]
    tm = TM_ALL
    sh, sc = _mod(shc_ref, shb_ref, tm, True), _mod(scc_ref, scb_ref, tm, True)
    h = ((_rms(x) * g_ref[...]) * (1.0 + sc) + sh).astype(BF16)
    cos, sin = cos_ref[...], sin_ref[...]
    first = (lax.broadcasted_iota(jnp.int32, (tm, 128), 1) % 32) < 16

    def rope(v):
        return v * cos + jnp.where(first, pltpu.roll(v, 112, 1), pltpu.roll(v, 16, 1)) * sin

    a = _dot(h, w_ref[:, 0:768])
    att_ref[:, 0:256] = (a[:, 0:256] * Q_SCALE).astype(BF16)
    att_ref[:, 256:768] = a[:, 256:768].astype(BF16)
    f32_ref[:, 0:256] = _dot(h, w_ref[:, 768:1024])
    s = _dot(h, w_ref[:, 1024:1536])
    att_ref[:, 768:896] = (rope(s[:, 0:128]) * Q_SCALE).astype(BF16)
    att_ref[:, 896:1024] = (rope(s[:, 128:256]) * Q_SCALE).astype(BF16)
    att_ref[:, 1024:1152] = rope(s[:, 256:384]).astype(BF16)
    att_ref[:, 1152:1280] = s[:, 384:512].astype(BF16)
    f32_ref[:, 256:1536] = _dot(h, w_ref[:, 1536:2816])


def _proj_call(x_parts, g, mods, w_in, cos, sin):
    b = x_parts[0].shape[0]
    tm = TM_ALL
    split = len(x_parts) == 2
    x_args = (x_parts[0],) * 3 + (x_parts[1],) if split else x_parts
    return pl.pallas_call(
        functools.partial(_proj_kernel, nx=len(x_args)),
        grid=(b, T_ALL // tm),
        in_specs=[
            *_x_specs(split),
            _const_spec((1, D_MODEL)),
            *_mod_specs(0), *_mod_specs(1),
            _const_spec((D_MODEL, D_IN)),
            pl.BlockSpec((tm, 128), lambda b, i: (i, 0)),
            pl.BlockSpec((tm, 128), lambda b, i: (i, 0)),
        ],
        out_specs=[
            pl.BlockSpec((None, tm, ATT_W), lambda b, i: (b, i, 0)),
            pl.BlockSpec((None, tm, F32_W), lambda b, i: (b, i, 0)),
        ],
        out_shape=[jax.ShapeDtypeStruct((b, T_ALL, ATT_W), BF16), jax.ShapeDtypeStruct((b, T_ALL, F32_W), F32)],
        compiler_params=_cparams(("arbitrary", "arbitrary")),
        name="proj_in",
    )(*x_args, g, mods, mods, mods, mods, w_in, cos, sin)


def _na_bias_kernel(vl_ref, vr_ref, o_ref):
    lane = lax.broadcasted_iota(jnp.int32, (GRID_W, 128), 1)
    for e in range(NA_DR_ROWS - 1):
        left = pltpu.roll(jnp.broadcast_to(vl_ref[e:e + 1, :], (GRID_W, 128)), 0, 1, stride=1, stride_axis=0)
        right = pltpu.roll(jnp.broadcast_to(vr_ref[e + 1:e + 2, :], (GRID_W, 128)), 0, 1, stride=1, stride_axis=0)
        o_ref[e] = jnp.where(lane < 64, left, right)


def _na_bias_call(vl, vr):
    spec = pl.BlockSpec((None, NA_DR_ROWS, 128), lambda h: (h, 0, 0))
    return pl.pallas_call(
        _na_bias_kernel,
        grid=(4,),
        in_specs=[spec, spec],
        out_specs=pl.BlockSpec((None, NA_DR_ROWS - 1, GRID_W, 128), lambda h: (h, 0, 0, 0)),
        out_shape=jax.ShapeDtypeStruct((4, NA_DR_ROWS - 1, GRID_W, 128), F32),
        compiler_params=_cparams(("arbitrary",)),
        name="na_bias",
    )(vl, vr)


def _na_kernel(q_ref, k_ref, v_ref, mask_ref, t2_ref, o_ref):
    i = pl.program_id(1)
    is_ctx = i == CTX_BLK
    r0 = i * 4
    kr0 = jnp.where(is_ctx, 0, jnp.clip(r0 - 4, 0, GRID_ROWS - NA_KROWS))
    kstart = pl.multiple_of(kr0 * GRID_W, GRID_W)
    delta = jnp.where(is_ctx, 3, kr0 - r0 + NA_WIN_R - 1) + NA_DR_PAD
    mask = mask_ref[...]
    lane = lax.broadcasted_iota(jnp.int32, (TB, 128), 1)
    for hp in range(2):
        cs = slice(hp * 128, (hp + 1) * 128)
        q2 = q_ref[:, cs]
        kw, vw = k_ref[pl.ds(kstart, NA_KEYS), cs], v_ref[pl.ds(kstart, NA_KEYS), cs]
        kc, vc = k_ref[SEQ:T_ALL, cs], v_ref[SEQ:T_ALL, cs]
        outs = []
        for j in range(2):
            h = 2 * hp + j
            bias = jnp.concatenate(
                [jnp.concatenate([t2_ref[h, delta + 2 * rp - a] for rp in range(NA_KROWS // 2)], axis=1)
                 for a in range(4)], axis=0) + mask
            qm = jnp.where((lane >= 64 * j) & (lane < 64 * (j + 1)), q2, jnp.zeros_like(q2))
            s_nb = _dot_nt(qm, kw) + bias
            s_c = _dot_nt(qm, kc)
            m = jnp.maximum(jnp.max(s_nb, axis=-1, keepdims=True), jnp.max(s_c, axis=-1, keepdims=True))
            p_nb, p_c = jnp.exp(s_nb - m), jnp.exp(s_c - m)
            l = jnp.sum(p_nb, axis=-1, keepdims=True) + jnp.sum(p_c, axis=-1, keepdims=True)
            o = _dot(p_nb.astype(BF16), vw) + _dot(p_c.astype(BF16), vc)
            outs.append(o * (1.0 / l))
        o_ref[:, cs] = jnp.where(lane < 64, outs[0], outs[1]).astype(BF16)


def _na_call(u_att, mask, t2, with_ctx):
    b = u_att.shape[0]
    nb = NBLK if with_ctx else CTX_BLK

    def variant(b, i):
        return (jnp.where(i == CTX_BLK, 3, jnp.where(i == 0, 0, jnp.where(i == CTX_BLK - 1, 2, 1))), 0, 0)

    return pl.pallas_call(
        _na_kernel,
        grid=(b, nb),
        in_specs=[
            pl.BlockSpec((None, TB, 256), lambda b, i: (b, i, 0)),
            pl.BlockSpec((None, T_ALL, 256), lambda b, i: (b, 0, 1)),
            pl.BlockSpec((None, T_ALL, 256), lambda b, i: (b, 0, 2)),
            pl.BlockSpec((None, TB, NA_KEYS), variant),
            _const_spec((4, NA_DR_ROWS - 1, GRID_W, 128)),
        ],
        out_specs=pl.BlockSpec((None, TB, 256), lambda b, i: (b, i, 0)),
        out_shape=jax.ShapeDtypeStruct((b, nb * TB, 256), BF16),
        compiler_params=_cparams(("arbitrary", "arbitrary")),
        name="na_attn",
    )(u_att, u_att, u_att, mask, t2)


def _swa_kernel(sink_ref, q_ref, k_ref, v_ref, o_ref):
    i = pl.program_id(1)
    q0 = i * TB
    ks0 = jnp.clip(q0 - SWA_WINDOW, 0, SEQ - SWA_KEYS)
    kstart = pl.multiple_of(ks0, SWA_WINDOW)
    rel = (lax.broadcasted_iota(jnp.int32, (TB, SWA_KEYS), 1)
           - lax.broadcasted_iota(jnp.int32, (TB, SWA_KEYS), 0) + (ks0 - q0))
    ok = (jnp.abs(rel) <= SWA_WINDOW) & (i < CTX_BLK)
    kw, vw = k_ref[pl.ds(kstart, SWA_KEYS), :], v_ref[pl.ds(kstart, SWA_KEYS), :]
    kc, vc = k_ref[SEQ:T_ALL, :], v_ref[SEQ:T_ALL, :]
    lane = lax.broadcasted_iota(jnp.int32, (TB, 128), 1)
    for g in range(2):
        q2 = q_ref[:, g * 128:(g + 1) * 128].astype(F32)
        gm = (lane >= 64 * g) & (lane < 64 * (g + 1))
        outs = []
        for j in range(2):
            qa = q2 if j == g else pltpu.roll(q2, 64, 1)
            qa = jnp.where(gm, qa, 0.0).astype(BF16)
            s_l = jnp.where(ok, _dot_nt(qa, kw), -jnp.inf)
            s_c = _dot_nt(qa, kc)
            sink = sink_ref[2 * g + j]
            m = jnp.maximum(jnp.maximum(jnp.max(s_l, axis=-1, keepdims=True),
                                        jnp.max(s_c, axis=-1, keepdims=True)), sink)
            p_l, p_c = jnp.exp(s_l - m), jnp.exp(s_c - m)
            l = jnp.sum(p_l, axis=-1, keepdims=True) + jnp.sum(p_c, axis=-1, keepdims=True) + jnp.exp(sink - m)
            o = (_dot(p_l.astype(BF16), vw) + _dot(p_c.astype(BF16), vc)) * (1.0 / l)
            outs.append(o if j == g else pltpu.roll(o, 64, 1))
        o_ref[:, g * 128:(g + 1) * 128] = jnp.where(lane < 64, outs[0], outs[1]).astype(BF16)


def _swa_call(u_att, sink, with_ctx):
    b = u_att.shape[0]
    nb = NBLK if with_ctx else CTX_BLK
    return pl.pallas_call(
        _swa_kernel,
        grid=(b, nb),
        in_specs=[
            pl.BlockSpec(memory_space=pltpu.SMEM),
            pl.BlockSpec((None, TB, 256), lambda b, i: (b, i, 3)),
            pl.BlockSpec((None, T_ALL, 128), lambda b, i: (b, 0, 8)),
            pl.BlockSpec((None, T_ALL, 128), lambda b, i: (b, 0, 9)),
        ],
        out_specs=pl.BlockSpec((None, TB, 256), lambda b, i: (b, i, 0)),
        out_shape=jax.ShapeDtypeStruct((b, nb * TB, 256), BF16),
        compiler_params=_cparams(("arbitrary", "arbitrary")),
        name="swa_attn",
    )(sink, u_att, u_att, u_att)


FN_XP = 136
FN_YP = 136
FN_ZP = 72


def _fnet_kernel(x_ref, f64_ref, g_ref, fc_ref, p_ref, q_ref, xz_ref, zq_ref, y_ref):
    sc_lat, sc_ctx = (64.0 * SEQ) ** -0.5, (64.0 * CTX) ** -0.5
    pq = _dot(fc_ref[...], x_ref[SEQ:T_ALL, :].astype(BF16))
    p_ref[SEQ:T_ALL, :] = (pq[0:CTX] * sc_ctx).astype(BF16)
    q_ref[SEQ:T_ALL, :] = (pq[CTX:2 * CTX] * sc_ctx).astype(BF16)
    f64 = f64_ref[...]

    def repitch(t1, carry):
        xz_ref[pl.ds(pl.multiple_of(t1 * FN_XP, 8), 128), :] = x_ref[pl.ds(pl.multiple_of(t1 * 128, 128), 128), :]
        return carry

    lax.fori_loop(0, 64, repitch, 0, unroll=4)

    def stage1(t2, carry):
        xs = xz_ref[pl.ds(t2, 64, stride=FN_XP), :].astype(BF16)
        y_ref[pl.ds(t2, 128, stride=FN_YP), :] = _dot(f64, xs)
        return carry

    lax.fori_loop(0, 128, stage1, 0, unroll=8)

    def stage2(k1, carry):
        yre = y_ref[pl.ds(pl.multiple_of(k1 * FN_YP, 8), 128), :]
        yim = y_ref[pl.ds(pl.multiple_of((64 + k1) * FN_YP, 8), 128), :]
        ycat = jnp.concatenate([yre, yim], axis=0).astype(BF16)
        pq = _dot(g_ref[k1], ycat)
        xz_ref[pl.ds(k1, 128, stride=FN_ZP), :] = pq[0:128] * sc_lat
        zq_ref[pl.ds(k1, 128, stride=FN_ZP), :] = pq[128:256] * sc_lat
        return carry

    lax.fori_loop(0, 64, stage2, 0, unroll=4)

    def emit(k2, carry):
        src = pl.ds(pl.multiple_of(k2 * FN_ZP, 8), 64)
        dst = pl.ds(pl.multiple_of(k2 * 64, 64), 64)
        p_ref[dst, :] = xz_ref[src, :].astype(BF16)
        q_ref[dst, :] = zq_ref[src, :].astype(BF16)
        return carry

    lax.fori_loop(0, 128, emit, 0, unroll=4)


def _fnet_call(u_f32, f64, g2, fc):
    b = u_f32.shape[0]
    spec = pl.BlockSpec((None, T_ALL, 128), lambda b, j: (b, 0, j))
    return pl.pallas_call(
        _fnet_kernel,
        grid=(b, 2),
        in_specs=[spec, _const_spec((128, 64)), _const_spec((64, 256, 256)), _const_spec((2 * CTX, CTX))],
        out_specs=[spec, spec],
        out_shape=[jax.ShapeDtypeStruct((b, T_ALL, 256), BF16)] * 2,
        scratch_shapes=[pltpu.VMEM((128 * FN_ZP, 128), F32), pltpu.VMEM((128 * FN_ZP, 128), F32),
                        pltpu.VMEM((128 * FN_YP, 128), F32)],
        compiler_params=_cparams(("arbitrary", "arbitrary")),
        name="fnet_dft",
    )(u_f32, f64, g2, fc)


def _hgrn_gates(qd, fz, gl):
    q = qd * jax.nn.sigmoid(qd)
    sp = jnp.log(1.0 + jnp.exp(-jnp.abs(fz)))
    b_ = gl[1:2] + jnp.minimum(fz, 0.0) - sp
    a_ = gl[0:1]
    logf = jnp.maximum(a_, b_) + jnp.log(1.0 + jnp.exp(-jnp.abs(a_ - b_)))
    kk = jnp.exp(gl[1:2] + jnp.minimum(-fz, 0.0) - sp)
    return q, kk, logf


def _dot01(b01, x):
    h1 = x.astype(BF16)
    r1 = x - h1.astype(F32)
    h2 = r1.astype(BF16)
    h3 = (r1 - h2.astype(F32)).astype(BF16)
    return _dot(b01, h1) + _dot(b01, h2) + _dot(b01, h3)


def _head_masks(rows):
    lane = lax.broadcasted_iota(jnp.int32, (rows, 256), 1)
    return [(lane >= 64 * h) & (lane < 64 * (h + 1)) for h in range(4)]


def _stack_heads(w, hms):
    return jnp.concatenate([jnp.where(hm, w, jnp.zeros_like(w)) for hm in hms], axis=0)


def _state_update(st_ref, st, e_last, vb, ks):
    ktv = _dot_tn(vb, ks)
    r2 = lax.broadcasted_iota(jnp.int32, (256, 256), 0) // 64
    c2 = lax.broadcasted_iota(jnp.int32, (256, 256), 1) // 64
    st_ref[...] = st * e_last + jnp.where(r2 == c2, ktv, 0.0)


def _hgrn_chunk_exact(qd, fz, v, st_ref, gl, ball, lmask_ref, d, rev):
    q, kk, logf = _hgrn_gates(qd, fz, gl)
    e = jnp.exp(_dot01(ball, logf))
    row = lax.broadcasted_iota(jnp.int32, (CHUNK, 256), 0)
    tau = (CHUNK - 1 - row) if rev else row
    hms = _head_masks(CHUNK)
    att = _dot_nt(_stack_heads(q.astype(BF16), hms), kk.astype(BF16)) * lmask_ref[d, 0]
    for li, sh in enumerate((5, 4, 3, 2, 1, 0)):
        up = ((tau >> sh) & 1) == 1
        w = (jnp.where(up, q, kk) * e[128 + 64 * li:192 + 64 * li]).astype(BF16)
        att = att + _dot_nt(_stack_heads(w, hms), w) * lmask_ref[d, 1 + li]
    vb = v.astype(BF16)
    st = st_ref[...]
    o = _dot_nt((q * e[0:64]).astype(BF16), st.astype(BF16))
    for h in range(4):
        o = o + _dot(att[64 * h:64 * (h + 1)].astype(BF16), jnp.where(hms[h], vb, jnp.zeros_like(vb)))
    tl = 0 if rev else CHUNK - 1
    _state_update(st_ref, st, e[tl:tl + 1], vb, (kk * e[64:128]).astype(BF16))
    return o


def _hgrn_block_fast(q, kk, a, cum_mid, cum_last, v, st_ref, o_ref, rev):
    eq, ek = jnp.exp(a), jnp.exp(-a)
    hms = _head_masks(CHUNK)
    r = lax.broadcasted_iota(jnp.int32, (4 * CHUNK, CHUNK), 0) & (CHUNK - 1)
    s = lax.broadcasted_iota(jnp.int32, (4 * CHUNK, CHUNK), 1)
    causal = (s >= r) if rev else (s <= r)
    for c in (range(NCHUNK - 1, -1, -1) if rev else range(NCHUNK)):
        sl = slice(c * CHUNK, (c + 1) * CHUNK)
        qt, kt = q[sl] * eq[sl], kk[sl] * ek[sl]
        vb = v[sl].astype(BF16)
        att = jnp.where(causal, _dot_nt(_stack_heads(qt.astype(BF16), hms), kt.astype(BF16)), 0.0)
        st = st_ref[...]
        o = _dot_nt((qt * jnp.exp(cum_mid[c])).astype(BF16), st.astype(BF16))
        for h in range(4):
            o = o + _dot(att[64 * h:64 * (h + 1)].astype(BF16), jnp.where(hms[h], vb, jnp.zeros_like(vb)))
        o_ref[sl, :] = o
        _state_update(st_ref, st, jnp.exp(cum_last[c]), vb, (kt * jnp.exp(cum_last[c] - cum_mid[c])).astype(BF16))


def _hgrn_kernel(qf_ref, ff_ref, vf_ref, qb_ref, fb_ref, vb_ref, gl_ref, ball_ref, lmask_ref, bblk_ref,
                 of_ref, ob_ref, sf_ref, sb_ref):
    @pl.when(pl.program_id(1) == 0)
    def _():
        sf_ref[...] = jnp.zeros_like(sf_ref)
        sb_ref[...] = jnp.zeros_like(sb_ref)

    dirs = ((qf_ref, ff_ref, vf_ref, sf_ref, of_ref, False), (qb_ref, fb_ref, vb_ref, sb_ref, ob_ref, True))
    pre, rng = [], 0.0
    for d, (q_ref, f_ref, v_ref, st_ref, o_ref, rev) in enumerate(dirs):
        q, kk, logf = _hgrn_gates(q_ref[...], f_ref[...], gl_ref[d])
        cum = _dot01(bblk_ref[d], logf)
        mid_row, last_row = (CHUNK // 2, 0) if rev else (CHUNK // 2 - 1, CHUNK - 1)
        cum_mid = [cum[c * CHUNK + mid_row:c * CHUNK + mid_row + 1] for c in range(NCHUNK)]
        cum_last = [cum[c * CHUNK + last_row:c * CHUNK + last_row + 1] for c in range(NCHUNK)]
        a = jnp.concatenate([cum[c * CHUNK:(c + 1) * CHUNK] - cum_mid[c] for c in range(NCHUNK)], axis=0)
        rng = jnp.maximum(rng, jnp.max(jnp.abs(a)))
        pre.append((q, kk, a, cum_mid, cum_last))
    fast = rng <= HGRN_FAST_RANGE

    @pl.when(fast)
    def _():
        for d, (q_ref, f_ref, v_ref, st_ref, o_ref, rev) in enumerate(dirs):
            q, kk, a, cum_mid, cum_last = pre[d]
            _hgrn_block_fast(q, kk, a, cum_mid, cum_last, v_ref[...], st_ref, o_ref, rev)

    @pl.when(jnp.logical_not(fast))
    def _():
        def body(c, carry):
            rf = pl.ds(pl.multiple_of(c * CHUNK, CHUNK), CHUNK)
            rb = pl.ds(pl.multiple_of((NCHUNK - 1 - c) * CHUNK, CHUNK), CHUNK)
            of_ref[rf, :] = _hgrn_chunk_exact(qf_ref[rf, :], ff_ref[rf, :], vf_ref[rf, :], sf_ref,
                                              gl_ref[0], ball_ref[0], lmask_ref, 0, False)
            ob_ref[rb, :] = _hgrn_chunk_exact(qb_ref[rb, :], fb_ref[rb, :], vb_ref[rb, :], sb_ref,
                                              gl_ref[1], ball_ref[1], lmask_ref, 1, True)
            return carry

        lax.fori_loop(0, NCHUNK, body, 0)


def _hgrn_call(u_f32, gl, ball, lmask, bblk):
    b = u_f32.shape[0]

    def fwd(col):
        return pl.BlockSpec((None, TB, 256), lambda b, i: (b, jnp.where(i == 0, CTX_BLK, i - 1), col))

    def bwd(col):
        return pl.BlockSpec((None, TB, 256), lambda b, i: (b, jnp.where(i == 0, CTX_BLK, CTX_BLK - i), col))

    return pl.pallas_call(
        _hgrn_kernel,
        grid=(b, NBLK),
        in_specs=[fwd(1), fwd(2), fwd(4), bwd(1), bwd(3), bwd(4),
                  _const_spec((2, 2, 256)), _const_spec((2, 8 * CHUNK, CHUNK)),
                  _const_spec((2, 7, 4 * CHUNK, CHUNK)), _const_spec((2, TB, TB))],
        out_specs=[fwd(0), bwd(0)],
        out_shape=[jax.ShapeDtypeStruct((b, T_ALL, 256), F32)] * 2,
        scratch_shapes=[pltpu.VMEM((256, 256), F32), pltpu.VMEM((256, 256), F32)],
        compiler_params=_cparams(("arbitrary", "arbitrary")),
        name="hgrn_scan",
    )(u_f32, u_f32, u_f32, u_f32, u_f32, u_f32, gl, ball, lmask, bblk)


def _outproj_kernel(*refs, nx, with_ctx):
    x = _x_tile(refs[:nx])
    (oa_ref, p_ref, q_ref, oc_ref, of_ref, ob_ref, gd_ref, hg_ref, hh_ref, cs_ref,
     wf_ref, wo_ref, gp_ref, gtc_ref, gtb_ref, o_ref) = refs[nx:]
    z = _dot(p_ref[...].astype(BF16), cs_ref[0:256, :]) + _dot(q_ref[...].astype(BF16), cs_ref[256:512, :])
    o_b = _dot(z.astype(BF16), wf_ref[...])
    od = of_ref[...] + ob_ref[...]
    sq = od * od
    hi = sq.astype(BF16)
    lo = (sq - hi.astype(F32)).astype(BF16)
    ms = (_dot(hi, hh_ref[...]) + _dot(lo, hh_ref[...])) * (1.0 / HEAD_DIM)
    gd = gd_ref[...]
    o_d = od * lax.rsqrt(ms + EPS) * hg_ref[...] * (gd * jax.nn.sigmoid(gd))
    y = (_dot(oa_ref[...], wo_ref[0:256, :]) + _dot(o_b.astype(BF16), wo_ref[256:512, :])
         + _dot(oc_ref[...], wo_ref[512:768, :]) + _dot(o_d.astype(BF16), wo_ref[768:1024, :]))
    gt = _mod(gtc_ref, gtb_ref, o_ref.shape[0], with_ctx)
    o_ref[...] = x + gt * (_rms(y) * gp_ref[...])


def _outproj_call(x_parts, o_a, p, q, o_c, o_f, o_bk, u_f32, hg, hh, cs, w_fnet, w_out, g_post, mods, with_ctx):
    b = x_parts[0].shape[0]
    tm, t = (TM_ALL, T_ALL) if with_ctx else (TM_LAT, SEQ)
    col = lambda c: pl.BlockSpec((None, tm, 256), lambda b, i: (b, i, c))
    split = len(x_parts) == 2
    x_args = (x_parts[0],) * 3 + (x_parts[1],) if split else x_parts
    x_specs = _x_specs(True) if split else [pl.BlockSpec((None, tm, D_MODEL), lambda b, i: (b, i, 0))]
    return pl.pallas_call(
        functools.partial(_outproj_kernel, nx=len(x_args), with_ctx=with_ctx),
        grid=(b, t // tm),
        in_specs=[
            *x_specs,
            col(0), col(0), col(0), col(0), col(0), col(0), col(5),
            _const_spec((1, 256)), _const_spec((256, 256)), _const_spec((512, 256)),
            _const_spec((256, 256)), _const_spec((D_MODEL, D_MODEL)), _const_spec((1, D_MODEL)),
            *_mod_specs(2),
        ],
        out_specs=pl.BlockSpec((None, tm, D_MODEL), lambda b, i: (b, i, 0)),
        out_shape=jax.ShapeDtypeStruct((b, t, D_MODEL), F32),
        compiler_params=_cparams(("arbitrary", "arbitrary")),
        name="out_proj",
    )(*x_args, o_a, p, q, o_c, o_f, o_bk, u_f32, hg, hh, cs, w_fnet, w_out, g_post, mods, mods)


def _ffn_kernel(x_ref, g_ref, shc_ref, shb_ref, scc_ref, scb_ref, w1_ref, w2_ref, gp_ref, gtc_ref, gtb_ref,
                o_ref, *, with_ctx):
    tm = x_ref.shape[0]
    x = x_ref[...]
    sh, sc = _mod(shc_ref, shb_ref, tm, with_ctx), _mod(scc_ref, scb_ref, tm, with_ctx)
    h = ((_rms(x) * g_ref[...]) * (1.0 + sc) + sh).astype(BF16)
    acc = jnp.zeros((tm, D_MODEL), F32)
    fc = 256
    for c in range(D_FF // fc):
        a = _dot(h, w1_ref[:, c * fc:(c + 1) * fc])
        g = _dot(h, w1_ref[:, D_FF + c * fc:D_FF + (c + 1) * fc])
        act = (a * jax.nn.sigmoid(a) * g).astype(BF16)
        acc = acc + _dot(act, w2_ref[c * fc:(c + 1) * fc, :])
    gt = _mod(gtc_ref, gtb_ref, tm, with_ctx)
    o_ref[...] = x + gt * (_rms(acc) * gp_ref[...])


def _ffn_call(xs, g_pre, w1, w2, g_post, mods, with_ctx):
    b, t, _ = xs.shape
    tm = TM_ALL if with_ctx else TM_LAT
    return pl.pallas_call(
        functools.partial(_ffn_kernel, with_ctx=with_ctx),
        grid=(b, t // tm),
        in_specs=[
            pl.BlockSpec((None, tm, D_MODEL), lambda b, i: (b, i, 0)),
            _const_spec((1, D_MODEL)),
            *_mod_specs(3), *_mod_specs(4),
            _const_spec((D_MODEL, 2 * D_FF)), _const_spec((D_FF, D_MODEL)), _const_spec((1, D_MODEL)),
            *_mod_specs(5),
        ],
        out_specs=pl.BlockSpec((None, tm, D_MODEL), lambda b, i: (b, i, 0)),
        out_shape=jax.ShapeDtypeStruct(xs.shape, F32),
        compiler_params=_cparams(("arbitrary", "arbitrary")),
        name="ffn",
    )(xs, g_pre, mods, mods, mods, mods, w1, w2, g_post, mods, mods)


def kernel(x, c, ctx, c_ctx, w_ada, b_ada, g_pre_mix, g_post_mix, g_pre_ffn, g_post_ffn, w_in, w_out,
           na_rpb, w_fnet, swa_sink, hgrn_lb_logits, hgrn_norm_g, w_ffn_in, w_ffn_out):
    tb = _tables()
    bf = lambda name: jnp.asarray(tb[name]).astype(BF16)
    cos, sin = jnp.asarray(tb["rope_cos"]), jnp.asarray(tb["rope_sin"])
    f64, g2, fc, chan_cs, hh = bf("f64"), bf("g_stage2"), bf("f_ctx"), bf("chan_cs"), bf("head_ones")
    ball, bblk, lmask = bf("hgrn_b"), bf("hgrn_bblk"), jnp.asarray(tb["hgrn_mask"])
    na_mask = jnp.asarray(tb["na_mask"])

    cc = jnp.concatenate([c, c_ctx[None, :]], axis=0)
    mods = _ada_call(jnp.broadcast_to(cc[:, :, None], (3, D_MODEL, 128)), w_ada, b_ada)

    lb = jnp.cumsum(jax.nn.softmax(hgrn_lb_logits.astype(F32), axis=1), axis=1)
    lb = lb - lb[:, :1]
    gl_all = jnp.stack([jnp.log(lb), jnp.log1p(-lb)], axis=2)

    xs = None
    for l in range(DEPTH):
        with_ctx = l < DEPTH - 1
        row = lambda a: a[l][None, :]
        x_all = (x, ctx) if l == 0 else (xs,)
        x_res = x_all if with_ctx else x_all[:1]
        u_att, u_f32 = _proj_call(x_all, row(g_pre_mix), mods[l], w_in[l].astype(BF16), cos, sin)

        rpb = jnp.pad(na_rpb[l], ((0, 0), (NA_DR_PAD, NA_DR_ROWS - NA_DR_PAD - (2 * NA_WIN_R - 1)), (0, 0)))
        t2 = _na_bias_call(rpb[:, :, tb["na_lane_l"]], rpb[:, :, tb["na_lane_r"]])
        o_a = _na_call(u_att, na_mask, t2, with_ctx)
        o_c = _swa_call(u_att, swa_sink[l], with_ctx)
        p, q = _fnet_call(u_f32, f64, g2, fc)
        o_f, o_bk = _hgrn_call(u_f32, gl_all[:, l], ball, lmask, bblk)

        xs = _outproj_call(x_res, o_a, p, q, o_c, o_f, o_bk, u_f32, row(hgrn_norm_g), hh, chan_cs,
                           w_fnet[l].astype(BF16), w_out[l].astype(BF16), row(g_post_mix), mods[l], with_ctx)
        xs = _ffn_call(xs, row(g_pre_ffn), w_ffn_in[l].astype(BF16), w_ffn_out[l].astype(BF16),
                       row(g_post_ffn), mods[l], with_ctx)
    return xs
```

```python
import functools

import numpy as np
import jax
import jax.numpy as jnp
from jax import lax
from jax.experimental import pallas as pl
from jax.experimental.pallas import tpu as pltpu

F32, BF16 = jnp.float32, jnp.bfloat16

D_MODEL = 1024
SEQ = 8192
DEPTH = 2
GRID_W = 64
GRID_ROWS = SEQ // GRID_W
CTX = 256
HEAD_DIM = 64
NA_WIN_R, NA_WIN_C = 8, 16
SWA_WINDOW = 128
ROPE_THETA = 10000.0
EPS = 1e-6
D_FF = 2816
D_IN = 2816

TB = 256
T_ALL = SEQ + CTX
NBLK = T_ALL // TB
CTX_BLK = SEQ // TB
TM_ALL = 768
TM_LAT = 1024
NA_KROWS = 12
NA_KEYS = NA_KROWS * GRID_W
NA_DR_PAD = 4
NA_DR_ROWS = 24
SWA_KEYS = TB + 2 * SWA_WINDOW
CHUNK = 64
NCHUNK = TB // CHUNK
HGRN_FAST_RANGE = 60.0
Q_SCALE = HEAD_DIM ** -0.5
VMEM_LIMIT = 56 * 1024 * 1024

ATT_W, F32_W = 1280, 1536


def _dot(a, b):
    return jnp.dot(a, b, preferred_element_type=F32)


def _dot_nt(a, b):
    return lax.dot_general(a, b, (((1,), (1,)), ((), ())), preferred_element_type=F32)


def _dot_tn(a, b):
    return lax.dot_general(a, b, (((0,), (0,)), ((), ())), preferred_element_type=F32)


def _cparams(sem):
    return pltpu.CompilerParams(dimension_semantics=sem, vmem_limit_bytes=VMEM_LIMIT)


def _const_spec(shape):
    nd = len(shape)
    return pl.BlockSpec(shape, lambda *_: (0,) * nd, pipeline_mode=pl.Buffered(1))


@functools.lru_cache(maxsize=None)
def _tables():
    t = {}
    lane = np.arange(128)
    d = lane % 64
    freq = ROPE_THETA ** (-(d % 16) / 16.0)
    pos = np.arange(SEQ)
    p = np.where(d[None, :] < 32, (pos // GRID_W)[:, None], (pos % GRID_W)[:, None]).astype(np.float64)
    ang = p * freq[None, :]
    sign = np.where((d % 32) < 16, -1.0, 1.0)
    cos = np.concatenate([np.cos(ang), np.ones((CTX, 128))], axis=0)
    sin = np.concatenate([np.sin(ang) * sign[None, :], np.zeros((CTX, 128))], axis=0)
    t["rope_cos"], t["rope_sin"] = cos.astype(np.float32), sin.astype(np.float32)

    masks = []
    a, cq = np.arange(4), np.arange(GRID_W)
    rho, ck = np.arange(NA_KROWS), np.arange(GRID_W)
    for r0 in (0, 8, GRID_ROWS - 4):
        kr0 = int(np.clip(r0 - 4, 0, GRID_ROWS - NA_KROWS))
        rq, rk = r0 + a, kr0 + rho
        start = np.clip(rq - NA_WIN_R // 2, 0, GRID_ROWS - NA_WIN_R)
        vr = (rk[None, :] >= start[:, None]) & (rk[None, :] < start[:, None] + NA_WIN_R)
        cs = np.clip(cq - NA_WIN_C // 2, 0, GRID_W - NA_WIN_C)
        vc = (ck[None, :] >= cs[:, None]) & (ck[None, :] < cs[:, None] + NA_WIN_C)
        val = vr[:, None, :, None] & vc[None, :, None, :]
        masks.append(np.where(val, 0.0, -np.inf).reshape(TB, NA_KEYS))
    masks.append(np.full((TB, NA_KEYS), -np.inf))
    t["na_mask"] = np.stack(masks).astype(np.float32)
    n = np.arange(128)
    clipc = lambda m: np.clip(m, -(NA_WIN_C - 1), NA_WIN_C - 1) + NA_WIN_C - 1
    t["na_lane_l"] = clipc(np.where(n < 64, n, n - 128)).astype(np.int32)
    t["na_lane_r"] = clipc(n - 64).astype(np.int32)

    rel = np.arange(SWA_KEYS)[None, :] - np.arange(TB)[:, None]
    swa = [np.where(np.abs(rel + off) <= SWA_WINDOW, 0.0, -np.inf) for off in (0, -SWA_WINDOW, -2 * SWA_WINDOW)]
    swa.append(np.full((TB, SWA_KEYS), -np.inf))
    t["swa_mask"] = np.stack(swa).astype(np.float32)

    def cs_tab(num, den):
        ang = 2.0 * np.pi * (num % den) / den
        return np.cos(ang), np.sin(ang)

    k1, t1 = np.arange(64)[:, None], np.arange(64)[None, :]
    c64, s64 = cs_tab(k1 * t1, 64)
    t["f64"] = np.concatenate([c64, -s64], axis=0).astype(np.float32)
    k2, t2 = np.arange(128)[:, None], np.arange(128)[None, :]
    g = []
    for kk in range(64):
        gc, gs = cs_tab((64 * k2 + kk) * t2, SEQ)
        g.append(np.block([[gc, gs], [-gs, gc]]))
    t["g_stage2"] = np.stack(g).astype(np.float32)
    kc, tc = np.arange(CTX)[:, None], np.arange(CTX)[None, :]
    cc, sc = cs_tab(kc * tc, CTX)
    t["f_ctx"] = np.concatenate([cc, -sc], axis=0).astype(np.float32)
    ch = np.arange(256)
    same = (ch[:, None] // 64) == (ch[None, :] // 64)
    c_ch, s_ch = cs_tab((ch[:, None] % 64) * (ch[None, :] % 64), 64)
    t["chan_cs"] = np.concatenate([np.where(same, c_ch, 0.0), np.where(same, s_ch, 0.0)], axis=0).astype(np.float32)
    t["head_ones"] = same.astype(np.float32)

    balls, lmasks, bblk = [], [], []
    for rev in (False, True):
        tau = (CHUNK - 1 - np.arange(CHUNK)) if rev else np.arange(CHUNK)
        tt, tj = tau[:, None], tau[None, :]
        blocks = [(tj <= tt), (tj > tt)]
        masks = [tt == tj]
        for h in (32, 16, 8, 4, 2, 1):
            up = ((tt // h) % 2) == 1
            mid = (tt // (2 * h)) * (2 * h) + h - 1
            blocks.append(np.where(up, (tj > mid) & (tj <= tt), (tj > tt) & (tj <= mid)))
            up_s = ((tj // h) % 2) == 1
            masks.append((tt // (2 * h) == tj // (2 * h)) & up & ~up_s)
        balls.append(np.concatenate(blocks, axis=0))
        lmasks.append(np.stack([np.tile(m, (4, 1)) for m in masks]))
        bblk.append(np.kron(np.eye(NCHUNK), blocks[0]))
    t["hgrn_b"] = np.stack(balls).astype(np.float32)
    t["hgrn_mask"] = np.stack(lmasks).astype(np.float32)
    t["hgrn_bblk"] = np.stack(bblk).astype(np.float32)
    return t


def _ada_kernel(cb_ref, w_ref, b_ref, o_ref, s_ref):
    tn = w_ref.shape[1]
    nrep = tn // 128

    @pl.when((pl.program_id(0) == 0) & (pl.program_id(1) == 0))
    def _():
        cb = cb_ref[...]
        s_ref[...] = cb * jax.nn.sigmoid(cb)

    def body(kc, accs):
        k0 = pl.multiple_of(kc * 8, 8)
        wt = w_ref[pl.ds(k0, 8), :]
        return tuple(accs[r] + wt * jnp.concatenate([s_ref[r, pl.ds(k0, 8), :]] * nrep, axis=1) for r in range(3))

    accs = lax.fori_loop(0, D_MODEL // 8, body, tuple(jnp.zeros((8, tn), F32) for _ in range(3)), unroll=8)
    for r in range(3):
        o_ref[r] = jnp.sum(accs[r], axis=0, keepdims=True) + b_ref[...]


def _ada_call(cb, w_ada, b_ada):
    tn = 1536
    return pl.pallas_call(
        _ada_kernel,
        grid=(DEPTH, 6 * D_MODEL // tn),
        in_specs=[
            pl.BlockSpec((3, D_MODEL, 128), lambda l, j: (0, 0, 0)),
            pl.BlockSpec((None, D_MODEL, tn), lambda l, j: (l, 0, j)),
            pl.BlockSpec((None, 1, tn), lambda l, j: (l, 0, j)),
        ],
        out_specs=pl.BlockSpec((None, 3, 1, tn), lambda l, j: (l, 0, 0, j)),
        out_shape=jax.ShapeDtypeStruct((DEPTH, 3, 1, 6 * D_MODEL), F32),
        scratch_shapes=[pltpu.VMEM((3, D_MODEL, 128), F32)],
        compiler_params=_cparams(("arbitrary", "arbitrary")),
        name="adaln",
    )(cb, w_ada, b_ada.reshape(DEPTH, 1, 6 * D_MODEL))


def _mod_specs(col):
    return [pl.BlockSpec((None, 1, D_MODEL), lambda b, i: (2, 0, col)),
            pl.BlockSpec((None, 1, D_MODEL), lambda b, i: (b, 0, col))]


def _mod(mc_ref, mb_ref, tm, with_ctx):
    if not with_ctx:
        return mb_ref[...]
    rows = pl.program_id(1) * tm + lax.broadcasted_iota(jnp.int32, (tm, 1), 0)
    return jnp.where(rows >= SEQ, mc_ref[...], mb_ref[...])


def _rms(x):
    return x * lax.rsqrt(jnp.mean(x * x, axis=-1, keepdims=True) + EPS)


def _x_specs(split):
    if not split:
        return [pl.BlockSpec((None, TM_ALL, D_MODEL), lambda b, i: (b, i, 0))]
    nx = SEQ // TB
    return [pl.BlockSpec((None, TB, D_MODEL), lambda b, i, j=j: (b, jnp.minimum(3 * i + j, nx - 1), 0))
            for j in range(TM_ALL // TB)] + [pl.BlockSpec((None, CTX, D_MODEL), lambda b, i: (b, 0, 0))]


def _x_tile(x_refs):
    if len(x_refs) == 1:
        return x_refs[0][...]
    xa, xb, xc, cx = x_refs
    last = pl.program_id(1) == pl.num_programs(1) - 1
    return jnp.concatenate([xa[...], xb[...], jnp.where(last, cx[...], xc[...])], axis=0)


def _proj_kernel(*refs, nx):
    x = _x_tile(refs[:nx])
    g_ref, shc_ref, shb_ref, scc_ref, scb_ref, w_ref, cos_ref, sin_ref, att_ref, f32_ref = refs[nx:]
    tm = TM_ALL
    sh, sc = _mod(shc_ref, shb_ref, tm, True), _mod(scc_ref, scb_ref, tm, True)
    h = ((_rms(x) * g_ref[...]) * (1.0 + sc) + sh).astype(BF16)
    cos, sin = cos_ref[...], sin_ref[...]
    first = (lax.broadcasted_iota(jnp.int32, (tm, 128), 1) % 32) < 16

    def rope(v):
        return v * cos + jnp.where(first, pltpu.roll(v, 112, 1), pltpu.roll(v, 16, 1)) * sin

    a = _dot(h, w_ref[:, 0:768])
    att_ref[:, 0:256] = (a[:, 0:256] * Q_SCALE).astype(BF16)
    att_ref[:, 256:768] = a[:, 256:768].astype(BF16)
    f32_ref[:, 0:256] = _dot(h, w_ref[:, 768:1024])
    s = _dot(h, w_ref[:, 1024:1536])
    att_ref[:, 768:896] = (rope(s[:, 0:128]) * Q_SCALE).astype(BF16)
    att_ref[:, 896:1024] = (rope(s[:, 128:256]) * Q_SCALE).astype(BF16)
    att_ref[:, 1024:1152] = rope(s[:, 256:384]).astype(BF16)
    att_ref[:, 1152:1280] = s[:, 384:512].astype(BF16)
    f32_ref[:, 256:1536] = _dot(h, w_ref[:, 1536:2816])


def _proj_call(x_parts, g, mods, w_in, cos, sin):
    b = x_parts[0].shape[0]
    tm = TM_ALL
    split = len(x_parts) == 2
    x_args = (x_parts[0],) * 3 + (x_parts[1],) if split else x_parts
    return pl.pallas_call(
        functools.partial(_proj_kernel, nx=len(x_args)),
        grid=(b, T_ALL // tm),
        in_specs=[
            *_x_specs(split),
            _const_spec((1, D_MODEL)),
            *_mod_specs(0), *_mod_specs(1),
            _const_spec((D_MODEL, D_IN)),
            pl.BlockSpec((tm, 128), lambda b, i: (i, 0)),
            pl.BlockSpec((tm, 128), lambda b, i: (i, 0)),
        ],
        out_specs=[
            pl.BlockSpec((None, tm, ATT_W), lambda b, i: (b, i, 0)),
            pl.BlockSpec((None, tm, F32_W), lambda b, i: (b, i, 0)),
        ],
        out_shape=[jax.ShapeDtypeStruct((b, T_ALL, ATT_W), BF16), jax.ShapeDtypeStruct((b, T_ALL, F32_W), F32)],
        compiler_params=_cparams(("arbitrary", "arbitrary")),
        name="proj_in",
    )(*x_args, g, mods, mods, mods, mods, w_in, cos, sin)


def _na_bias_kernel(vl_ref, vr_ref, o_ref):
    lane = lax.broadcasted_iota(jnp.int32, (GRID_W, 128), 1)
    for e in range(NA_DR_ROWS - 1):
        left = pltpu.roll(jnp.broadcast_to(vl_ref[e:e + 1, :], (GRID_W, 128)), 0, 1, stride=1, stride_axis=0)
        right = pltpu.roll(jnp.broadcast_to(vr_ref[e + 1:e + 2, :], (GRID_W, 128)), 0, 1, stride=1, stride_axis=0)
        o_ref[e] = jnp.where(lane < 64, left, right)


def _na_bias_call(vl, vr):
    spec = pl.BlockSpec((None, NA_DR_ROWS, 128), lambda h: (h, 0, 0))
    return pl.pallas_call(
        _na_bias_kernel,
        grid=(4,),
        in_specs=[spec, spec],
        out_specs=pl.BlockSpec((None, NA_DR_ROWS - 1, GRID_W, 128), lambda h: (h, 0, 0, 0)),
        out_shape=jax.ShapeDtypeStruct((4, NA_DR_ROWS - 1, GRID_W, 128), F32),
        compiler_params=_cparams(("arbitrary",)),
        name="na_bias",
    )(vl, vr)


AT_ROWS = 16


def _attn_block_variant(b, i):
    return (jnp.where(i == CTX_BLK, 3, jnp.where(i == 0, 0, jnp.where(i == CTX_BLK - 1, 2, 1))), 0, 0)


def _lane_half(shape, j):
    lane = lax.broadcasted_iota(jnp.int32, shape, 1)
    return (lane >= 64 * j) & (lane < 64 * (j + 1))


def _na_kernel(q_ref, k_ref, v_ref, mask_ref, t2_ref, o_ref, s_ref, p_ref):
    i = pl.program_id(1)
    is_ctx = i == CTX_BLK
    r0 = i * 4
    kr0 = jnp.where(is_ctx, 0, jnp.clip(r0 - 4, 0, GRID_ROWS - NA_KROWS))
    kstart = pl.multiple_of(kr0 * GRID_W, GRID_W)
    delta = jnp.where(is_ctx, 3, kr0 - r0 + NA_WIN_R - 1) + NA_DR_PAD
    lane = lax.broadcasted_iota(jnp.int32, (TB, 128), 1)
    for hp in range(2):
        cs = slice(hp * 128, (hp + 1) * 128)
        q2 = q_ref[:, cs]
        kw, vw = k_ref[pl.ds(kstart, NA_KEYS), cs], v_ref[pl.ds(kstart, NA_KEYS), cs]
        kc, vc = k_ref[SEQ:T_ALL, cs], v_ref[SEQ:T_ALL, cs]
        outs = []
        for j in range(2):
            h = 2 * hp + j
            qm = jnp.where(_lane_half((TB, 128), j), q2, jnp.zeros_like(q2))
            s_ref[h, :, 0:CTX] = _dot_nt(qm, kc)
            s_ref[h, :, CTX:] = _dot_nt(qm, kw)
            for r in range(TB // AT_ROWS):
                rows = slice(r * AT_ROWS, (r + 1) * AT_ROWS)
                a, sub = divmod(r, GRID_W // AT_ROWS)
                trow = slice(sub * AT_ROWS, (sub + 1) * AT_ROWS)
                bias = jnp.concatenate([t2_ref[h, delta + 2 * rp - a, trow, :] for rp in range(NA_KROWS // 2)], axis=1)
                s_nb = s_ref[h, rows, CTX:] + bias + mask_ref[rows, :]
                s_c = s_ref[h, rows, 0:CTX]
                m = jnp.maximum(jnp.max(s_nb, axis=-1, keepdims=True), jnp.max(s_c, axis=-1, keepdims=True))
                p_ref[h, rows, 0:CTX] = jnp.exp((s_c - m).astype(BF16))
                p_ref[h, rows, CTX:] = jnp.exp((s_nb - m).astype(BF16))
            o = (_dot(p_ref[h, :, 0:CTX], jnp.where(_lane_half((CTX, 128), j), vc, jnp.ones_like(vc)))
                 + _dot(p_ref[h, :, CTX:], jnp.where(_lane_half((NA_KEYS, 128), j), vw, jnp.ones_like(vw))))
            outs.append(o * (1.0 / pltpu.roll(o, 64, 1)))
        o_ref[:, cs] = jnp.where(lane < 64, outs[0], outs[1]).astype(BF16)


def _na_call(u_att, mask, t2, with_ctx):
    b = u_att.shape[0]
    nb = NBLK if with_ctx else CTX_BLK
    return pl.pallas_call(
        _na_kernel,
        grid=(b, nb),
        in_specs=[
            pl.BlockSpec((None, TB, 256), lambda b, i: (b, i, 0)),
            pl.BlockSpec((None, T_ALL, 256), lambda b, i: (b, 0, 1)),
            pl.BlockSpec((None, T_ALL, 256), lambda b, i: (b, 0, 2)),
            pl.BlockSpec((None, TB, NA_KEYS), _attn_block_variant),
            _const_spec((4, NA_DR_ROWS - 1, GRID_W, 128)),
        ],
        out_specs=pl.BlockSpec((None, TB, 256), lambda b, i: (b, i, 0)),
        out_shape=jax.ShapeDtypeStruct((b, nb * TB, 256), BF16),
        scratch_shapes=[pltpu.VMEM((4, TB, CTX + NA_KEYS), F32), pltpu.VMEM((4, TB, CTX + NA_KEYS), BF16)],
        compiler_params=_cparams(("arbitrary", "arbitrary")),
        name="na_attn",
    )(u_att, u_att, u_att, mask, t2)


def _swa_kernel(sink_ref, q_ref, k_ref, v_ref, mask_ref, o_ref, s_ref, p_ref, m_ref):
    i = pl.program_id(1)
    ks0 = jnp.clip(i * TB - SWA_WINDOW, 0, SEQ - SWA_KEYS)
    kstart = pl.multiple_of(ks0, SWA_WINDOW)
    kw, vw = k_ref[pl.ds(kstart, SWA_KEYS), :], v_ref[pl.ds(kstart, SWA_KEYS), :]
    kc, vc = k_ref[SEQ:T_ALL, :], v_ref[SEQ:T_ALL, :]
    lane = lax.broadcasted_iota(jnp.int32, (TB, 128), 1)
    for g in range(2):
        q2 = q_ref[:, g * 128:(g + 1) * 128].astype(F32)
        outs = []
        for j in range(2):
            h = 2 * g + j
            qa = q2 if j == g else pltpu.roll(q2, 64, 1)
            qa = jnp.where(_lane_half((TB, 128), g), qa, 0.0).astype(BF16)
            s_ref[h, :, 0:CTX] = _dot_nt(qa, kc)
            s_ref[h, :, CTX:] = _dot_nt(qa, kw)
            sink = sink_ref[h]
            for r in range(TB // AT_ROWS):
                rows = slice(r * AT_ROWS, (r + 1) * AT_ROWS)
                s_l = s_ref[h, rows, CTX:] + mask_ref[rows, :]
                s_c = s_ref[h, rows, 0:CTX]
                m = jnp.maximum(jnp.maximum(jnp.max(s_l, axis=-1, keepdims=True),
                                            jnp.max(s_c, axis=-1, keepdims=True)), sink)
                p_ref[h, rows, 0:CTX] = jnp.exp((s_c - m).astype(BF16))
                p_ref[h, rows, CTX:] = jnp.exp((s_l - m).astype(BF16))
                m_ref[h, rows, :] = jnp.broadcast_to(m, (AT_ROWS, 128))
            o = (_dot(p_ref[h, :, 0:CTX], jnp.where(_lane_half((CTX, 128), g), vc, jnp.ones_like(vc)))
                 + _dot(p_ref[h, :, CTX:], jnp.where(_lane_half((SWA_KEYS, 128), g), vw, jnp.ones_like(vw))))
            o = o * (1.0 / (pltpu.roll(o, 64, 1) + jnp.exp(sink - m_ref[h])))
            outs.append(o if j == g else pltpu.roll(o, 64, 1))
        o_ref[:, g * 128:(g + 1) * 128] = jnp.where(lane < 64, outs[0], outs[1]).astype(BF16)


def _swa_call(u_att, sink, mask, with_ctx):
    b = u_att.shape[0]
    nb = NBLK if with_ctx else CTX_BLK
    return pl.pallas_call(
        _swa_kernel,
        grid=(b, nb),
        in_specs=[
            pl.BlockSpec(memory_space=pltpu.SMEM),
            pl.BlockSpec((None, TB, 256), lambda b, i: (b, i, 3)),
            pl.BlockSpec((None, T_ALL, 128), lambda b, i: (b, 0, 8)),
            pl.BlockSpec((None, T_ALL, 128), lambda b, i: (b, 0, 9)),
            pl.BlockSpec((None, TB, SWA_KEYS), _attn_block_variant),
        ],
        out_specs=pl.BlockSpec((None, TB, 256), lambda b, i: (b, i, 0)),
        out_shape=jax.ShapeDtypeStruct((b, nb * TB, 256), BF16),
        scratch_shapes=[pltpu.VMEM((4, TB, CTX + SWA_KEYS), F32), pltpu.VMEM((4, TB, CTX + SWA_KEYS), BF16),
                        pltpu.VMEM((4, TB, 128), F32)],
        compiler_params=_cparams(("arbitrary", "arbitrary")),
        name="swa_attn",
    )(sink, u_att, u_att, u_att, mask)


FN_XP = 136
FN_YP = 136
FN_ZP = 72


def _fnet_kernel(x_ref, f64_ref, g_ref, fc_ref, p_ref, q_ref, xz_ref, zq_ref, y_ref):
    sc_lat, sc_ctx = (64.0 * SEQ) ** -0.5, (64.0 * CTX) ** -0.5
    pq = _dot(fc_ref[...], x_ref[SEQ:T_ALL, :].astype(BF16))
    p_ref[SEQ:T_ALL, :] = (pq[0:CTX] * sc_ctx).astype(BF16)
    q_ref[SEQ:T_ALL, :] = (pq[CTX:2 * CTX] * sc_ctx).astype(BF16)
    f64 = f64_ref[...]

    def repitch(t1, carry):
        xz_ref[pl.ds(pl.multiple_of(t1 * FN_XP, 8), 128), :] = x_ref[pl.ds(pl.multiple_of(t1 * 128, 128), 128), :]
        return carry

    lax.fori_loop(0, 64, repitch, 0, unroll=4)

    def stage1(t2, carry):
        xs = xz_ref[pl.ds(t2, 64, stride=FN_XP), :].astype(BF16)
        y_ref[pl.ds(t2, 128, stride=FN_YP), :] = _dot(f64, xs)
        return carry

    lax.fori_loop(0, 128, stage1, 0, unroll=8)

    def stage2(k1, carry):
        yre = y_ref[pl.ds(pl.multiple_of(k1 * FN_YP, 8), 128), :]
        yim = y_ref[pl.ds(pl.multiple_of((64 + k1) * FN_YP, 8), 128), :]
        ycat = jnp.concatenate([yre, yim], axis=0).astype(BF16)
        pq = _dot(g_ref[k1], ycat)
        xz_ref[pl.ds(k1, 128, stride=FN_ZP), :] = pq[0:128] * sc_lat
        zq_ref[pl.ds(k1, 128, stride=FN_ZP), :] = pq[128:256] * sc_lat
        return carry

    lax.fori_loop(0, 64, stage2, 0, unroll=4)

    def emit(k2, carry):
        src = pl.ds(pl.multiple_of(k2 * FN_ZP, 8), 64)
        dst = pl.ds(pl.multiple_of(k2 * 64, 64), 64)
        p_ref[dst, :] = xz_ref[src, :].astype(BF16)
        q_ref[dst, :] = zq_ref[src, :].astype(BF16)
        return carry

    lax.fori_loop(0, 128, emit, 0, unroll=4)


def _fnet_call(u_f32, f64, g2, fc):
    b = u_f32.shape[0]
    spec = pl.BlockSpec((None, T_ALL, 128), lambda b, j: (b, 0, j))
    return pl.pallas_call(
        _fnet_kernel,
        grid=(b, 2),
        in_specs=[spec, _const_spec((128, 64)), _const_spec((64, 256, 256)), _const_spec((2 * CTX, CTX))],
        out_specs=[spec, spec],
        out_shape=[jax.ShapeDtypeStruct((b, T_ALL, 256), BF16)] * 2,
        scratch_shapes=[pltpu.VMEM((128 * FN_ZP, 128), F32), pltpu.VMEM((128 * FN_ZP, 128), F32),
                        pltpu.VMEM((128 * FN_YP, 128), F32)],
        compiler_params=_cparams(("arbitrary", "arbitrary")),
        name="fnet_dft",
    )(u_f32, f64, g2, fc)


def _hgrn_gates(qd, fz, gl, lb_zero):
    q = qd * jax.nn.sigmoid(qd)
    sp = jnp.log(1.0 + jnp.exp(-jnp.abs(fz)))
    ls = jnp.minimum(fz, 0.0) - sp
    lk = jnp.minimum(-fz, 0.0) - sp
    if lb_zero:
        return q, jnp.exp(lk), ls
    b_ = gl[1:2] + ls
    a_ = gl[0:1]
    logf = jnp.maximum(a_, b_) + jnp.log(1.0 + jnp.exp(-jnp.abs(a_ - b_)))
    return q, jnp.exp(gl[1:2] + lk), logf


def _dot01(b01, x):
    h1 = x.astype(BF16)
    r1 = x - h1.astype(F32)
    h2 = r1.astype(BF16)
    h3 = (r1 - h2.astype(F32)).astype(BF16)
    return _dot(b01, h1) + _dot(b01, h2) + _dot(b01, h3)


def _head_masks(rows):
    lane = lax.broadcasted_iota(jnp.int32, (rows, 256), 1)
    return [(lane >= 64 * h) & (lane < 64 * (h + 1)) for h in range(4)]


def _stack_heads(w, hms):
    return jnp.concatenate([jnp.where(hm, w, jnp.zeros_like(w)) for hm in hms], axis=0)


def _state_update(st_ref, st, e_last, vb, ks):
    ktv = _dot_tn(vb, ks)
    r2 = lax.broadcasted_iota(jnp.int32, (256, 256), 0) // 64
    c2 = lax.broadcasted_iota(jnp.int32, (256, 256), 1) // 64
    st_ref[...] = st * e_last + jnp.where(r2 == c2, ktv, 0.0)


def _hgrn_chunk_exact(qd, fz, v, st_ref, gl, ball, lmask_ref, d, rev, lb_zero):
    q, kk, logf = _hgrn_gates(qd, fz, gl, lb_zero)
    e = jnp.exp(_dot01(ball, logf))
    row = lax.broadcasted_iota(jnp.int32, (CHUNK, 256), 0)
    tau = (CHUNK - 1 - row) if rev else row
    hms = _head_masks(CHUNK)
    att = _dot_nt(_stack_heads(q.astype(BF16), hms), kk.astype(BF16)) * lmask_ref[d, 0]
    for li, sh in enumerate((5, 4, 3, 2, 1, 0)):
        up = ((tau >> sh) & 1) == 1
        w = (jnp.where(up, q, kk) * e[128 + 64 * li:192 + 64 * li]).astype(BF16)
        att = att + _dot_nt(_stack_heads(w, hms), w) * lmask_ref[d, 1 + li]
    vb = v.astype(BF16)
    st = st_ref[...]
    o = _dot_nt((q * e[0:64]).astype(BF16), st.astype(BF16))
    for h in range(4):
        o = o + _dot(att[64 * h:64 * (h + 1)].astype(BF16), jnp.where(hms[h], vb, jnp.zeros_like(vb)))
    tl = 0 if rev else CHUNK - 1
    _state_update(st_ref, st, e[tl:tl + 1], vb, (kk * e[64:128]).astype(BF16))
    return o


def _hgrn_block_fast(q, kk, a, cum_mid, cum_last, v, st_ref, o_ref, rev):
    eq, ek = jnp.exp(a), jnp.exp(-a)
    hms = _head_masks(CHUNK)
    r = lax.broadcasted_iota(jnp.int32, (4 * CHUNK, CHUNK), 0) & (CHUNK - 1)
    s = lax.broadcasted_iota(jnp.int32, (4 * CHUNK, CHUNK), 1)
    causal = (s >= r) if rev else (s <= r)
    for c in (range(NCHUNK - 1, -1, -1) if rev else range(NCHUNK)):
        sl = slice(c * CHUNK, (c + 1) * CHUNK)
        qt, kt = q[sl] * eq[sl], kk[sl] * ek[sl]
        vb = v[sl].astype(BF16)
        att = jnp.where(causal, _dot_nt(_stack_heads(qt.astype(BF16), hms), kt.astype(BF16)), 0.0)
        st = st_ref[...]
        o = _dot_nt((qt * jnp.exp(cum_mid[c])).astype(BF16), st.astype(BF16))
        for h in range(4):
            o = o + _dot(att[64 * h:64 * (h + 1)].astype(BF16), jnp.where(hms[h], vb, jnp.zeros_like(vb)))
        o_ref[sl, :] = o
        _state_update(st_ref, st, jnp.exp(cum_last[c]), vb, (kt * jnp.exp(cum_last[c] - cum_mid[c])).astype(BF16))


def _hgrn_kernel(qf_ref, ff_ref, vf_ref, qb_ref, fb_ref, vb_ref, gl_ref, ball_ref, lmask_ref, bblk_ref,
                 of_ref, ob_ref, sf_ref, sb_ref, sf0_ref, sb0_ref, *, lb_zero):
    @pl.when(pl.program_id(1) == 0)
    def _():
        sf_ref[...] = jnp.zeros_like(sf_ref)
        sb_ref[...] = jnp.zeros_like(sb_ref)

    sf0_ref[...] = sf_ref[...]
    sb0_ref[...] = sb_ref[...]
    dirs = ((qf_ref, ff_ref, vf_ref, sf_ref, of_ref, False), (qb_ref, fb_ref, vb_ref, sb_ref, ob_ref, True))
    rng = 0.0
    for d, (q_ref, f_ref, v_ref, st_ref, o_ref, rev) in enumerate(dirs):
        q, kk, logf = _hgrn_gates(q_ref[...], f_ref[...], gl_ref[d], lb_zero)
        cum = _dot01(bblk_ref[d], logf)
        mid_row, last_row = (CHUNK // 2, 0) if rev else (CHUNK // 2 - 1, CHUNK - 1)
        cum_mid = [cum[c * CHUNK + mid_row:c * CHUNK + mid_row + 1] for c in range(NCHUNK)]
        cum_last = [cum[c * CHUNK + last_row:c * CHUNK + last_row + 1] for c in range(NCHUNK)]
        a = jnp.concatenate([cum[c * CHUNK:(c + 1) * CHUNK] - cum_mid[c] for c in range(NCHUNK)], axis=0)
        rng = jnp.maximum(rng, jnp.max(jnp.abs(a)))
        _hgrn_block_fast(q, kk, a, cum_mid, cum_last, v_ref[...], st_ref, o_ref, rev)

    @pl.when(jnp.logical_not(rng <= HGRN_FAST_RANGE))
    def _():
        sf_ref[...] = sf0_ref[...]
        sb_ref[...] = sb0_ref[...]

        def body(c, carry):
            rf = pl.ds(pl.multiple_of(c * CHUNK, CHUNK), CHUNK)
            rb = pl.ds(pl.multiple_of((NCHUNK - 1 - c) * CHUNK, CHUNK), CHUNK)
            of_ref[rf, :] = _hgrn_chunk_exact(qf_ref[rf, :], ff_ref[rf, :], vf_ref[rf, :], sf_ref,
                                              gl_ref[0], ball_ref[0], lmask_ref, 0, False, lb_zero)
            ob_ref[rb, :] = _hgrn_chunk_exact(qb_ref[rb, :], fb_ref[rb, :], vb_ref[rb, :], sb_ref,
                                              gl_ref[1], ball_ref[1], lmask_ref, 1, True, lb_zero)
            return carry

        lax.fori_loop(0, NCHUNK, body, 0)


def _hgrn_call(u_f32, gl, ball, lmask, bblk, lb_zero):
    b = u_f32.shape[0]

    def fwd(col):
        return pl.BlockSpec((None, TB, 256), lambda b, i: (b, jnp.where(i == 0, CTX_BLK, i - 1), col))

    def bwd(col):
        return pl.BlockSpec((None, TB, 256), lambda b, i: (b, jnp.where(i == 0, CTX_BLK, CTX_BLK - i), col))

    return pl.pallas_call(
        functools.partial(_hgrn_kernel, lb_zero=lb_zero),
        grid=(b, NBLK),
        in_specs=[fwd(1), fwd(2), fwd(4), bwd(1), bwd(3), bwd(4),
                  _const_spec((2, 2, 256)), _const_spec((2, 8 * CHUNK, CHUNK)),
                  _const_spec((2, 7, 4 * CHUNK, CHUNK)), _const_spec((2, TB, TB))],
        out_specs=[fwd(0), bwd(0)],
        out_shape=[jax.ShapeDtypeStruct((b, T_ALL, 256), F32)] * 2,
        scratch_shapes=[pltpu.VMEM((256, 256), F32)] * 4,
        compiler_params=_cparams(("arbitrary", "arbitrary")),
        name="hgrn_scan",
    )(u_f32, u_f32, u_f32, u_f32, u_f32, u_f32, gl, ball, lmask, bblk)


def _outproj_kernel(*refs, nx, with_ctx):
    x = _x_tile(refs[:nx])
    (oa_ref, p_ref, q_ref, oc_ref, of_ref, ob_ref, gd_ref, hg_ref, hh_ref, cs_ref,
     wf_ref, wo_ref, gp_ref, gtc_ref, gtb_ref, o_ref) = refs[nx:]
    z = _dot(p_ref[...].astype(BF16), cs_ref[0:256, :]) + _dot(q_ref[...].astype(BF16), cs_ref[256:512, :])
    o_b = _dot(z.astype(BF16), wf_ref[...])
    od = of_ref[...] + ob_ref[...]
    sq = od * od
    hi = sq.astype(BF16)
    lo = (sq - hi.astype(F32)).astype(BF16)
    ms = (_dot(hi, hh_ref[...]) + _dot(lo, hh_ref[...])) * (1.0 / HEAD_DIM)
    gd = gd_ref[...]
    o_d = od * lax.rsqrt(ms + EPS) * hg_ref[...] * (gd * jax.nn.sigmoid(gd))
    y = (_dot(oa_ref[...], wo_ref[0:256, :]) + _dot(o_b.astype(BF16), wo_ref[256:512, :])
         + _dot(oc_ref[...], wo_ref[512:768, :]) + _dot(o_d.astype(BF16), wo_ref[768:1024, :]))
    gt = _mod(gtc_ref, gtb_ref, o_ref.shape[0], with_ctx)
    o_ref[...] = x + gt * (_rms(y) * gp_ref[...])


def _outproj_call(x_parts, o_a, p, q, o_c, o_f, o_bk, u_f32, hg, hh, cs, w_fnet, w_out, g_post, mods, with_ctx):
    b = x_parts[0].shape[0]
    tm, t = (TM_ALL, T_ALL) if with_ctx else (TM_LAT, SEQ)
    col = lambda c: pl.BlockSpec((None, tm, 256), lambda b, i: (b, i, c))
    split = len(x_parts) == 2
    x_args = (x_parts[0],) * 3 + (x_parts[1],) if split else x_parts
    x_specs = _x_specs(True) if split else [pl.BlockSpec((None, tm, D_MODEL), lambda b, i: (b, i, 0))]
    return pl.pallas_call(
        functools.partial(_outproj_kernel, nx=len(x_args), with_ctx=with_ctx),
        grid=(b, t // tm),
        in_specs=[
            *x_specs,
            col(0), col(0), col(0), col(0), col(0), col(0), col(5),
            _const_spec((1, 256)), _const_spec((256, 256)), _const_spec((512, 256)),
            _const_spec((256, 256)), _const_spec((D_MODEL, D_MODEL)), _const_spec((1, D_MODEL)),
            *_mod_specs(2),
        ],
        out_specs=pl.BlockSpec((None, tm, D_MODEL), lambda b, i: (b, i, 0)),
        out_shape=jax.ShapeDtypeStruct((b, t, D_MODEL), F32),
        compiler_params=_cparams(("arbitrary", "arbitrary")),
        name="out_proj",
    )(*x_args, o_a, p, q, o_c, o_f, o_bk, u_f32, hg, hh, cs, w_fnet, w_out, g_post, mods, mods)


def _ffn_kernel(x_ref, g_ref, shc_ref, shb_ref, scc_ref, scb_ref, w1_ref, w2_ref, gp_ref, gtc_ref, gtb_ref,
                o_ref, *, with_ctx):
    tm = x_ref.shape[0]
    x = x_ref[...]
    sh, sc = _mod(shc_ref, shb_ref, tm, with_ctx), _mod(scc_ref, scb_ref, tm, with_ctx)
    h = ((_rms(x) * g_ref[...]) * (1.0 + sc) + sh).astype(BF16)
    acc = jnp.zeros((tm, D_MODEL), F32)
    fc = 256
    for c in range(D_FF // fc):
        a = _dot(h, w1_ref[:, c * fc:(c + 1) * fc])
        g = _dot(h, w1_ref[:, D_FF + c * fc:D_FF + (c + 1) * fc])
        act = (a * jax.nn.sigmoid(a) * g).astype(BF16)
        acc = acc + _dot(act, w2_ref[c * fc:(c + 1) * fc, :])
    gt = _mod(gtc_ref, gtb_ref, tm, with_ctx)
    o_ref[...] = x + gt * (_rms(acc) * gp_ref[...])


def _ffn_call(xs, g_pre, w1, w2, g_post, mods, with_ctx):
    b, t, _ = xs.shape
    tm = TM_ALL if with_ctx else TM_LAT
    return pl.pallas_call(
        functools.partial(_ffn_kernel, with_ctx=with_ctx),
        grid=(b, t // tm),
        in_specs=[
            pl.BlockSpec((None, tm, D_MODEL), lambda b, i: (b, i, 0)),
            _const_spec((1, D_MODEL)),
            *_mod_specs(3), *_mod_specs(4),
            _const_spec((D_MODEL, 2 * D_FF)), _const_spec((D_FF, D_MODEL)), _const_spec((1, D_MODEL)),
            *_mod_specs(5),
        ],
        out_specs=pl.BlockSpec((None, tm, D_MODEL), lambda b, i: (b, i, 0)),
        out_shape=jax.ShapeDtypeStruct(xs.shape, F32),
        compiler_params=_cparams(("arbitrary", "arbitrary")),
        name="ffn",
    )(xs, g_pre, mods, mods, mods, mods, w1, w2, g_post, mods, mods)


def kernel(x, c, ctx, c_ctx, w_ada, b_ada, g_pre_mix, g_post_mix, g_pre_ffn, g_post_ffn, w_in, w_out,
           na_rpb, w_fnet, swa_sink, hgrn_lb_logits, hgrn_norm_g, w_ffn_in, w_ffn_out):
    tb = _tables()
    bf = lambda name: jnp.asarray(tb[name]).astype(BF16)
    cos, sin = jnp.asarray(tb["rope_cos"]), jnp.asarray(tb["rope_sin"])
    f64, g2, fc, chan_cs, hh = bf("f64"), bf("g_stage2"), bf("f_ctx"), bf("chan_cs"), bf("head_ones")
    ball, bblk, lmask = bf("hgrn_b"), bf("hgrn_bblk"), jnp.asarray(tb["hgrn_mask"])
    na_mask, swa_mask = jnp.asarray(tb["na_mask"]), jnp.asarray(tb["swa_mask"])

    cc = jnp.concatenate([c, c_ctx[None, :]], axis=0)
    mods = _ada_call(jnp.broadcast_to(cc[:, :, None], (3, D_MODEL, 128)), w_ada, b_ada)

    lb = jnp.cumsum(jax.nn.softmax(hgrn_lb_logits.astype(F32), axis=1), axis=1)
    lb = lb - lb[:, :1]
    gl_all = jnp.stack([jnp.log(lb), jnp.log1p(-lb)], axis=2)

    xs = None
    for l in range(DEPTH):
        with_ctx = l < DEPTH - 1
        row = lambda a: a[l][None, :]
        x_all = (x, ctx) if l == 0 else (xs,)
        x_res = x_all if with_ctx else x_all[:1]
        u_att, u_f32 = _proj_call(x_all, row(g_pre_mix), mods[l], w_in[l].astype(BF16), cos, sin)

        rpb = jnp.pad(na_rpb[l], ((0, 0), (NA_DR_PAD, NA_DR_ROWS - NA_DR_PAD - (2 * NA_WIN_R - 1)), (0, 0)))
        t2 = _na_bias_call(rpb[:, :, tb["na_lane_l"]], rpb[:, :, tb["na_lane_r"]])
        o_a = _na_call(u_att, na_mask, t2, with_ctx)
        o_c = _swa_call(u_att, swa_sink[l], swa_mask, with_ctx)
        p, q = _fnet_call(u_f32, f64, g2, fc)
        o_f, o_bk = _hgrn_call(u_f32, gl_all[:, l], ball, lmask, bblk, lb_zero=(l == 0))

        xs = _outproj_call(x_res, o_a, p, q, o_c, o_f, o_bk, u_f32, row(hgrn_norm_g), hh, chan_cs,
                           w_fnet[l].astype(BF16), w_out[l].astype(BF16), row(g_post_mix), mods[l], with_ctx)
        xs = _ffn_call(xs, row(g_pre_ffn), w_ffn_in[l].astype(BF16), w_ffn_out[l].astype(BF16),
                       row(g_post_ffn), mods[l], with_ctx)
    return xs
```

```python
import functools

import numpy as np
import jax
import jax.numpy as jnp
from jax import lax
from jax.experimental import pallas as pl
from jax.experimental.pallas import tpu as pltpu

F32, BF16 = jnp.float32, jnp.bfloat16

D_MODEL = 1024
SEQ = 8192
DEPTH = 2
GRID_W = 64
GRID_ROWS = SEQ // GRID_W
CTX = 256
HEAD_DIM = 64
NA_WIN_R, NA_WIN_C = 8, 16
SWA_WINDOW = 128
ROPE_THETA = 10000.0
EPS = 1e-6
D_FF = 2816
D_IN = 2816

TB = 256
T_ALL = SEQ + CTX
NBLK = T_ALL // TB
CTX_BLK = SEQ // TB
TM_ALL = 768
TM_LAT = 1024
NA_KROWS = 12
NA_KEYS = NA_KROWS * GRID_W
NA_DR_PAD = 4
NA_DR_ROWS = 24
SWA_KEYS = TB + 2 * SWA_WINDOW
CHUNK = 64
NCHUNK = TB // CHUNK
HGRN_FAST_RANGE = 60.0
Q_SCALE = HEAD_DIM ** -0.5
VMEM_LIMIT = 56 * 1024 * 1024

ATT_W, F32_W = 1280, 1536


def _dot(a, b):
    return jnp.dot(a, b, preferred_element_type=F32)


def _dot_nt(a, b):
    return lax.dot_general(a, b, (((1,), (1,)), ((), ())), preferred_element_type=F32)


def _dot_tn(a, b):
    return lax.dot_general(a, b, (((0,), (0,)), ((), ())), preferred_element_type=F32)


def _cparams(sem):
    return pltpu.CompilerParams(dimension_semantics=sem, vmem_limit_bytes=VMEM_LIMIT)


def _const_spec(shape):
    nd = len(shape)
    return pl.BlockSpec(shape, lambda *_: (0,) * nd, pipeline_mode=pl.Buffered(1))


@functools.lru_cache(maxsize=None)
def _tables():
    t = {}
    lane = np.arange(128)
    d = lane % 64
    freq = ROPE_THETA ** (-(d % 16) / 16.0)
    pos = np.arange(SEQ)
    p = np.where(d[None, :] < 32, (pos // GRID_W)[:, None], (pos % GRID_W)[:, None]).astype(np.float64)
    ang = p * freq[None, :]
    sign = np.where((d % 32) < 16, -1.0, 1.0)
    cos = np.concatenate([np.cos(ang), np.ones((CTX, 128))], axis=0)
    sin = np.concatenate([np.sin(ang) * sign[None, :], np.zeros((CTX, 128))], axis=0)
    t["rope_cos"], t["rope_sin"] = cos.astype(np.float32), sin.astype(np.float32)

    masks = []
    a, cq = np.arange(4), np.arange(GRID_W)
    rho, ck = np.arange(NA_KROWS), np.arange(GRID_W)
    for r0 in (0, 8, GRID_ROWS - 4):
        kr0 = int(np.clip(r0 - 4, 0, GRID_ROWS - NA_KROWS))
        rq, rk = r0 + a, kr0 + rho
        start = np.clip(rq - NA_WIN_R // 2, 0, GRID_ROWS - NA_WIN_R)
        vr = (rk[None, :] >= start[:, None]) & (rk[None, :] < start[:, None] + NA_WIN_R)
        cs = np.clip(cq - NA_WIN_C // 2, 0, GRID_W - NA_WIN_C)
        vc = (ck[None, :] >= cs[:, None]) & (ck[None, :] < cs[:, None] + NA_WIN_C)
        val = vr[:, None, :, None] & vc[None, :, None, :]
        masks.append(np.where(val, 0.0, -np.inf).reshape(TB, NA_KEYS))
    masks.append(np.full((TB, NA_KEYS), -np.inf))
    t["na_mask"] = np.stack(masks).astype(np.float32)
    n = np.arange(128)
    clipc = lambda m: np.clip(m, -(NA_WIN_C - 1), NA_WIN_C - 1) + NA_WIN_C - 1
    t["na_lane_l"] = clipc(np.where(n < 64, n, n - 128)).astype(np.int32)
    t["na_lane_r"] = clipc(n - 64).astype(np.int32)

    rel = np.arange(SWA_KEYS)[None, :] - np.arange(TB)[:, None]
    swa = [np.where(np.abs(rel + off) <= SWA_WINDOW, 0.0, -np.inf) for off in (0, -SWA_WINDOW, -2 * SWA_WINDOW)]
    swa.append(np.full((TB, SWA_KEYS), -np.inf))
    t["swa_mask"] = np.stack(swa).astype(np.float32)

    def cs_tab(num, den):
        ang = 2.0 * np.pi * (num % den) / den
        return np.cos(ang), np.sin(ang)

    k1, t1 = np.arange(64)[:, None], np.arange(64)[None, :]
    c64, s64 = cs_tab(k1 * t1, 64)
    t["f64"] = np.concatenate([c64, -s64], axis=0).astype(np.float32)
    k2, t2 = np.arange(128)[:, None], np.arange(128)[None, :]
    g = []
    for kk in range(64):
        gc, gs = cs_tab((64 * k2 + kk) * t2, SEQ)
        g.append(np.block([[gc, gs], [-gs, gc]]))
    t["g_stage2"] = np.stack(g).astype(np.float32)
    kc, tc = np.arange(CTX)[:, None], np.arange(CTX)[None, :]
    cc, sc = cs_tab(kc * tc, CTX)
    t["f_ctx"] = np.concatenate([cc, -sc], axis=0).astype(np.float32)
    ch = np.arange(256)
    same = (ch[:, None] // 64) == (ch[None, :] // 64)
    c_ch, s_ch = cs_tab((ch[:, None] % 64) * (ch[None, :] % 64), 64)
    t["chan_cs"] = np.concatenate([np.where(same, c_ch, 0.0), np.where(same, s_ch, 0.0)], axis=0).astype(np.float32)
    t["head_ones"] = same.astype(np.float32)

    balls, lmasks, bblk = [], [], []
    for rev in (False, True):
        tau = (CHUNK - 1 - np.arange(CHUNK)) if rev else np.arange(CHUNK)
        tt, tj = tau[:, None], tau[None, :]
        blocks = [(tj <= tt), (tj > tt)]
        masks = [tt == tj]
        for h in (32, 16, 8, 4, 2, 1):
            up = ((tt // h) % 2) == 1
            mid = (tt // (2 * h)) * (2 * h) + h - 1
            blocks.append(np.where(up, (tj > mid) & (tj <= tt), (tj > tt) & (tj <= mid)))
            up_s = ((tj // h) % 2) == 1
            masks.append((tt // (2 * h) == tj // (2 * h)) & up & ~up_s)
        balls.append(np.concatenate(blocks, axis=0))
        lmasks.append(np.stack([np.tile(m, (4, 1)) for m in masks]))
        bblk.append(np.kron(np.eye(NCHUNK), blocks[0]))
    t["hgrn_b"] = np.stack(balls).astype(np.float32)
    t["hgrn_mask"] = np.stack(lmasks).astype(np.float32)
    t["hgrn_bblk"] = np.stack(bblk).astype(np.float32)
    return t


def _ada_kernel(cb_ref, w_ref, b_ref, o_ref, s_ref):
    tn = w_ref.shape[1]
    nrep = tn // 128

    @pl.when((pl.program_id(0) == 0) & (pl.program_id(1) == 0))
    def _():
        cb = cb_ref[...]
        s_ref[...] = cb * jax.nn.sigmoid(cb)

    def body(kc, accs):
        k0 = pl.multiple_of(kc * 8, 8)
        wt = w_ref[pl.ds(k0, 8), :]
        return tuple(accs[r] + wt * jnp.concatenate([s_ref[r, pl.ds(k0, 8), :]] * nrep, axis=1) for r in range(3))

    accs = lax.fori_loop(0, D_MODEL // 8, body, tuple(jnp.zeros((8, tn), F32) for _ in range(3)), unroll=8)
    for r in range(3):
        o_ref[r] = jnp.sum(accs[r], axis=0, keepdims=True) + b_ref[...]


def _ada_call(cb, w_ada, b_ada):
    tn = 1536
    return pl.pallas_call(
        _ada_kernel,
        grid=(DEPTH, 6 * D_MODEL // tn),
        in_specs=[
            pl.BlockSpec((3, D_MODEL, 128), lambda l, j: (0, 0, 0)),
            pl.BlockSpec((None, D_MODEL, tn), lambda l, j: (l, 0, j)),
            pl.BlockSpec((None, 1, tn), lambda l, j: (l, 0, j)),
        ],
        out_specs=pl.BlockSpec((None, 3, 1, tn), lambda l, j: (l, 0, 0, j)),
        out_shape=jax.ShapeDtypeStruct((DEPTH, 3, 1, 6 * D_MODEL), F32),
        scratch_shapes=[pltpu.VMEM((3, D_MODEL, 128), F32)],
        compiler_params=_cparams(("arbitrary", "arbitrary")),
        name="adaln",
    )(cb, w_ada, b_ada.reshape(DEPTH, 1, 6 * D_MODEL))


def _mod_specs(col):
    return [pl.BlockSpec((None, 1, D_MODEL), lambda b, i: (2, 0, col)),
            pl.BlockSpec((None, 1, D_MODEL), lambda b, i: (b, 0, col))]


def _mod(mc_ref, mb_ref, tm, with_ctx):
    if not with_ctx:
        return mb_ref[...]
    rows = pl.program_id(1) * tm + lax.broadcasted_iota(jnp.int32, (tm, 1), 0)
    return jnp.where(rows >= SEQ, mc_ref[...], mb_ref[...])


def _rms(x):
    return x * lax.rsqrt(jnp.mean(x * x, axis=-1, keepdims=True) + EPS)


def _x_specs(split):
    if not split:
        return [pl.BlockSpec((None, TM_ALL, D_MODEL), lambda b, i: (b, i, 0))]
    nx = SEQ // TB
    return [pl.BlockSpec((None, TB, D_MODEL), lambda b, i, j=j: (b, jnp.minimum(3 * i + j, nx - 1), 0))
            for j in range(TM_ALL // TB)] + [pl.BlockSpec((None, CTX, D_MODEL), lambda b, i: (b, 0, 0))]


def _x_tile(x_refs):
    if len(x_refs) == 1:
        return x_refs[0][...]
    xa, xb, xc, cx = x_refs
    last = pl.program_id(1) == pl.num_programs(1) - 1
    return jnp.concatenate([xa[...], xb[...], jnp.where(last, cx[...], xc[...])], axis=0)


def _proj_kernel(*refs, nx):
    x = _x_tile(refs[:nx])
    g_ref, shc_ref, shb_ref, scc_ref, scb_ref, w_ref, cos_ref, sin_ref, att_ref, f32_ref = refs[nx:]
    tm = TM_ALL
    sh, sc = _mod(shc_ref, shb_ref, tm, True), _mod(scc_ref, scb_ref, tm, True)
    h = ((_rms(x) * g_ref[...]) * (1.0 + sc) + sh).astype(BF16)
    cos, sin = cos_ref[...], sin_ref[...]
    first = (lax.broadcasted_iota(jnp.int32, (tm, 128), 1) % 32) < 16

    def rope(v):
        return v * cos + jnp.where(first, pltpu.roll(v, 112, 1), pltpu.roll(v, 16, 1)) * sin

    a = _dot(h, w_ref[:, 0:768])
    att_ref[:, 0:256] = (a[:, 0:256] * Q_SCALE).astype(BF16)
    att_ref[:, 256:768] = a[:, 256:768].astype(BF16)
    f32_ref[:, 0:256] = _dot(h, w_ref[:, 768:1024])
    s = _dot(h, w_ref[:, 1024:1536])
    att_ref[:, 768:896] = (rope(s[:, 0:128]) * Q_SCALE).astype(BF16)
    att_ref[:, 896:1024] = (rope(s[:, 128:256]) * Q_SCALE).astype(BF16)
    att_ref[:, 1024:1152] = rope(s[:, 256:384]).astype(BF16)
    att_ref[:, 1152:1280] = s[:, 384:512].astype(BF16)
    f32_ref[:, 256:1536] = _dot(h, w_ref[:, 1536:2816])


def _proj_call(x_parts, g, mods, w_in, cos, sin):
    b = x_parts[0].shape[0]
    tm = TM_ALL
    split = len(x_parts) == 2
    x_args = (x_parts[0],) * 3 + (x_parts[1],) if split else x_parts
    return pl.pallas_call(
        functools.partial(_proj_kernel, nx=len(x_args)),
        grid=(b, T_ALL // tm),
        in_specs=[
            *_x_specs(split),
            _const_spec((1, D_MODEL)),
            *_mod_specs(0), *_mod_specs(1),
            _const_spec((D_MODEL, D_IN)),
            pl.BlockSpec((tm, 128), lambda b, i: (i, 0)),
            pl.BlockSpec((tm, 128), lambda b, i: (i, 0)),
        ],
        out_specs=[
            pl.BlockSpec((None, tm, ATT_W), lambda b, i: (b, i, 0)),
            pl.BlockSpec((None, tm, F32_W), lambda b, i: (b, i, 0)),
        ],
        out_shape=[jax.ShapeDtypeStruct((b, T_ALL, ATT_W), BF16), jax.ShapeDtypeStruct((b, T_ALL, F32_W), F32)],
        compiler_params=_cparams(("arbitrary", "arbitrary")),
        name="proj_in",
    )(*x_args, g, mods, mods, mods, mods, w_in, cos, sin)


def _na_bias_kernel(vl_ref, vr_ref, o_ref):
    lane = lax.broadcasted_iota(jnp.int32, (GRID_W, 128), 1)
    for e in range(NA_DR_ROWS - 1):
        left = pltpu.roll(jnp.broadcast_to(vl_ref[e:e + 1, :], (GRID_W, 128)), 0, 1, stride=1, stride_axis=0)
        right = pltpu.roll(jnp.broadcast_to(vr_ref[e + 1:e + 2, :], (GRID_W, 128)), 0, 1, stride=1, stride_axis=0)
        o_ref[e] = jnp.where(lane < 64, left, right)


def _na_bias_call(vl, vr):
    spec = pl.BlockSpec((None, NA_DR_ROWS, 128), lambda h: (h, 0, 0))
    return pl.pallas_call(
        _na_bias_kernel,
        grid=(4,),
        in_specs=[spec, spec],
        out_specs=pl.BlockSpec((None, NA_DR_ROWS - 1, GRID_W, 128), lambda h: (h, 0, 0, 0)),
        out_shape=jax.ShapeDtypeStruct((4, NA_DR_ROWS - 1, GRID_W, 128), F32),
        compiler_params=_cparams(("arbitrary",)),
        name="na_bias",
    )(vl, vr)


AT_ROWS = 16


def _attn_block_variant(b, i):
    return (jnp.where(i == CTX_BLK, 3, jnp.where(i == 0, 0, jnp.where(i == CTX_BLK - 1, 2, 1))), 0, 0)


def _lane_half(shape, j):
    lane = lax.broadcasted_iota(jnp.int32, shape, 1)
    return (lane >= 64 * j) & (lane < 64 * (j + 1))


def _na_kernel(q_ref, k_ref, v_ref, mask_ref, t2_ref, o_ref, s_ref, p_ref):
    i = pl.program_id(1)
    is_ctx = i == CTX_BLK
    r0 = i * 4
    kr0 = jnp.where(is_ctx, 0, jnp.clip(r0 - 4, 0, GRID_ROWS - NA_KROWS))
    kstart = pl.multiple_of(kr0 * GRID_W, GRID_W)
    delta = jnp.where(is_ctx, 3, kr0 - r0 + NA_WIN_R - 1) + NA_DR_PAD
    lane = lax.broadcasted_iota(jnp.int32, (TB, 128), 1)
    for hp in range(2):
        cs = slice(hp * 128, (hp + 1) * 128)
        q2 = q_ref[:, cs]
        kw, kc = k_ref[pl.ds(kstart, NA_KEYS), cs], k_ref[SEQ:T_ALL, cs]
        qm = jnp.concatenate([jnp.where(_lane_half((TB, 128), j), q2, jnp.zeros_like(q2)) for j in range(2)], axis=0)
        s_ref[hp, :, 0:CTX] = _dot_nt(qm, kc)
        s_ref[hp, :, CTX:] = _dot_nt(qm, kw)
    for hp in range(2):
        for r in range(2 * TB // AT_ROWS):
            rows = slice(r * AT_ROWS, (r + 1) * AT_ROWS)
            j, rq = divmod(r, TB // AT_ROWS)
            a, sub = divmod(rq, GRID_W // AT_ROWS)
            trow = slice(sub * AT_ROWS, (sub + 1) * AT_ROWS)
            bias = jnp.concatenate(
                [t2_ref[2 * hp + j, delta + 2 * rp - a, trow, :] for rp in range(NA_KROWS // 2)], axis=1)
            s_nb = s_ref[hp, rows, CTX:] + bias + mask_ref[rq * AT_ROWS:(rq + 1) * AT_ROWS, :]
            s_c = s_ref[hp, rows, 0:CTX]
            m = jnp.maximum(jnp.max(s_nb, axis=-1, keepdims=True), jnp.max(s_c, axis=-1, keepdims=True))
            p_ref[hp, rows, 0:CTX] = jnp.exp((s_c - m).astype(BF16))
            p_ref[hp, rows, CTX:] = jnp.exp((s_nb - m).astype(BF16))
    for hp in range(2):
        cs = slice(hp * 128, (hp + 1) * 128)
        vw, vc = v_ref[pl.ds(kstart, NA_KEYS), cs], v_ref[SEQ:T_ALL, cs]
        o2 = (_dot(p_ref[hp, :, 0:CTX], jnp.concatenate([vc, jnp.ones_like(vc)], axis=1))
              + _dot(p_ref[hp, :, CTX:], jnp.concatenate([vw, jnp.ones_like(vw)], axis=1)))
        top = o2[0:TB, 0:128] * (1.0 / o2[0:TB, 128:256])
        bot = o2[TB:2 * TB, 0:128] * (1.0 / o2[TB:2 * TB, 128:256])
        o_ref[:, cs] = jnp.where(lane < 64, top, bot).astype(BF16)


def _na_call(u_att, mask, t2, with_ctx):
    b = u_att.shape[0]
    nb = NBLK if with_ctx else CTX_BLK
    return pl.pallas_call(
        _na_kernel,
        grid=(b, nb),
        in_specs=[
            pl.BlockSpec((None, TB, 256), lambda b, i: (b, i, 0)),
            pl.BlockSpec((None, T_ALL, 256), lambda b, i: (b, 0, 1)),
            pl.BlockSpec((None, T_ALL, 256), lambda b, i: (b, 0, 2)),
            pl.BlockSpec((None, TB, NA_KEYS), _attn_block_variant),
            _const_spec((4, NA_DR_ROWS - 1, GRID_W, 128)),
        ],
        out_specs=pl.BlockSpec((None, TB, 256), lambda b, i: (b, i, 0)),
        out_shape=jax.ShapeDtypeStruct((b, nb * TB, 256), BF16),
        scratch_shapes=[pltpu.VMEM((2, 2 * TB, CTX + NA_KEYS), F32), pltpu.VMEM((2, 2 * TB, CTX + NA_KEYS), BF16)],
        compiler_params=_cparams(("arbitrary", "arbitrary")),
        name="na_attn",
    )(u_att, u_att, u_att, mask, t2)


def _swa_kernel(sink_ref, q_ref, k_ref, v_ref, mask_ref, o_ref, s_ref, p_ref, esink_ref):
    i = pl.program_id(1)
    ks0 = jnp.clip(i * TB - SWA_WINDOW, 0, SEQ - SWA_KEYS)
    kstart = pl.multiple_of(ks0, SWA_WINDOW)
    kw, vw = k_ref[pl.ds(kstart, SWA_KEYS), :], v_ref[pl.ds(kstart, SWA_KEYS), :]
    kc, vc = k_ref[SEQ:T_ALL, :], v_ref[SEQ:T_ALL, :]
    lane = lax.broadcasted_iota(jnp.int32, (TB, 128), 1)
    vc2 = jnp.concatenate([vc, jnp.ones_like(vc)], axis=1)
    vw2 = jnp.concatenate([vw, jnp.ones_like(vw)], axis=1)
    for g in range(2):
        q2 = q_ref[:, g * 128:(g + 1) * 128].astype(F32)
        qa = jnp.concatenate([jnp.where(_lane_half((TB, 128), g), q2 if j == g else pltpu.roll(q2, 64, 1), 0.0)
                              for j in range(2)], axis=0).astype(BF16)
        s_ref[g, :, 0:CTX] = _dot_nt(qa, kc)
        s_ref[g, :, CTX:] = _dot_nt(qa, kw)
    for g in range(2):
        for r in range(2 * TB // AT_ROWS):
            rows = slice(r * AT_ROWS, (r + 1) * AT_ROWS)
            j, rq = divmod(r, TB // AT_ROWS)
            sink = sink_ref[2 * g + j]
            s_l = s_ref[g, rows, CTX:] + mask_ref[rq * AT_ROWS:(rq + 1) * AT_ROWS, :]
            s_c = s_ref[g, rows, 0:CTX]
            m = jnp.maximum(jnp.maximum(jnp.max(s_l, axis=-1, keepdims=True),
                                        jnp.max(s_c, axis=-1, keepdims=True)), sink)
            p_ref[g, rows, 0:CTX] = jnp.exp((s_c - m).astype(BF16))
            p_ref[g, rows, CTX:] = jnp.exp((s_l - m).astype(BF16))
            esink_ref[g, rows, :] = jnp.exp(jnp.broadcast_to(sink - m, (AT_ROWS, 128)))
    for g in range(2):
        o2 = _dot(p_ref[g, :, 0:CTX], vc2) + _dot(p_ref[g, :, CTX:], vw2)
        o = o2[:, 0:128] * (1.0 / (o2[:, 128:256] + esink_ref[g]))
        outs = [o[j * TB:(j + 1) * TB] if j == g else pltpu.roll(o[j * TB:(j + 1) * TB], 64, 1) for j in range(2)]
        o_ref[:, g * 128:(g + 1) * 128] = jnp.where(lane < 64, outs[0], outs[1]).astype(BF16)


def _swa_call(u_att, sink, mask, with_ctx):
    b = u_att.shape[0]
    nb = NBLK if with_ctx else CTX_BLK
    return pl.pallas_call(
        _swa_kernel,
        grid=(b, nb),
        in_specs=[
            pl.BlockSpec(memory_space=pltpu.SMEM),
            pl.BlockSpec((None, TB, 256), lambda b, i: (b, i, 3)),
            pl.BlockSpec((None, T_ALL, 128), lambda b, i: (b, 0, 8)),
            pl.BlockSpec((None, T_ALL, 128), lambda b, i: (b, 0, 9)),
            pl.BlockSpec((None, TB, SWA_KEYS), _attn_block_variant),
        ],
        out_specs=pl.BlockSpec((None, TB, 256), lambda b, i: (b, i, 0)),
        out_shape=jax.ShapeDtypeStruct((b, nb * TB, 256), BF16),
        scratch_shapes=[pltpu.VMEM((2, 2 * TB, CTX + SWA_KEYS), F32), pltpu.VMEM((2, 2 * TB, CTX + SWA_KEYS), BF16),
                        pltpu.VMEM((2, 2 * TB, 128), F32)],
        compiler_params=_cparams(("arbitrary", "arbitrary")),
        name="swa_attn",
    )(sink, u_att, u_att, u_att, mask)


FN_XP = 136
FN_YP = 136
FN_ZP = 72


def _fnet_kernel(x_ref, f64_ref, g_ref, fc_ref, p_ref, q_ref, xz_ref, zq_ref, y_ref):
    sc_lat, sc_ctx = (64.0 * SEQ) ** -0.5, (64.0 * CTX) ** -0.5
    pq = _dot(fc_ref[...], x_ref[SEQ:T_ALL, :].astype(BF16))
    p_ref[SEQ:T_ALL, :] = (pq[0:CTX] * sc_ctx).astype(BF16)
    q_ref[SEQ:T_ALL, :] = (pq[CTX:2 * CTX] * sc_ctx).astype(BF16)
    f64 = f64_ref[...]

    def repitch(t1, carry):
        xz_ref[pl.ds(pl.multiple_of(t1 * FN_XP, 8), 128), :] = x_ref[pl.ds(pl.multiple_of(t1 * 128, 128), 128), :]
        return carry

    lax.fori_loop(0, 64, repitch, 0, unroll=4)

    def stage1(t2, carry):
        xs = xz_ref[pl.ds(t2, 64, stride=FN_XP), :].astype(BF16)
        y_ref[pl.ds(t2, 128, stride=FN_YP), :] = _dot(f64, xs)
        return carry

    lax.fori_loop(0, 128, stage1, 0, unroll=8)

    def stage2(k1, carry):
        yre = y_ref[pl.ds(pl.multiple_of(k1 * FN_YP, 8), 128), :]
        yim = y_ref[pl.ds(pl.multiple_of((64 + k1) * FN_YP, 8), 128), :]
        ycat = jnp.concatenate([yre, yim], axis=0).astype(BF16)
        pq = _dot(g_ref[k1], ycat)
        xz_ref[pl.ds(k1, 128, stride=FN_ZP), :] = pq[0:128] * sc_lat
        zq_ref[pl.ds(k1, 128, stride=FN_ZP), :] = pq[128:256] * sc_lat
        return carry

    lax.fori_loop(0, 64, stage2, 0, unroll=4)

    def emit(k2, carry):
        src = pl.ds(pl.multiple_of(k2 * FN_ZP, 8), 64)
        dst = pl.ds(pl.multiple_of(k2 * 64, 64), 64)
        p_ref[dst, :] = xz_ref[src, :].astype(BF16)
        q_ref[dst, :] = zq_ref[src, :].astype(BF16)
        return carry

    lax.fori_loop(0, 128, emit, 0, unroll=4)


def _fnet_call(u_f32, f64, g2, fc):
    b = u_f32.shape[0]
    spec = pl.BlockSpec((None, T_ALL, 128), lambda b, j: (b, 0, j))
    return pl.pallas_call(
        _fnet_kernel,
        grid=(b, 2),
        in_specs=[spec, _const_spec((128, 64)), _const_spec((64, 256, 256)), _const_spec((2 * CTX, CTX))],
        out_specs=[spec, spec],
        out_shape=[jax.ShapeDtypeStruct((b, T_ALL, 256), BF16)] * 2,
        scratch_shapes=[pltpu.VMEM((128 * FN_ZP, 128), F32), pltpu.VMEM((128 * FN_ZP, 128), F32),
                        pltpu.VMEM((128 * FN_YP, 128), F32)],
        compiler_params=_cparams(("arbitrary", "arbitrary")),
        name="fnet_dft",
    )(u_f32, f64, g2, fc)


def _hgrn_gates(qd, fz, gl, lb_zero):
    q = qd * jax.nn.sigmoid(qd)
    sp = jnp.log(1.0 + jnp.exp(-jnp.abs(fz)))
    ls = jnp.minimum(fz, 0.0) - sp
    lk = jnp.minimum(-fz, 0.0) - sp
    if lb_zero:
        return q, jnp.exp(lk), ls
    b_ = gl[1:2] + ls
    a_ = gl[0:1]
    logf = jnp.maximum(a_, b_) + jnp.log(1.0 + jnp.exp(-jnp.abs(a_ - b_)))
    return q, jnp.exp(gl[1:2] + lk), logf


def _dot01(b01, x):
    h1 = x.astype(BF16)
    r1 = x - h1.astype(F32)
    h2 = r1.astype(BF16)
    h3 = (r1 - h2.astype(F32)).astype(BF16)
    return _dot(b01, h1) + _dot(b01, h2) + _dot(b01, h3)


def _head_masks(rows):
    lane = lax.broadcasted_iota(jnp.int32, (rows, 256), 1)
    return [(lane >= 64 * h) & (lane < 64 * (h + 1)) for h in range(4)]


def _stack_heads(w, hms):
    return jnp.concatenate([jnp.where(hm, w, jnp.zeros_like(w)) for hm in hms], axis=0)


def _state_update(st_ref, st, e_last, vb, ks):
    ktv = _dot_tn(vb, ks)
    r2 = lax.broadcasted_iota(jnp.int32, (256, 256), 0) // 64
    c2 = lax.broadcasted_iota(jnp.int32, (256, 256), 1) // 64
    st_ref[...] = st * e_last + jnp.where(r2 == c2, ktv, 0.0)


def _hgrn_chunk_exact(qd, fz, v, st_ref, gl, ball, lmask_ref, d, rev, lb_zero):
    q, kk, logf = _hgrn_gates(qd, fz, gl, lb_zero)
    e = jnp.exp(_dot01(ball, logf))
    row = lax.broadcasted_iota(jnp.int32, (CHUNK, 256), 0)
    tau = (CHUNK - 1 - row) if rev else row
    hms = _head_masks(CHUNK)
    att = _dot_nt(_stack_heads(q.astype(BF16), hms), kk.astype(BF16)) * lmask_ref[d, 0]
    for li, sh in enumerate((5, 4, 3, 2, 1, 0)):
        up = ((tau >> sh) & 1) == 1
        w = (jnp.where(up, q, kk) * e[128 + 64 * li:192 + 64 * li]).astype(BF16)
        att = att + _dot_nt(_stack_heads(w, hms), w) * lmask_ref[d, 1 + li]
    vb = v.astype(BF16)
    st = st_ref[...]
    o = _dot_nt((q * e[0:64]).astype(BF16), st.astype(BF16))
    for h in range(4):
        o = o + _dot(att[64 * h:64 * (h + 1)].astype(BF16), jnp.where(hms[h], vb, jnp.zeros_like(vb)))
    tl = 0 if rev else CHUNK - 1
    _state_update(st_ref, st, e[tl:tl + 1], vb, (kk * e[64:128]).astype(BF16))
    return o


def _hgrn_chunk_fast(c, qt, kt, cum_mid, cum_last, v_ref, st_ref, o_ref, rev):
    hms = _head_masks(CHUNK)
    r = lax.broadcasted_iota(jnp.int32, (4 * CHUNK, CHUNK), 0) & (CHUNK - 1)
    s = lax.broadcasted_iota(jnp.int32, (4 * CHUNK, CHUNK), 1)
    causal = (s >= r) if rev else (s <= r)
    sl = slice(c * CHUNK, (c + 1) * CHUNK)
    qt, kt = qt[sl], kt[sl]
    vb = v_ref[sl, :].astype(BF16)
    att = jnp.where(causal, _dot_nt(_stack_heads(qt.astype(BF16), hms), kt.astype(BF16)), 0.0)
    st = st_ref[...]
    o = _dot_nt((qt * jnp.exp(cum_mid[c])).astype(BF16), st.astype(BF16))
    for h in range(4):
        o = o + _dot(att[64 * h:64 * (h + 1)].astype(BF16), jnp.where(hms[h], vb, jnp.zeros_like(vb)))
    o_ref[sl, :] = o
    _state_update(st_ref, st, jnp.exp(cum_last[c]), vb, (kt * jnp.exp(cum_last[c] - cum_mid[c])).astype(BF16))


def _hgrn_kernel(qf_ref, ff_ref, vf_ref, qb_ref, fb_ref, vb_ref, gl_ref, ball_ref, lmask_ref, bblk_ref,
                 of_ref, ob_ref, sf_ref, sb_ref, sf0_ref, sb0_ref, *, lb_zero):
    @pl.when(pl.program_id(1) == 0)
    def _():
        sf_ref[...] = jnp.zeros_like(sf_ref)
        sb_ref[...] = jnp.zeros_like(sb_ref)

    sf0_ref[...] = sf_ref[...]
    sb0_ref[...] = sb_ref[...]
    dirs = ((qf_ref, ff_ref, vf_ref, sf_ref, of_ref, False), (qb_ref, fb_ref, vb_ref, sb_ref, ob_ref, True))
    rng, pre = 0.0, []
    for d, (q_ref, f_ref, v_ref, st_ref, o_ref, rev) in enumerate(dirs):
        q, kk, logf = _hgrn_gates(q_ref[...], f_ref[...], gl_ref[d], lb_zero)
        cum = _dot01(bblk_ref[d], logf)
        mid_row, last_row = (CHUNK // 2, 0) if rev else (CHUNK // 2 - 1, CHUNK - 1)
        cum_mid = [cum[c * CHUNK + mid_row:c * CHUNK + mid_row + 1] for c in range(NCHUNK)]
        cum_last = [cum[c * CHUNK + last_row:c * CHUNK + last_row + 1] for c in range(NCHUNK)]
        a = jnp.concatenate([cum[c * CHUNK:(c + 1) * CHUNK] - cum_mid[c] for c in range(NCHUNK)], axis=0)
        rng = jnp.maximum(rng, jnp.max(jnp.abs(a)))
        pre.append((q * jnp.exp(a), kk * jnp.exp(-a), cum_mid, cum_last))
    for step in range(NCHUNK):
        for d, (q_ref, f_ref, v_ref, st_ref, o_ref, rev) in enumerate(dirs):
            qt, kt, cum_mid, cum_last = pre[d]
            c = NCHUNK - 1 - step if rev else step
            _hgrn_chunk_fast(c, qt, kt, cum_mid, cum_last, v_ref, st_ref, o_ref, rev)

    @pl.when(jnp.logical_not(rng <= HGRN_FAST_RANGE))
    def _():
        sf_ref[...] = sf0_ref[...]
        sb_ref[...] = sb0_ref[...]

        def body(c, carry):
            rf = pl.ds(pl.multiple_of(c * CHUNK, CHUNK), CHUNK)
            rb = pl.ds(pl.multiple_of((NCHUNK - 1 - c) * CHUNK, CHUNK), CHUNK)
            of_ref[rf, :] = _hgrn_chunk_exact(qf_ref[rf, :], ff_ref[rf, :], vf_ref[rf, :], sf_ref,
                                              gl_ref[0], ball_ref[0], lmask_ref, 0, False, lb_zero)
            ob_ref[rb, :] = _hgrn_chunk_exact(qb_ref[rb, :], fb_ref[rb, :], vb_ref[rb, :], sb_ref,
                                              gl_ref[1], ball_ref[1], lmask_ref, 1, True, lb_zero)
            return carry

        lax.fori_loop(0, NCHUNK, body, 0)


def _hgrn_call(u_f32, gl, ball, lmask, bblk, lb_zero):
    b = u_f32.shape[0]

    def fwd(col):
        return pl.BlockSpec((None, TB, 256), lambda b, i: (b, jnp.where(i == 0, CTX_BLK, i - 1), col))

    def bwd(col):
        return pl.BlockSpec((None, TB, 256), lambda b, i: (b, jnp.where(i == 0, CTX_BLK, CTX_BLK - i), col))

    return pl.pallas_call(
        functools.partial(_hgrn_kernel, lb_zero=lb_zero),
        grid=(b, NBLK),
        in_specs=[fwd(1), fwd(2), fwd(4), bwd(1), bwd(3), bwd(4),
                  _const_spec((2, 2, 256)), _const_spec((2, 8 * CHUNK, CHUNK)),
                  _const_spec((2, 7, 4 * CHUNK, CHUNK)), _const_spec((2, TB, TB))],
        out_specs=[fwd(0), bwd(0)],
        out_shape=[jax.ShapeDtypeStruct((b, T_ALL, 256), F32)] * 2,
        scratch_shapes=[pltpu.VMEM((256, 256), F32)] * 4,
        compiler_params=_cparams(("arbitrary", "arbitrary")),
        name="hgrn_scan",
    )(u_f32, u_f32, u_f32, u_f32, u_f32, u_f32, gl, ball, lmask, bblk)


def _outproj_kernel(*refs, nx, with_ctx):
    x = _x_tile(refs[:nx])
    (oa_ref, p_ref, q_ref, oc_ref, of_ref, ob_ref, gd_ref, hg_ref, hh_ref, cs_ref,
     wf_ref, wo_ref, gp_ref, gtc_ref, gtb_ref, o_ref) = refs[nx:]
    z = _dot(p_ref[...].astype(BF16), cs_ref[0:256, :]) + _dot(q_ref[...].astype(BF16), cs_ref[256:512, :])
    o_b = _dot(z.astype(BF16), wf_ref[...])
    od = of_ref[...] + ob_ref[...]
    sq = od * od
    hi = sq.astype(BF16)
    lo = (sq - hi.astype(F32)).astype(BF16)
    ms = (_dot(hi, hh_ref[...]) + _dot(lo, hh_ref[...])) * (1.0 / HEAD_DIM)
    gd = gd_ref[...]
    o_d = od * lax.rsqrt(ms + EPS) * hg_ref[...] * (gd * jax.nn.sigmoid(gd))
    y = (_dot(oa_ref[...], wo_ref[0:256, :]) + _dot(o_b.astype(BF16), wo_ref[256:512, :])
         + _dot(oc_ref[...], wo_ref[512:768, :]) + _dot(o_d.astype(BF16), wo_ref[768:1024, :]))
    gt = _mod(gtc_ref, gtb_ref, o_ref.shape[0], with_ctx)
    o_ref[...] = x + gt * (_rms(y) * gp_ref[...])


def _outproj_call(x_parts, o_a, p, q, o_c, o_f, o_bk, u_f32, hg, hh, cs, w_fnet, w_out, g_post, mods, with_ctx):
    b = x_parts[0].shape[0]
    tm, t = (TM_ALL, T_ALL) if with_ctx else (TM_LAT, SEQ)
    col = lambda c: pl.BlockSpec((None, tm, 256), lambda b, i: (b, i, c))
    split = len(x_parts) == 2
    x_args = (x_parts[0],) * 3 + (x_parts[1],) if split else x_parts
    x_specs = _x_specs(True) if split else [pl.BlockSpec((None, tm, D_MODEL), lambda b, i: (b, i, 0))]
    return pl.pallas_call(
        functools.partial(_outproj_kernel, nx=len(x_args), with_ctx=with_ctx),
        grid=(b, t // tm),
        in_specs=[
            *x_specs,
            col(0), col(0), col(0), col(0), col(0), col(0), col(5),
            _const_spec((1, 256)), _const_spec((256, 256)), _const_spec((512, 256)),
            _const_spec((256, 256)), _const_spec((D_MODEL, D_MODEL)), _const_spec((1, D_MODEL)),
            *_mod_specs(2),
        ],
        out_specs=pl.BlockSpec((None, tm, D_MODEL), lambda b, i: (b, i, 0)),
        out_shape=jax.ShapeDtypeStruct((b, t, D_MODEL), F32),
        compiler_params=_cparams(("arbitrary", "arbitrary")),
        name="out_proj",
    )(*x_args, o_a, p, q, o_c, o_f, o_bk, u_f32, hg, hh, cs, w_fnet, w_out, g_post, mods, mods)


def _ffn_kernel(x_ref, g_ref, shc_ref, shb_ref, scc_ref, scb_ref, w1_ref, w2_ref, gp_ref, gtc_ref, gtb_ref,
                o_ref, *, with_ctx):
    tm = x_ref.shape[0]
    x = x_ref[...]
    sh, sc = _mod(shc_ref, shb_ref, tm, with_ctx), _mod(scc_ref, scb_ref, tm, with_ctx)
    h = ((_rms(x) * g_ref[...]) * (1.0 + sc) + sh).astype(BF16)
    acc = jnp.zeros((tm, D_MODEL), F32)
    fc = 256
    for c in range(D_FF // fc):
        a = _dot(h, w1_ref[:, c * fc:(c + 1) * fc])
        g = _dot(h, w1_ref[:, D_FF + c * fc:D_FF + (c + 1) * fc])
        act = (a * jax.nn.sigmoid(a) * g).astype(BF16)
        acc = acc + _dot(act, w2_ref[c * fc:(c + 1) * fc, :])
    gt = _mod(gtc_ref, gtb_ref, tm, with_ctx)
    o_ref[...] = x + gt * (_rms(acc) * gp_ref[...])


def _ffn_call(xs, g_pre, w1, w2, g_post, mods, with_ctx):
    b, t, _ = xs.shape
    tm = TM_ALL if with_ctx else TM_LAT
    return pl.pallas_call(
        functools.partial(_ffn_kernel, with_ctx=with_ctx),
        grid=(b, t // tm),
        in_specs=[
            pl.BlockSpec((None, tm, D_MODEL), lambda b, i: (b, i, 0)),
            _const_spec((1, D_MODEL)),
            *_mod_specs(3), *_mod_specs(4),
            _const_spec((D_MODEL, 2 * D_FF)), _const_spec((D_FF, D_MODEL)), _const_spec((1, D_MODEL)),
            *_mod_specs(5),
        ],
        out_specs=pl.BlockSpec((None, tm, D_MODEL), lambda b, i: (b, i, 0)),
        out_shape=jax.ShapeDtypeStruct(xs.shape, F32),
        compiler_params=_cparams(("arbitrary", "arbitrary")),
        name="ffn",
    )(xs, g_pre, mods, mods, mods, mods, w1, w2, g_post, mods, mods)


def kernel(x, c, ctx, c_ctx, w_ada, b_ada, g_pre_mix, g_post_mix, g_pre_ffn, g_post_ffn, w_in, w_out,
           na_rpb, w_fnet, swa_sink, hgrn_lb_logits, hgrn_norm_g, w_ffn_in, w_ffn_out):
    tb = _tables()
    bf = lambda name: jnp.asarray(tb[name]).astype(BF16)
    cos, sin = jnp.asarray(tb["rope_cos"]), jnp.asarray(tb["rope_sin"])
    f64, g2, fc, chan_cs, hh = bf("f64"), bf("g_stage2"), bf("f_ctx"), bf("chan_cs"), bf("head_ones")
    ball, bblk, lmask = bf("hgrn_b"), bf("hgrn_bblk"), jnp.asarray(tb["hgrn_mask"])
    na_mask, swa_mask = jnp.asarray(tb["na_mask"]), jnp.asarray(tb["swa_mask"])

    cc = jnp.concatenate([c, c_ctx[None, :]], axis=0)
    mods = _ada_call(jnp.broadcast_to(cc[:, :, None], (3, D_MODEL, 128)), w_ada, b_ada)

    lb = jnp.cumsum(jax.nn.softmax(hgrn_lb_logits.astype(F32), axis=1), axis=1)
    lb = lb - lb[:, :1]
    gl_all = jnp.stack([jnp.log(lb), jnp.log1p(-lb)], axis=2)

    xs = None
    for l in range(DEPTH):
        with_ctx = l < DEPTH - 1
        row = lambda a: a[l][None, :]
        x_all = (x, ctx) if l == 0 else (xs,)
        x_res = x_all if with_ctx else x_all[:1]
        u_att, u_f32 = _proj_call(x_all, row(g_pre_mix), mods[l], w_in[l].astype(BF16), cos, sin)

        rpb = jnp.pad(na_rpb[l], ((0, 0), (NA_DR_PAD, NA_DR_ROWS - NA_DR_PAD - (2 * NA_WIN_R - 1)), (0, 0)))
        t2 = _na_bias_call(rpb[:, :, tb["na_lane_l"]], rpb[:, :, tb["na_lane_r"]])
        o_a = _na_call(u_att, na_mask, t2, with_ctx)
        o_c = _swa_call(u_att, swa_sink[l], swa_mask, with_ctx)
        p, q = _fnet_call(u_f32, f64, g2, fc)
        o_f, o_bk = _hgrn_call(u_f32, gl_all[:, l], ball, lmask, bblk, lb_zero=(l == 0))

        xs = _outproj_call(x_res, o_a, p, q, o_c, o_f, o_bk, u_f32, row(hgrn_norm_g), hh, chan_cs,
                           w_fnet[l].astype(BF16), w_out[l].astype(BF16), row(g_post_mix), mods[l], with_ctx)
        xs = _ffn_call(xs, row(g_pre_ffn), w_ffn_in[l].astype(BF16), w_ffn_out[l].astype(BF16),
                       row(g_post_ffn), mods[l], with_ctx)
    return xs
```

```python
import functools

import numpy as np
import jax
import jax.numpy as jnp
from jax import lax
from jax.experimental import pallas as pl
from jax.experimental.pallas import tpu as pltpu

F32, BF16 = jnp.float32, jnp.bfloat16

D_MODEL = 1024
SEQ = 8192
DEPTH = 2
GRID_W = 64
GRID_ROWS = SEQ // GRID_W
CTX = 256
HEAD_DIM = 64
NA_WIN_R, NA_WIN_C = 8, 16
SWA_WINDOW = 128
ROPE_THETA = 10000.0
EPS = 1e-6
D_FF = 2816
D_IN = 2816

TB = 256
T_ALL = SEQ + CTX
NBLK = T_ALL // TB
CTX_BLK = SEQ // TB
TM_ALL = 768
TM_LAT = 1024
NA_KROWS = 12
NA_KEYS = NA_KROWS * GRID_W
NA_DR_PAD = 4
NA_DR_ROWS = 24
SWA_KEYS = TB + 2 * SWA_WINDOW
CHUNK = 64
NCHUNK = TB // CHUNK
HGRN_FAST_RANGE = 60.0
Q_SCALE = HEAD_DIM ** -0.5
VMEM_LIMIT = 56 * 1024 * 1024

ATT_W, F32_W = 1280, 1536


def _dot(a, b):
    return jnp.dot(a, b, preferred_element_type=F32)


def _dot_nt(a, b):
    return lax.dot_general(a, b, (((1,), (1,)), ((), ())), preferred_element_type=F32)


def _dot_tn(a, b):
    return lax.dot_general(a, b, (((0,), (0,)), ((), ())), preferred_element_type=F32)


def _cparams(sem):
    return pltpu.CompilerParams(dimension_semantics=sem, vmem_limit_bytes=VMEM_LIMIT)


def _const_spec(shape):
    nd = len(shape)
    return pl.BlockSpec(shape, lambda *_: (0,) * nd, pipeline_mode=pl.Buffered(1))


def _layer_spec(shape, l):
    nd = len(shape)
    return pl.BlockSpec((None,) + shape, lambda *_: (l,) + (0,) * nd, pipeline_mode=pl.Buffered(1))


@functools.lru_cache(maxsize=None)
def _tables():
    t = {}
    lane = np.arange(128)
    d = lane % 64
    freq = ROPE_THETA ** (-(d % 16) / 16.0)
    pos = np.arange(SEQ)
    p = np.where(d[None, :] < 32, (pos // GRID_W)[:, None], (pos % GRID_W)[:, None]).astype(np.float64)
    ang = p * freq[None, :]
    sign = np.where((d % 32) < 16, -1.0, 1.0)
    cos = np.concatenate([np.cos(ang), np.ones((CTX, 128))], axis=0)
    sin = np.concatenate([np.sin(ang) * sign[None, :], np.zeros((CTX, 128))], axis=0)
    t["rope_cos"], t["rope_sin"] = cos.astype(np.float32), sin.astype(np.float32)

    masks = []
    a, cq = np.arange(4), np.arange(GRID_W)
    rho, ck = np.arange(NA_KROWS), np.arange(GRID_W)
    for r0 in (0, 8, GRID_ROWS - 4):
        kr0 = int(np.clip(r0 - 4, 0, GRID_ROWS - NA_KROWS))
        rq, rk = r0 + a, kr0 + rho
        start = np.clip(rq - NA_WIN_R // 2, 0, GRID_ROWS - NA_WIN_R)
        vr = (rk[None, :] >= start[:, None]) & (rk[None, :] < start[:, None] + NA_WIN_R)
        cs = np.clip(cq - NA_WIN_C // 2, 0, GRID_W - NA_WIN_C)
        vc = (ck[None, :] >= cs[:, None]) & (ck[None, :] < cs[:, None] + NA_WIN_C)
        val = vr[:, None, :, None] & vc[None, :, None, :]
        masks.append(np.where(val, 0.0, -np.inf).reshape(TB, NA_KEYS))
    masks.append(np.full((TB, NA_KEYS), -np.inf))
    t["na_mask"] = np.stack(masks).astype(np.float32)
    n = np.arange(128)
    clipc = lambda m: np.clip(m, -(NA_WIN_C - 1), NA_WIN_C - 1) + NA_WIN_C - 1
    t["na_lane_l"] = clipc(np.where(n < 64, n, n - 128)).astype(np.int32)
    t["na_lane_r"] = clipc(n - 64).astype(np.int32)

    rel = np.arange(SWA_KEYS)[None, :] - np.arange(TB)[:, None]
    swa = [np.where(np.abs(rel + off) <= SWA_WINDOW, 0.0, -np.inf) for off in (0, -SWA_WINDOW, -2 * SWA_WINDOW)]
    swa.append(np.full((TB, SWA_KEYS), -np.inf))
    t["swa_mask"] = np.stack(swa).astype(np.float32)

    def cs_tab(num, den):
        ang = 2.0 * np.pi * (num % den) / den
        return np.cos(ang), np.sin(ang)

    k1, t1 = np.arange(64)[:, None], np.arange(64)[None, :]
    c64, s64 = cs_tab(k1 * t1, 64)
    t["f64"] = np.concatenate([c64, -s64], axis=0).astype(np.float32)
    k2, t2 = np.arange(128)[:, None], np.arange(128)[None, :]
    g = []
    for kk in range(64):
        gc, gs = cs_tab((64 * k2 + kk) * t2, SEQ)
        g.append(np.block([[gc, gs], [-gs, gc]]))
    t["g_stage2"] = np.stack(g).astype(np.float32)
    kc, tc = np.arange(CTX)[:, None], np.arange(CTX)[None, :]
    cc, sc = cs_tab(kc * tc, CTX)
    t["f_ctx"] = np.concatenate([cc, -sc], axis=0).astype(np.float32)
    ch = np.arange(256)
    same = (ch[:, None] // 64) == (ch[None, :] // 64)
    c_ch, s_ch = cs_tab((ch[:, None] % 64) * (ch[None, :] % 64), 64)
    t["chan_cs"] = np.concatenate([np.where(same, c_ch, 0.0), np.where(same, s_ch, 0.0)], axis=0).astype(np.float32)
    t["head_ones"] = same.astype(np.float32)

    balls, lmasks, bblk = [], [], []
    for rev in (False, True):
        tau = (CHUNK - 1 - np.arange(CHUNK)) if rev else np.arange(CHUNK)
        tt, tj = tau[:, None], tau[None, :]
        blocks = [(tj <= tt), (tj > tt)]
        masks = [tt == tj]
        for h in (32, 16, 8, 4, 2, 1):
            up = ((tt // h) % 2) == 1
            mid = (tt // (2 * h)) * (2 * h) + h - 1
            blocks.append(np.where(up, (tj > mid) & (tj <= tt), (tj > tt) & (tj <= mid)))
            up_s = ((tj // h) % 2) == 1
            masks.append((tt // (2 * h) == tj // (2 * h)) & up & ~up_s)
        balls.append(np.concatenate(blocks, axis=0))
        lmasks.append(np.stack([np.tile(m, (4, 1)) for m in masks]))
        bblk.append(np.kron(np.eye(NCHUNK), blocks[0]))
    t["hgrn_b"] = np.stack(balls).astype(np.float32)
    t["hgrn_mask"] = np.stack(lmasks).astype(np.float32)
    t["hgrn_bblk"] = np.stack(bblk).astype(np.float32)
    return t


def _ada_kernel(cb_ref, w_ref, b_ref, o_ref, s_ref):
    tn = w_ref.shape[1]
    nrep = tn // 128

    @pl.when((pl.program_id(0) == 0) & (pl.program_id(1) == 0))
    def _():
        cb = cb_ref[...]
        s_ref[...] = cb * jax.nn.sigmoid(cb)

    def body(kc, accs):
        k0 = pl.multiple_of(kc * 8, 8)
        wt = w_ref[pl.ds(k0, 8), :]
        return tuple(accs[r] + wt * jnp.concatenate([s_ref[r, pl.ds(k0, 8), :]] * nrep, axis=1) for r in range(3))

    accs = lax.fori_loop(0, D_MODEL // 8, body, tuple(jnp.zeros((8, tn), F32) for _ in range(3)), unroll=8)
    for r in range(3):
        o_ref[r] = jnp.sum(accs[r], axis=0, keepdims=True) + b_ref[...]


def _ada_call(cb, w_ada, b_ada):
    tn = 1536
    return pl.pallas_call(
        _ada_kernel,
        grid=(DEPTH, 6 * D_MODEL // tn),
        in_specs=[
            pl.BlockSpec((3, D_MODEL, 128), lambda l, j: (0, 0, 0)),
            pl.BlockSpec((None, D_MODEL, tn), lambda l, j: (l, 0, j)),
            pl.BlockSpec((None, 1, tn), lambda l, j: (l, 0, j)),
        ],
        out_specs=pl.BlockSpec((None, 3, 1, tn), lambda l, j: (l, 0, 0, j)),
        out_shape=jax.ShapeDtypeStruct((DEPTH, 3, 1, 6 * D_MODEL), F32),
        scratch_shapes=[pltpu.VMEM((3, D_MODEL, 128), F32)],
        compiler_params=_cparams(("arbitrary", "arbitrary")),
        name="adaln",
    )(cb, w_ada, b_ada.reshape(DEPTH, 1, 6 * D_MODEL))


def _mod_specs(col):
    return [pl.BlockSpec((None, 1, D_MODEL), lambda b, i: (2, 0, col)),
            pl.BlockSpec((None, 1, D_MODEL), lambda b, i: (b, 0, col))]


def _row_parts(tm, with_ctx):
    return [(0, tm - CTX), (tm - CTX, tm)] if with_ctx else [(0, tm // 2), (tm // 2, tm)]


def _mod(mc_ref, mb_ref, part, with_ctx):
    if not with_ctx or part == 0:
        return mb_ref[...]
    last = pl.program_id(1) == pl.num_programs(1) - 1
    return jnp.where(last, mc_ref[...], mb_ref[...])


def _modulated(x, g_ref, sh_refs, sc_refs, part, with_ctx):
    gs = g_ref[...] * (1.0 + _mod(*sc_refs, part, with_ctx))
    return (_rms(x) * gs + _mod(*sh_refs, part, with_ctx)).astype(BF16)


def _gated_residual(x, y, gp_ref, gt_refs, part, with_ctx):
    return x + (_mod(*gt_refs, part, with_ctx) * gp_ref[...]) * _rms(y)


def _rms(x):
    return x * lax.rsqrt(jnp.mean(x * x, axis=-1, keepdims=True) + EPS)


def _x_specs(split):
    if not split:
        return [pl.BlockSpec((None, TM_ALL, D_MODEL), lambda b, i: (b, i, 0))]
    nx = SEQ // TB
    return [pl.BlockSpec((None, TB, D_MODEL), lambda b, i, j=j: (b, jnp.minimum(3 * i + j, nx - 1), 0))
            for j in range(TM_ALL // TB)] + [pl.BlockSpec((None, CTX, D_MODEL), lambda b, i: (b, 0, 0))]


def _x_part(x_refs, part, lo, hi):
    if len(x_refs) == 1:
        return x_refs[0][lo:hi, :]
    xa, xb, xc, cx = x_refs
    if part == 0:
        return jnp.concatenate([xa[...], xb[...]], axis=0)
    last = pl.program_id(1) == pl.num_programs(1) - 1
    return jnp.where(last, cx[...], xc[...])


def _proj_kernel(*refs, nx):
    g_ref, shc_ref, shb_ref, scc_ref, scb_ref, w_ref, cos_ref, sin_ref, att_ref, f32_ref = refs[nx:]
    for part, (lo, hi) in enumerate(_row_parts(TM_ALL, True)):
        rows = slice(lo, hi)
        h = _modulated(_x_part(refs[:nx], part, lo, hi), g_ref, (shc_ref, shb_ref), (scc_ref, scb_ref), part, True)
        cos, sin = cos_ref[rows, :], sin_ref[rows, :]
        first = (lax.broadcasted_iota(jnp.int32, (hi - lo, 128), 1) % 32) < 16

        def rope(v):
            return v * cos + jnp.where(first, pltpu.roll(v, 112, 1), pltpu.roll(v, 16, 1)) * sin

        a = _dot(h, w_ref[:, 0:768])
        att_ref[rows, 0:256] = (a[:, 0:256] * Q_SCALE).astype(BF16)
        att_ref[rows, 256:768] = a[:, 256:768].astype(BF16)
        f32_ref[rows, 0:256] = _dot(h, w_ref[:, 768:1024])
        s = _dot(h, w_ref[:, 1024:1536])
        att_ref[rows, 768:896] = (rope(s[:, 0:128]) * Q_SCALE).astype(BF16)
        att_ref[rows, 896:1024] = (rope(s[:, 128:256]) * Q_SCALE).astype(BF16)
        att_ref[rows, 1024:1152] = rope(s[:, 256:384]).astype(BF16)
        att_ref[rows, 1152:1280] = s[:, 384:512].astype(BF16)
        f32_ref[rows, 256:1536] = _dot(h, w_ref[:, 1536:2816])


def _proj_call(x_parts, g, mods, w_in, l, cos, sin):
    b = x_parts[0].shape[0]
    tm = TM_ALL
    split = len(x_parts) == 2
    x_args = (x_parts[0],) * 3 + (x_parts[1],) if split else x_parts
    return pl.pallas_call(
        functools.partial(_proj_kernel, nx=len(x_args)),
        grid=(b, T_ALL // tm),
        in_specs=[
            *_x_specs(split),
            _const_spec((1, D_MODEL)),
            *_mod_specs(0), *_mod_specs(1),
            _layer_spec((D_MODEL, D_IN), l),
            pl.BlockSpec((tm, 128), lambda b, i: (i, 0)),
            pl.BlockSpec((tm, 128), lambda b, i: (i, 0)),
        ],
        out_specs=[
            pl.BlockSpec((None, tm, ATT_W), lambda b, i: (b, i, 0)),
            pl.BlockSpec((None, tm, F32_W), lambda b, i: (b, i, 0)),
        ],
        out_shape=[jax.ShapeDtypeStruct((b, T_ALL, ATT_W), BF16), jax.ShapeDtypeStruct((b, T_ALL, F32_W), F32)],
        compiler_params=_cparams(("arbitrary", "arbitrary")),
        name="proj_in",
    )(*x_args, g, mods, mods, mods, mods, w_in, cos, sin)


def _na_bias_kernel(vl_ref, vr_ref, o_ref):
    lane = lax.broadcasted_iota(jnp.int32, (GRID_W, 128), 1)
    for e in range(NA_DR_ROWS - 1):
        left = pltpu.roll(jnp.broadcast_to(vl_ref[e:e + 1, :], (GRID_W, 128)), 0, 1, stride=1, stride_axis=0)
        right = pltpu.roll(jnp.broadcast_to(vr_ref[e + 1:e + 2, :], (GRID_W, 128)), 0, 1, stride=1, stride_axis=0)
        o_ref[e] = jnp.where(lane < 64, left, right)


def _na_bias_call(vl, vr):
    spec = pl.BlockSpec((None, NA_DR_ROWS, 128), lambda h: (h, 0, 0))
    return pl.pallas_call(
        _na_bias_kernel,
        grid=(4,),
        in_specs=[spec, spec],
        out_specs=pl.BlockSpec((None, NA_DR_ROWS - 1, GRID_W, 128), lambda h: (h, 0, 0, 0)),
        out_shape=jax.ShapeDtypeStruct((4, NA_DR_ROWS - 1, GRID_W, 128), F32),
        compiler_params=_cparams(("arbitrary",)),
        name="na_bias",
    )(vl, vr)


AT_ROWS = 16


def _attn_block_variant(b, i):
    return (jnp.where(i == CTX_BLK, 3, jnp.where(i == 0, 0, jnp.where(i == CTX_BLK - 1, 2, 1))), 0, 0)


def _lane_half(shape, j):
    lane = lax.broadcasted_iota(jnp.int32, shape, 1)
    return (lane >= 64 * j) & (lane < 64 * (j + 1))


def _na_kernel(q_ref, k_ref, v_ref, mask_ref, t2_ref, o_ref, s_ref, p_ref):
    i = pl.program_id(1)
    is_ctx = i == CTX_BLK
    r0 = i * 4
    kr0 = jnp.where(is_ctx, 0, jnp.clip(r0 - 4, 0, GRID_ROWS - NA_KROWS))
    kstart = pl.multiple_of(kr0 * GRID_W, GRID_W)
    delta = jnp.where(is_ctx, 3, kr0 - r0 + NA_WIN_R - 1) + NA_DR_PAD
    lane = lax.broadcasted_iota(jnp.int32, (TB, 128), 1)
    for hp in range(2):
        cs = slice(hp * 128, (hp + 1) * 128)
        q2 = q_ref[:, cs]
        kw, kc = k_ref[pl.ds(kstart, NA_KEYS), cs], k_ref[SEQ:T_ALL, cs]
        qm = jnp.concatenate([jnp.where(_lane_half((TB, 128), j), q2, jnp.zeros_like(q2)) for j in range(2)], axis=0)
        s_ref[hp, :, 0:CTX] = _dot_nt(qm, kc)
        s_ref[hp, :, CTX:] = _dot_nt(qm, kw)
    for hp in range(2):
        for r in range(2 * TB // AT_ROWS):
            rows = slice(r * AT_ROWS, (r + 1) * AT_ROWS)
            j, rq = divmod(r, TB // AT_ROWS)
            a, sub = divmod(rq, GRID_W // AT_ROWS)
            trow = slice(sub * AT_ROWS, (sub + 1) * AT_ROWS)
            bias = jnp.concatenate(
                [t2_ref[2 * hp + j, delta + 2 * rp - a, trow, :] for rp in range(NA_KROWS // 2)], axis=1)
            s_nb = s_ref[hp, rows, CTX:] + bias + mask_ref[rq * AT_ROWS:(rq + 1) * AT_ROWS, :]
            s_c = s_ref[hp, rows, 0:CTX]
            m = jnp.maximum(jnp.max(s_nb, axis=-1, keepdims=True), jnp.max(s_c, axis=-1, keepdims=True))
            p_ref[hp, rows, 0:CTX] = jnp.exp((s_c - m).astype(BF16))
            p_ref[hp, rows, CTX:] = jnp.exp((s_nb - m).astype(BF16))
    for hp in range(2):
        cs = slice(hp * 128, (hp + 1) * 128)
        vw, vc = v_ref[pl.ds(kstart, NA_KEYS), cs], v_ref[SEQ:T_ALL, cs]
        o2 = (_dot(p_ref[hp, :, 0:CTX], jnp.concatenate([vc, jnp.ones_like(vc)], axis=1))
              + _dot(p_ref[hp, :, CTX:], jnp.concatenate([vw, jnp.ones_like(vw)], axis=1)))
        top = o2[0:TB, 0:128] * (1.0 / o2[0:TB, 128:256])
        bot = o2[TB:2 * TB, 0:128] * (1.0 / o2[TB:2 * TB, 128:256])
        o_ref[:, cs] = jnp.where(lane < 64, top, bot).astype(BF16)


def _na_call(u_att, mask, t2, with_ctx):
    b = u_att.shape[0]
    nb = NBLK if with_ctx else CTX_BLK
    return pl.pallas_call(
        _na_kernel,
        grid=(b, nb),
        in_specs=[
            pl.BlockSpec((None, TB, 256), lambda b, i: (b, i, 0)),
            pl.BlockSpec((None, T_ALL, 256), lambda b, i: (b, 0, 1)),
            pl.BlockSpec((None, T_ALL, 256), lambda b, i: (b, 0, 2)),
            pl.BlockSpec((None, TB, NA_KEYS), _attn_block_variant),
            _const_spec((4, NA_DR_ROWS - 1, GRID_W, 128)),
        ],
        out_specs=pl.BlockSpec((None, TB, 256), lambda b, i: (b, i, 0)),
        out_shape=jax.ShapeDtypeStruct((b, nb * TB, 256), BF16),
        scratch_shapes=[pltpu.VMEM((2, 2 * TB, CTX + NA_KEYS), F32), pltpu.VMEM((2, 2 * TB, CTX + NA_KEYS), BF16)],
        compiler_params=_cparams(("arbitrary", "arbitrary")),
        name="na_attn",
    )(u_att, u_att, u_att, mask, t2)


def _swa_kernel(sink_ref, q_ref, k_ref, v_ref, mask_ref, o_ref, s_ref, p_ref, esink_ref):
    i = pl.program_id(1)
    ks0 = jnp.clip(i * TB - SWA_WINDOW, 0, SEQ - SWA_KEYS)
    kstart = pl.multiple_of(ks0, SWA_WINDOW)
    kw, vw = k_ref[pl.ds(kstart, SWA_KEYS), :], v_ref[pl.ds(kstart, SWA_KEYS), :]
    kc, vc = k_ref[SEQ:T_ALL, :], v_ref[SEQ:T_ALL, :]
    lane = lax.broadcasted_iota(jnp.int32, (TB, 128), 1)
    vc2 = jnp.concatenate([vc, jnp.ones_like(vc)], axis=1)
    vw2 = jnp.concatenate([vw, jnp.ones_like(vw)], axis=1)
    for g in range(2):
        q2 = q_ref[:, g * 128:(g + 1) * 128].astype(F32)
        qa = jnp.concatenate([jnp.where(_lane_half((TB, 128), g), q2 if j == g else pltpu.roll(q2, 64, 1), 0.0)
                              for j in range(2)], axis=0).astype(BF16)
        s_ref[g, :, 0:CTX] = _dot_nt(qa, kc)
        s_ref[g, :, CTX:] = _dot_nt(qa, kw)
    for g in range(2):
        for r in range(2 * TB // AT_ROWS):
            rows = slice(r * AT_ROWS, (r + 1) * AT_ROWS)
            j, rq = divmod(r, TB // AT_ROWS)
            sink = sink_ref[2 * g + j]
            s_l = s_ref[g, rows, CTX:] + mask_ref[rq * AT_ROWS:(rq + 1) * AT_ROWS, :]
            s_c = s_ref[g, rows, 0:CTX]
            m = jnp.maximum(jnp.maximum(jnp.max(s_l, axis=-1, keepdims=True),
                                        jnp.max(s_c, axis=-1, keepdims=True)), sink)
            p_ref[g, rows, 0:CTX] = jnp.exp((s_c - m).astype(BF16))
            p_ref[g, rows, CTX:] = jnp.exp((s_l - m).astype(BF16))
            esink_ref[g, rows, :] = jnp.exp(jnp.broadcast_to(sink - m, (AT_ROWS, 128)))
    for g in range(2):
        o2 = _dot(p_ref[g, :, 0:CTX], vc2) + _dot(p_ref[g, :, CTX:], vw2)
        o = o2[:, 0:128] * (1.0 / (o2[:, 128:256] + esink_ref[g]))
        outs = [o[j * TB:(j + 1) * TB] if j == g else pltpu.roll(o[j * TB:(j + 1) * TB], 64, 1) for j in range(2)]
        o_ref[:, g * 128:(g + 1) * 128] = jnp.where(lane < 64, outs[0], outs[1]).astype(BF16)


def _swa_call(u_att, sink, mask, with_ctx):
    b = u_att.shape[0]
    nb = NBLK if with_ctx else CTX_BLK
    return pl.pallas_call(
        _swa_kernel,
        grid=(b, nb),
        in_specs=[
            pl.BlockSpec(memory_space=pltpu.SMEM),
            pl.BlockSpec((None, TB, 256), lambda b, i: (b, i, 3)),
            pl.BlockSpec((None, T_ALL, 128), lambda b, i: (b, 0, 8)),
            pl.BlockSpec((None, T_ALL, 128), lambda b, i: (b, 0, 9)),
            pl.BlockSpec((None, TB, SWA_KEYS), _attn_block_variant),
        ],
        out_specs=pl.BlockSpec((None, TB, 256), lambda b, i: (b, i, 0)),
        out_shape=jax.ShapeDtypeStruct((b, nb * TB, 256), BF16),
        scratch_shapes=[pltpu.VMEM((2, 2 * TB, CTX + SWA_KEYS), F32), pltpu.VMEM((2, 2 * TB, CTX + SWA_KEYS), BF16),
                        pltpu.VMEM((2, 2 * TB, 128), F32)],
        compiler_params=_cparams(("arbitrary", "arbitrary")),
        name="swa_attn",
    )(sink, u_att, u_att, u_att, mask)


FN_XP = 136
FN_YP = 136
FN_ZP = 72


def _fnet_kernel(x_ref, f64_ref, g_ref, fc_ref, p_ref, q_ref, xz_ref, zq_ref, y_ref):
    sc_lat, sc_ctx = (64.0 * SEQ) ** -0.5, (64.0 * CTX) ** -0.5
    pq = _dot(fc_ref[...], x_ref[SEQ:T_ALL, :].astype(BF16))
    p_ref[SEQ:T_ALL, :] = (pq[0:CTX] * sc_ctx).astype(BF16)
    q_ref[SEQ:T_ALL, :] = (pq[CTX:2 * CTX] * sc_ctx).astype(BF16)
    f64 = f64_ref[...]

    def repitch(t1, carry):
        xz_ref[pl.ds(pl.multiple_of(t1 * FN_XP, 8), 128), :] = x_ref[pl.ds(pl.multiple_of(t1 * 128, 128), 128), :]
        return carry

    lax.fori_loop(0, 64, repitch, 0, unroll=4)

    def stage1(t2, carry):
        xs = xz_ref[pl.ds(t2, 64, stride=FN_XP), :].astype(BF16)
        y_ref[pl.ds(t2, 128, stride=FN_YP), :] = _dot(f64, xs)
        return carry

    lax.fori_loop(0, 128, stage1, 0, unroll=8)

    def stage2(k1, carry):
        yre = y_ref[pl.ds(pl.multiple_of(k1 * FN_YP, 8), 128), :]
        yim = y_ref[pl.ds(pl.multiple_of((64 + k1) * FN_YP, 8), 128), :]
        ycat = jnp.concatenate([yre, yim], axis=0).astype(BF16)
        pq = _dot(g_ref[k1], ycat)
        xz_ref[pl.ds(k1, 128, stride=FN_ZP), :] = pq[0:128] * sc_lat
        zq_ref[pl.ds(k1, 128, stride=FN_ZP), :] = pq[128:256] * sc_lat
        return carry

    lax.fori_loop(0, 64, stage2, 0, unroll=4)

    def emit(k2, carry):
        src = pl.ds(pl.multiple_of(k2 * FN_ZP, 8), 64)
        dst = pl.ds(pl.multiple_of(k2 * 64, 64), 64)
        p_ref[dst, :] = xz_ref[src, :].astype(BF16)
        q_ref[dst, :] = zq_ref[src, :].astype(BF16)
        return carry

    lax.fori_loop(0, 128, emit, 0, unroll=4)


def _fnet_call(u_f32, f64, g2, fc):
    b = u_f32.shape[0]
    spec = pl.BlockSpec((None, T_ALL, 128), lambda b, j: (b, 0, j))
    return pl.pallas_call(
        _fnet_kernel,
        grid=(b, 2),
        in_specs=[spec, _const_spec((128, 64)), _const_spec((64, 256, 256)), _const_spec((2 * CTX, CTX))],
        out_specs=[spec, spec],
        out_shape=[jax.ShapeDtypeStruct((b, T_ALL, 256), BF16)] * 2,
        scratch_shapes=[pltpu.VMEM((128 * FN_ZP, 128), F32), pltpu.VMEM((128 * FN_ZP, 128), F32),
                        pltpu.VMEM((128 * FN_YP, 128), F32)],
        compiler_params=_cparams(("arbitrary", "arbitrary")),
        name="fnet_dft",
    )(u_f32, f64, g2, fc)


def _hgrn_gates(qd, fz, gl, lb_zero):
    q = qd * jax.nn.sigmoid(qd)
    sp = jnp.log(1.0 + jnp.exp(-jnp.abs(fz)))
    ls = jnp.minimum(fz, 0.0) - sp
    lk = jnp.minimum(-fz, 0.0) - sp
    if lb_zero:
        return q, jnp.exp(lk), ls
    b_ = gl[1:2] + ls
    a_ = gl[0:1]
    logf = jnp.maximum(a_, b_) + jnp.log(1.0 + jnp.exp(-jnp.abs(a_ - b_)))
    return q, jnp.exp(gl[1:2] + lk), logf


def _dot01(b01, x):
    h1 = x.astype(BF16)
    r1 = x - h1.astype(F32)
    h2 = r1.astype(BF16)
    h3 = (r1 - h2.astype(F32)).astype(BF16)
    return _dot(b01, h1) + _dot(b01, h2) + _dot(b01, h3)


def _head_masks(rows):
    lane = lax.broadcasted_iota(jnp.int32, (rows, 256), 1)
    return [(lane >= 64 * h) & (lane < 64 * (h + 1)) for h in range(4)]


def _stack_heads(w, hms):
    return jnp.concatenate([jnp.where(hm, w, jnp.zeros_like(w)) for hm in hms], axis=0)


def _state_update(st_ref, st, e_last, vb, ks):
    ktv = _dot_tn(vb, ks)
    r2 = lax.broadcasted_iota(jnp.int32, (256, 256), 0) // 64
    c2 = lax.broadcasted_iota(jnp.int32, (256, 256), 1) // 64
    st_ref[...] = st * e_last + jnp.where(r2 == c2, ktv, 0.0)


def _hgrn_chunk_exact(qd, fz, v, st_ref, gl, ball, lmask_ref, d, rev, lb_zero):
    q, kk, logf = _hgrn_gates(qd, fz, gl, lb_zero)
    e = jnp.exp(_dot01(ball, logf))
    row = lax.broadcasted_iota(jnp.int32, (CHUNK, 256), 0)
    tau = (CHUNK - 1 - row) if rev else row
    hms = _head_masks(CHUNK)
    att = _dot_nt(_stack_heads(q.astype(BF16), hms), kk.astype(BF16)) * lmask_ref[d, 0]
    for li, sh in enumerate((5, 4, 3, 2, 1, 0)):
        up = ((tau >> sh) & 1) == 1
        w = (jnp.where(up, q, kk) * e[128 + 64 * li:192 + 64 * li]).astype(BF16)
        att = att + _dot_nt(_stack_heads(w, hms), w) * lmask_ref[d, 1 + li]
    vb = v.astype(BF16)
    st = st_ref[...]
    o = _dot_nt((q * e[0:64]).astype(BF16), st.astype(BF16))
    for h in range(4):
        o = o + _dot(att[64 * h:64 * (h + 1)].astype(BF16), jnp.where(hms[h], vb, jnp.zeros_like(vb)))
    tl = 0 if rev else CHUNK - 1
    _state_update(st_ref, st, e[tl:tl + 1], vb, (kk * e[64:128]).astype(BF16))
    return o


def _hgrn_chunk_fast(c, qt, kt, cum_mid, cum_last, v_ref, st_ref, o_ref, rev):
    hms = _head_masks(CHUNK)
    r = lax.broadcasted_iota(jnp.int32, (4 * CHUNK, CHUNK), 0) & (CHUNK - 1)
    s = lax.broadcasted_iota(jnp.int32, (4 * CHUNK, CHUNK), 1)
    causal = (s >= r) if rev else (s <= r)
    sl = slice(c * CHUNK, (c + 1) * CHUNK)
    qt, kt = qt[sl], kt[sl]
    vb = v_ref[sl, :].astype(BF16)
    att = jnp.where(causal, _dot_nt(_stack_heads(qt.astype(BF16), hms), kt.astype(BF16)), 0.0)
    st = st_ref[...]
    o = _dot_nt((qt * jnp.exp(cum_mid[c])).astype(BF16), st.astype(BF16))
    for h in range(4):
        o = o + _dot(att[64 * h:64 * (h + 1)].astype(BF16), jnp.where(hms[h], vb, jnp.zeros_like(vb)))
    o_ref[sl, :] = o
    _state_update(st_ref, st, jnp.exp(cum_last[c]), vb, (kt * jnp.exp(cum_last[c] - cum_mid[c])).astype(BF16))


def _hgrn_kernel(qf_ref, ff_ref, vf_ref, qb_ref, fb_ref, vb_ref, gl_ref, ball_ref, lmask_ref, bblk_ref,
                 of_ref, ob_ref, sf_ref, sb_ref, sf0_ref, sb0_ref, *, lb_zero):
    @pl.when(pl.program_id(1) == 0)
    def _():
        sf_ref[...] = jnp.zeros_like(sf_ref)
        sb_ref[...] = jnp.zeros_like(sb_ref)

    sf0_ref[...] = sf_ref[...]
    sb0_ref[...] = sb_ref[...]
    dirs = ((qf_ref, ff_ref, vf_ref, sf_ref, of_ref, False), (qb_ref, fb_ref, vb_ref, sb_ref, ob_ref, True))
    rng, pre = 0.0, []
    for d, (q_ref, f_ref, v_ref, st_ref, o_ref, rev) in enumerate(dirs):
        q, kk, logf = _hgrn_gates(q_ref[...], f_ref[...], gl_ref[d], lb_zero)
        cum = _dot01(bblk_ref[d], logf)
        mid_row, last_row = (CHUNK // 2, 0) if rev else (CHUNK // 2 - 1, CHUNK - 1)
        cum_mid = [cum[c * CHUNK + mid_row:c * CHUNK + mid_row + 1] for c in range(NCHUNK)]
        cum_last = [cum[c * CHUNK + last_row:c * CHUNK + last_row + 1] for c in range(NCHUNK)]
        a = jnp.concatenate([cum[c * CHUNK:(c + 1) * CHUNK] - cum_mid[c] for c in range(NCHUNK)], axis=0)
        rng = jnp.maximum(rng, jnp.max(jnp.abs(a)))
        pre.append((q * jnp.exp(a), kk * jnp.exp(-a), cum_mid, cum_last))
    for step in range(NCHUNK):
        for d, (q_ref, f_ref, v_ref, st_ref, o_ref, rev) in enumerate(dirs):
            qt, kt, cum_mid, cum_last = pre[d]
            c = NCHUNK - 1 - step if rev else step
            _hgrn_chunk_fast(c, qt, kt, cum_mid, cum_last, v_ref, st_ref, o_ref, rev)

    @pl.when(jnp.logical_not(rng <= HGRN_FAST_RANGE))
    def _():
        sf_ref[...] = sf0_ref[...]
        sb_ref[...] = sb0_ref[...]

        def body(c, carry):
            rf = pl.ds(pl.multiple_of(c * CHUNK, CHUNK), CHUNK)
            rb = pl.ds(pl.multiple_of((NCHUNK - 1 - c) * CHUNK, CHUNK), CHUNK)
            of_ref[rf, :] = _hgrn_chunk_exact(qf_ref[rf, :], ff_ref[rf, :], vf_ref[rf, :], sf_ref,
                                              gl_ref[0], ball_ref[0], lmask_ref, 0, False, lb_zero)
            ob_ref[rb, :] = _hgrn_chunk_exact(qb_ref[rb, :], fb_ref[rb, :], vb_ref[rb, :], sb_ref,
                                              gl_ref[1], ball_ref[1], lmask_ref, 1, True, lb_zero)
            return carry

        lax.fori_loop(0, NCHUNK, body, 0)


def _hgrn_call(u_f32, gl, ball, lmask, bblk, lb_zero):
    b = u_f32.shape[0]

    def fwd(col):
        return pl.BlockSpec((None, TB, 256), lambda b, i: (b, jnp.where(i == 0, CTX_BLK, i - 1), col))

    def bwd(col):
        return pl.BlockSpec((None, TB, 256), lambda b, i: (b, jnp.where(i == 0, CTX_BLK, CTX_BLK - i), col))

    return pl.pallas_call(
        functools.partial(_hgrn_kernel, lb_zero=lb_zero),
        grid=(b, NBLK),
        in_specs=[fwd(1), fwd(2), fwd(4), bwd(1), bwd(3), bwd(4),
                  _const_spec((2, 2, 256)), _const_spec((2, 8 * CHUNK, CHUNK)),
                  _const_spec((2, 7, 4 * CHUNK, CHUNK)), _const_spec((2, TB, TB))],
        out_specs=[fwd(0), bwd(0)],
        out_shape=[jax.ShapeDtypeStruct((b, T_ALL, 256), F32)] * 2,
        scratch_shapes=[pltpu.VMEM((256, 256), F32)] * 4,
        compiler_params=_cparams(("arbitrary", "arbitrary")),
        name="hgrn_scan",
    )(u_f32, u_f32, u_f32, u_f32, u_f32, u_f32, gl, ball, lmask, bblk)


def _outproj_kernel(*refs, nx, with_ctx):
    (oa_ref, p_ref, q_ref, oc_ref, of_ref, ob_ref, gd_ref, hg_ref, hh_ref, cs_ref,
     wf_ref, wo_ref, gp_ref, gtc_ref, gtb_ref, o_ref) = refs[nx:]
    for part, (r0, r1) in enumerate(_row_parts(o_ref.shape[0], with_ctx)):
        rows = slice(r0, r1)
        z = _dot(p_ref[rows, :], cs_ref[0:256, :]) + _dot(q_ref[rows, :], cs_ref[256:512, :])
        o_b = _dot(z.astype(BF16), wf_ref[...])
        od = of_ref[rows, :] + ob_ref[rows, :]
        sq = od * od
        hi = sq.astype(BF16)
        lo = (sq - hi.astype(F32)).astype(BF16)
        ms = (_dot(hi, hh_ref[...]) + _dot(lo, hh_ref[...])) * (1.0 / HEAD_DIM)
        gd = gd_ref[rows, :]
        o_d = od * lax.rsqrt(ms + EPS) * hg_ref[...] * (gd * jax.nn.sigmoid(gd))
        y = (_dot(oa_ref[rows, :], wo_ref[0:256, :]) + _dot(o_b.astype(BF16), wo_ref[256:512, :])
             + _dot(oc_ref[rows, :], wo_ref[512:768, :]) + _dot(o_d.astype(BF16), wo_ref[768:1024, :]))
        x = _x_part(refs[:nx], part, r0, r1)
        o_ref[rows, :] = _gated_residual(x, y, gp_ref, (gtc_ref, gtb_ref), part, with_ctx)


def _outproj_call(x_parts, o_a, p, q, o_c, o_f, o_bk, u_f32, hg, hh, cs, w_fnet, w_out, l, g_post, mods, with_ctx):
    b = x_parts[0].shape[0]
    tm, t = (TM_ALL, T_ALL) if with_ctx else (TM_LAT, SEQ)
    col = lambda c: pl.BlockSpec((None, tm, 256), lambda b, i: (b, i, c))
    split = len(x_parts) == 2
    x_args = (x_parts[0],) * 3 + (x_parts[1],) if split else x_parts
    x_specs = _x_specs(True) if split else [pl.BlockSpec((None, tm, D_MODEL), lambda b, i: (b, i, 0))]
    return pl.pallas_call(
        functools.partial(_outproj_kernel, nx=len(x_args), with_ctx=with_ctx),
        grid=(b, t // tm),
        in_specs=[
            *x_specs,
            col(0), col(0), col(0), col(0), col(0), col(0), col(5),
            _const_spec((1, 256)), _const_spec((256, 256)), _const_spec((512, 256)),
            _layer_spec((256, 256), l), _layer_spec((D_MODEL, D_MODEL), l), _const_spec((1, D_MODEL)),
            *_mod_specs(2),
        ],
        out_specs=pl.BlockSpec((None, tm, D_MODEL), lambda b, i: (b, i, 0)),
        out_shape=jax.ShapeDtypeStruct((b, t, D_MODEL), F32),
        compiler_params=_cparams(("arbitrary", "arbitrary")),
        name="out_proj",
    )(*x_args, o_a, p, q, o_c, o_f, o_bk, u_f32, hg, hh, cs, w_fnet, w_out, g_post, mods, mods)


def _ffn_kernel(x_ref, g_ref, shc_ref, shb_ref, scc_ref, scb_ref, w1_ref, w2_ref, gp_ref, gtc_ref, gtb_ref,
                o_ref, *, with_ctx):
    fc = 256

    def swiglu(h):
        acc = jnp.zeros((h.shape[0], D_MODEL), F32)
        for c in range(D_FF // fc):
            a = _dot(h, w1_ref[:, c * fc:(c + 1) * fc])
            g = _dot(h, w1_ref[:, D_FF + c * fc:D_FF + (c + 1) * fc])
            act = (a * jax.nn.sigmoid(a) * g).astype(BF16)
            acc = acc + _dot(act, w2_ref[c * fc:(c + 1) * fc, :])
        return acc

    parts = list(enumerate(_row_parts(x_ref.shape[0], with_ctx)))
    pre = lambda part, r0, r1: _modulated(x_ref[r0:r1, :], g_ref, (shc_ref, shb_ref), (scc_ref, scb_ref), part, with_ctx)
    post = lambda part, r0, r1, acc: _gated_residual(x_ref[r0:r1, :], acc, gp_ref, (gtc_ref, gtb_ref), part, with_ctx)
    if with_ctx:
        acc = swiglu(jnp.concatenate([pre(part, r0, r1) for part, (r0, r1) in parts], axis=0))
        for part, (r0, r1) in parts:
            o_ref[r0:r1, :] = post(part, r0, r1, acc[r0:r1])
    else:
        for part, (r0, r1) in parts:
            o_ref[r0:r1, :] = post(part, r0, r1, swiglu(pre(part, r0, r1)))


def _ffn_call(xs, g_pre, w1, w2, l, g_post, mods, with_ctx):
    b, t, _ = xs.shape
    tm = TM_ALL if with_ctx else TM_LAT
    return pl.pallas_call(
        functools.partial(_ffn_kernel, with_ctx=with_ctx),
        grid=(b, t // tm),
        in_specs=[
            pl.BlockSpec((None, tm, D_MODEL), lambda b, i: (b, i, 0)),
            _const_spec((1, D_MODEL)),
            *_mod_specs(3), *_mod_specs(4),
            _layer_spec((D_MODEL, 2 * D_FF), l), _layer_spec((D_FF, D_MODEL), l), _const_spec((1, D_MODEL)),
            *_mod_specs(5),
        ],
        out_specs=pl.BlockSpec((None, tm, D_MODEL), lambda b, i: (b, i, 0)),
        out_shape=jax.ShapeDtypeStruct(xs.shape, F32),
        compiler_params=_cparams(("arbitrary", "arbitrary")),
        name="ffn",
    )(xs, g_pre, mods, mods, mods, mods, w1, w2, g_post, mods, mods)


def kernel(x, c, ctx, c_ctx, w_ada, b_ada, g_pre_mix, g_post_mix, g_pre_ffn, g_post_ffn, w_in, w_out,
           na_rpb, w_fnet, swa_sink, hgrn_lb_logits, hgrn_norm_g, w_ffn_in, w_ffn_out):
    tb = _tables()
    bf = lambda name: jnp.asarray(tb[name]).astype(BF16)
    cos, sin = jnp.asarray(tb["rope_cos"]), jnp.asarray(tb["rope_sin"])
    f64, g2, fc, chan_cs, hh = bf("f64"), bf("g_stage2"), bf("f_ctx"), bf("chan_cs"), bf("head_ones")
    ball, bblk, lmask = bf("hgrn_b"), bf("hgrn_bblk"), jnp.asarray(tb["hgrn_mask"])
    na_mask, swa_mask = jnp.asarray(tb["na_mask"]), jnp.asarray(tb["swa_mask"])

    cc = jnp.concatenate([c, c_ctx[None, :]], axis=0)
    mods = _ada_call(jnp.broadcast_to(cc[:, :, None], (3, D_MODEL, 128)), w_ada, b_ada)

    lb = jnp.cumsum(jax.nn.softmax(hgrn_lb_logits.astype(F32), axis=1), axis=1)
    lb = lb - lb[:, :1]
    gl_all = jnp.stack([jnp.log(lb), jnp.log1p(-lb)], axis=2)

    w_in_b, w_out_b, w_fnet_b = w_in.astype(BF16), w_out.astype(BF16), w_fnet.astype(BF16)
    w_ffn_in_b, w_ffn_out_b = w_ffn_in.astype(BF16), w_ffn_out.astype(BF16)

    xs = None
    for l in range(DEPTH):
        with_ctx = l < DEPTH - 1
        row = lambda a: a[l][None, :]
        x_all = (x, ctx) if l == 0 else (xs,)
        x_res = x_all if with_ctx else x_all[:1]
        u_att, u_f32 = _proj_call(x_all, row(g_pre_mix), mods[l], w_in_b, l, cos, sin)

        rpb = jnp.pad(na_rpb[l], ((0, 0), (NA_DR_PAD, NA_DR_ROWS - NA_DR_PAD - (2 * NA_WIN_R - 1)), (0, 0)))
        t2 = _na_bias_call(rpb[:, :, tb["na_lane_l"]], rpb[:, :, tb["na_lane_r"]])
        o_a = _na_call(u_att, na_mask, t2, with_ctx)
        o_c = _swa_call(u_att, swa_sink[l], swa_mask, with_ctx)
        p, q = _fnet_call(u_f32, f64, g2, fc)
        o_f, o_bk = _hgrn_call(u_f32, gl_all[:, l], ball, lmask, bblk, lb_zero=(l == 0))

        xs = _outproj_call(x_res, o_a, p, q, o_c, o_f, o_bk, u_f32, row(hgrn_norm_g), hh, chan_cs,
                           w_fnet_b, w_out_b, l, row(g_post_mix), mods[l], with_ctx)
        xs = _ffn_call(xs, row(g_pre_ffn), w_ffn_in_b, w_ffn_out_b, l, row(g_post_ffn), mods[l], with_ctx)
    return xs
```

```python
import functools

import numpy as np
import jax
import jax.numpy as jnp
from jax import lax
from jax.experimental import pallas as pl
from jax.experimental.pallas import tpu as pltpu

F32, BF16 = jnp.float32, jnp.bfloat16

D_MODEL = 1024
SEQ = 8192
DEPTH = 2
GRID_W = 64
GRID_ROWS = SEQ // GRID_W
CTX = 256
HEAD_DIM = 64
NA_WIN_R, NA_WIN_C = 8, 16
SWA_WINDOW = 128
ROPE_THETA = 10000.0
EPS = 1e-6
D_FF = 2816
D_IN = 2816

TB = 256
T_ALL = SEQ + CTX
NBLK = T_ALL // TB
CTX_BLK = SEQ // TB
TM_ALL = 768
TM_LAT = 1024
NA_KROWS = 12
NA_KEYS = NA_KROWS * GRID_W
NA_DR_PAD = 4
NA_DR_ROWS = 24
SWA_KEYS = TB + 2 * SWA_WINDOW
CHUNK = 64
NCHUNK = TB // CHUNK
HGRN_FAST_RANGE = 60.0
Q_SCALE = HEAD_DIM ** -0.5
VMEM_LIMIT = 56 * 1024 * 1024

ATT_W, F32_W = 1280, 1536


def _dot(a, b):
    return jnp.dot(a, b, preferred_element_type=F32)


def _dot_nt(a, b):
    return lax.dot_general(a, b, (((1,), (1,)), ((), ())), preferred_element_type=F32)


def _dot_tn(a, b):
    return lax.dot_general(a, b, (((0,), (0,)), ((), ())), preferred_element_type=F32)


def _cparams(sem):
    return pltpu.CompilerParams(dimension_semantics=sem, vmem_limit_bytes=VMEM_LIMIT)


def _const_spec(shape):
    nd = len(shape)
    return pl.BlockSpec(shape, lambda *_: (0,) * nd, pipeline_mode=pl.Buffered(1))


def _layer_spec(shape, l):
    nd = len(shape)
    return pl.BlockSpec((None,) + shape, lambda *_: (l,) + (0,) * nd, pipeline_mode=pl.Buffered(1))


@functools.lru_cache(maxsize=None)
def _tables():
    t = {}
    lane = np.arange(128)
    d = lane % 64
    freq = ROPE_THETA ** (-(d % 16) / 16.0)
    pos = np.arange(SEQ)
    p = np.where(d[None, :] < 32, (pos // GRID_W)[:, None], (pos % GRID_W)[:, None]).astype(np.float64)
    ang = p * freq[None, :]
    sign = np.where((d % 32) < 16, -1.0, 1.0)
    cos = np.concatenate([np.cos(ang), np.ones((CTX, 128))], axis=0)
    sin = np.concatenate([np.sin(ang) * sign[None, :], np.zeros((CTX, 128))], axis=0)
    t["rope_cos"], t["rope_sin"] = cos.astype(np.float32), sin.astype(np.float32)

    masks = []
    a, cq = np.arange(4), np.arange(GRID_W)
    rho, ck = np.arange(NA_KROWS), np.arange(GRID_W)
    for r0 in (0, 8, GRID_ROWS - 4):
        kr0 = int(np.clip(r0 - 4, 0, GRID_ROWS - NA_KROWS))
        rq, rk = r0 + a, kr0 + rho
        start = np.clip(rq - NA_WIN_R // 2, 0, GRID_ROWS - NA_WIN_R)
        vr = (rk[None, :] >= start[:, None]) & (rk[None, :] < start[:, None] + NA_WIN_R)
        cs = np.clip(cq - NA_WIN_C // 2, 0, GRID_W - NA_WIN_C)
        vc = (ck[None, :] >= cs[:, None]) & (ck[None, :] < cs[:, None] + NA_WIN_C)
        val = vr[:, None, :, None] & vc[None, :, None, :]
        masks.append(np.where(val, 0.0, -np.inf).reshape(TB, NA_KEYS))
    masks.append(np.full((TB, NA_KEYS), -np.inf))
    t["na_mask"] = np.stack(masks).astype(np.float32)
    n = np.arange(128)
    clipc = lambda m: np.clip(m, -(NA_WIN_C - 1), NA_WIN_C - 1) + NA_WIN_C - 1
    t["na_lane_l"] = clipc(np.where(n < 64, n, n - 128)).astype(np.int32)
    t["na_lane_r"] = clipc(n - 64).astype(np.int32)

    rel = np.arange(SWA_KEYS)[None, :] - np.arange(TB)[:, None]
    swa = [np.where(np.abs(rel + off) <= SWA_WINDOW, 0.0, -np.inf) for off in (0, -SWA_WINDOW, -2 * SWA_WINDOW)]
    swa.append(np.full((TB, SWA_KEYS), -np.inf))
    t["swa_mask"] = np.stack(swa).astype(np.float32)

    def cs_tab(num, den):
        ang = 2.0 * np.pi * (num % den) / den
        return np.cos(ang), np.sin(ang)

    k1, t1 = np.arange(64)[:, None], np.arange(64)[None, :]
    c64, s64 = cs_tab(k1 * t1, 64)
    t["f64"] = np.concatenate([c64, -s64], axis=0).astype(np.float32)
    k2, t2 = np.arange(128)[:, None], np.arange(128)[None, :]
    g = []
    for kk in range(64):
        gc, gs = cs_tab((64 * k2 + kk) * t2, SEQ)
        g.append(np.block([[gc, gs], [-gs, gc]]))
    t["g_stage2"] = np.stack(g).astype(np.float32)
    kc, tc = np.arange(CTX)[:, None], np.arange(CTX)[None, :]
    cc, sc = cs_tab(kc * tc, CTX)
    t["f_ctx"] = np.concatenate([cc, -sc], axis=0).astype(np.float32)
    ch = np.arange(256)
    same = (ch[:, None] // 64) == (ch[None, :] // 64)
    c_ch, s_ch = cs_tab((ch[:, None] % 64) * (ch[None, :] % 64), 64)
    t["chan_cs"] = np.concatenate([np.where(same, c_ch, 0.0), np.where(same, s_ch, 0.0)], axis=0).astype(np.float32)
    t["head_ones"] = same.astype(np.float32)

    balls, lmasks, bblk = [], [], []
    for rev in (False, True):
        tau = (CHUNK - 1 - np.arange(CHUNK)) if rev else np.arange(CHUNK)
        tt, tj = tau[:, None], tau[None, :]
        blocks = [(tj <= tt), (tj > tt)]
        masks = [tt == tj]
        for h in (32, 16, 8, 4, 2, 1):
            up = ((tt // h) % 2) == 1
            mid = (tt // (2 * h)) * (2 * h) + h - 1
            blocks.append(np.where(up, (tj > mid) & (tj <= tt), (tj > tt) & (tj <= mid)))
            up_s = ((tj // h) % 2) == 1
            masks.append((tt // (2 * h) == tj // (2 * h)) & up & ~up_s)
        balls.append(np.concatenate(blocks, axis=0))
        lmasks.append(np.stack([np.tile(m, (4, 1)) for m in masks]))
        bblk.append(np.kron(np.eye(NCHUNK), blocks[0]))
    t["hgrn_b"] = np.stack(balls).astype(np.float32)
    t["hgrn_mask"] = np.stack(lmasks).astype(np.float32)
    t["hgrn_bblk"] = np.stack(bblk).astype(np.float32)
    return t


def _ada_kernel(cb_ref, w_ref, b_ref, o_ref, s_ref):
    tn = w_ref.shape[1]
    nrep = tn // 128

    @pl.when((pl.program_id(0) == 0) & (pl.program_id(1) == 0))
    def _():
        cb = cb_ref[...]
        s_ref[...] = cb * jax.nn.sigmoid(cb)

    def body(kc, accs):
        k0 = pl.multiple_of(kc * 8, 8)
        wt = w_ref[pl.ds(k0, 8), :]
        return tuple(accs[r] + wt * jnp.concatenate([s_ref[r, pl.ds(k0, 8), :]] * nrep, axis=1) for r in range(3))

    accs = lax.fori_loop(0, D_MODEL // 8, body, tuple(jnp.zeros((8, tn), F32) for _ in range(3)), unroll=8)
    for r in range(3):
        o_ref[r] = jnp.sum(accs[r], axis=0, keepdims=True) + b_ref[...]


def _ada_call(cb, w_ada, b_ada):
    tn = 1536
    return pl.pallas_call(
        _ada_kernel,
        grid=(DEPTH, 6 * D_MODEL // tn),
        in_specs=[
            pl.BlockSpec((3, D_MODEL, 128), lambda l, j: (0, 0, 0)),
            pl.BlockSpec((None, D_MODEL, tn), lambda l, j: (l, 0, j)),
            pl.BlockSpec((None, 1, tn), lambda l, j: (l, 0, j)),
        ],
        out_specs=pl.BlockSpec((None, 3, 1, tn), lambda l, j: (l, 0, 0, j)),
        out_shape=jax.ShapeDtypeStruct((DEPTH, 3, 1, 6 * D_MODEL), F32),
        scratch_shapes=[pltpu.VMEM((3, D_MODEL, 128), F32)],
        compiler_params=_cparams(("arbitrary", "arbitrary")),
        name="adaln",
    )(cb, w_ada, b_ada.reshape(DEPTH, 1, 6 * D_MODEL))


def _mod_specs(col):
    return [pl.BlockSpec((None, 1, D_MODEL), lambda b, i: (2, 0, col)),
            pl.BlockSpec((None, 1, D_MODEL), lambda b, i: (b, 0, col))]


def _row_parts(tm, with_ctx):
    return [(0, tm - CTX), (tm - CTX, tm)] if with_ctx else [(0, tm // 2), (tm // 2, tm)]


def _mod(mc_ref, mb_ref, part, with_ctx):
    if not with_ctx or part == 0:
        return mb_ref[...]
    last = pl.program_id(1) == pl.num_programs(1) - 1
    return jnp.where(last, mc_ref[...], mb_ref[...])


def _modulated(x, g_ref, sh_refs, sc_refs, part, with_ctx):
    gs = g_ref[...] * (1.0 + _mod(*sc_refs, part, with_ctx))
    return (_rms(x) * gs + _mod(*sh_refs, part, with_ctx)).astype(BF16)


def _gated_residual(x, y, gp_ref, gt_refs, part, with_ctx):
    return x + (_mod(*gt_refs, part, with_ctx) * gp_ref[...]) * _rms(y)


def _rms(x):
    return x * lax.rsqrt(jnp.mean(x * x, axis=-1, keepdims=True) + EPS)


def _x_specs(split):
    if not split:
        return [pl.BlockSpec((None, TM_ALL, D_MODEL), lambda b, i: (b, i, 0))]
    nx = SEQ // TB
    return [pl.BlockSpec((None, TB, D_MODEL), lambda b, i, j=j: (b, jnp.minimum(3 * i + j, nx - 1), 0))
            for j in range(TM_ALL // TB)] + [pl.BlockSpec((None, CTX, D_MODEL), lambda b, i: (b, 0, 0))]


def _x_part(x_refs, part, lo, hi):
    if len(x_refs) == 1:
        return x_refs[0][lo:hi, :]
    xa, xb, xc, cx = x_refs
    if part == 0:
        return jnp.concatenate([xa[...], xb[...]], axis=0)
    last = pl.program_id(1) == pl.num_programs(1) - 1
    return jnp.where(last, cx[...], xc[...])


def _proj_kernel(*refs, nx):
    g_ref, shc_ref, shb_ref, scc_ref, scb_ref, w_ref, cos_ref, sin_ref, att_ref, f32_ref = refs[nx:]
    for part, (lo, hi) in enumerate(_row_parts(TM_ALL, True)):
        rows = slice(lo, hi)
        h = _modulated(_x_part(refs[:nx], part, lo, hi), g_ref, (shc_ref, shb_ref), (scc_ref, scb_ref), part, True)
        cos, sin = cos_ref[rows, :], sin_ref[rows, :]
        first = (lax.broadcasted_iota(jnp.int32, (hi - lo, 128), 1) % 32) < 16

        def rope(v):
            return v * cos + jnp.where(first, pltpu.roll(v, 112, 1), pltpu.roll(v, 16, 1)) * sin

        a = _dot(h, w_ref[:, 0:768])
        att_ref[rows, 0:256] = (a[:, 0:256] * Q_SCALE).astype(BF16)
        att_ref[rows, 256:768] = a[:, 256:768].astype(BF16)
        f32_ref[rows, 0:256] = _dot(h, w_ref[:, 768:1024])
        s = _dot(h, w_ref[:, 1024:1536])
        att_ref[rows, 768:896] = (rope(s[:, 0:128]) * Q_SCALE).astype(BF16)
        att_ref[rows, 896:1024] = (rope(s[:, 128:256]) * Q_SCALE).astype(BF16)
        att_ref[rows, 1024:1152] = rope(s[:, 256:384]).astype(BF16)
        att_ref[rows, 1152:1280] = s[:, 384:512].astype(BF16)
        f32_ref[rows, 256:1536] = _dot(h, w_ref[:, 1536:2816])


def _proj_call(x_parts, g, mods, w_in, l, cos, sin):
    b = x_parts[0].shape[0]
    tm = TM_ALL
    split = len(x_parts) == 2
    x_args = (x_parts[0],) * 3 + (x_parts[1],) if split else x_parts
    return pl.pallas_call(
        functools.partial(_proj_kernel, nx=len(x_args)),
        grid=(b, T_ALL // tm),
        in_specs=[
            *_x_specs(split),
            _const_spec((1, D_MODEL)),
            *_mod_specs(0), *_mod_specs(1),
            _layer_spec((D_MODEL, D_IN), l),
            pl.BlockSpec((tm, 128), lambda b, i: (i, 0)),
            pl.BlockSpec((tm, 128), lambda b, i: (i, 0)),
        ],
        out_specs=[
            pl.BlockSpec((None, tm, ATT_W), lambda b, i: (b, i, 0)),
            pl.BlockSpec((None, tm, F32_W), lambda b, i: (b, i, 0)),
        ],
        out_shape=[jax.ShapeDtypeStruct((b, T_ALL, ATT_W), BF16), jax.ShapeDtypeStruct((b, T_ALL, F32_W), F32)],
        compiler_params=_cparams(("arbitrary", "arbitrary")),
        name="proj_in",
    )(*x_args, g, mods, mods, mods, mods, w_in, cos, sin)


def _na_bias_kernel(vl_ref, vr_ref, o_ref):
    lane = lax.broadcasted_iota(jnp.int32, (GRID_W, 128), 1)
    for e in range(NA_DR_ROWS - 1):
        left = pltpu.roll(jnp.broadcast_to(vl_ref[e:e + 1, :], (GRID_W, 128)), 0, 1, stride=1, stride_axis=0)
        right = pltpu.roll(jnp.broadcast_to(vr_ref[e + 1:e + 2, :], (GRID_W, 128)), 0, 1, stride=1, stride_axis=0)
        o_ref[e] = jnp.where(lane < 64, left, right)


def _na_bias_call(vl, vr):
    spec = pl.BlockSpec((None, NA_DR_ROWS, 128), lambda h: (h, 0, 0))
    return pl.pallas_call(
        _na_bias_kernel,
        grid=(4,),
        in_specs=[spec, spec],
        out_specs=pl.BlockSpec((None, NA_DR_ROWS - 1, GRID_W, 128), lambda h: (h, 0, 0, 0)),
        out_shape=jax.ShapeDtypeStruct((4, NA_DR_ROWS - 1, GRID_W, 128), F32),
        compiler_params=_cparams(("arbitrary",)),
        name="na_bias",
    )(vl, vr)


AT_ROWS = 16


def _attn_block_variant(b, i):
    return (jnp.where(i == CTX_BLK, 3, jnp.where(i == 0, 0, jnp.where(i == CTX_BLK - 1, 2, 1))), 0, 0)


def _lane_half(shape, j):
    lane = lax.broadcasted_iota(jnp.int32, shape, 1)
    return (lane >= 64 * j) & (lane < 64 * (j + 1))


def _na_window(i):
    is_ctx = i == CTX_BLK
    r0 = i * 4
    kr0 = jnp.where(is_ctx, 0, jnp.clip(r0 - 4, 0, GRID_ROWS - NA_KROWS))
    delta = jnp.where(is_ctx, 3, kr0 - r0 + NA_WIN_R - 1) + NA_DR_PAD
    return pl.multiple_of(kr0 * GRID_W, GRID_W), delta


def _na_scores(i, q_ref, k_ref, s_ref):
    kstart, _ = _na_window(i)
    for hp in range(2):
        cs = slice(hp * 128, (hp + 1) * 128)
        q2 = q_ref[:, cs]
        kw, kc = k_ref[pl.ds(kstart, NA_KEYS), cs], k_ref[SEQ:T_ALL, cs]
        qm = jnp.concatenate([jnp.where(_lane_half((TB, 128), j), q2, jnp.zeros_like(q2)) for j in range(2)], axis=0)
        s_ref[hp, :, 0:CTX] = _dot_nt(qm, kc)
        s_ref[hp, :, CTX:] = _dot_nt(qm, kw)


def _na_softmax(i, s_ref, mask_ref, t2_ref, p_ref):
    _, delta = _na_window(i)
    for hp in range(2):
        for r in range(2 * TB // AT_ROWS):
            rows = slice(r * AT_ROWS, (r + 1) * AT_ROWS)
            j, rq = divmod(r, TB // AT_ROWS)
            a, sub = divmod(rq, GRID_W // AT_ROWS)
            trow = slice(sub * AT_ROWS, (sub + 1) * AT_ROWS)
            bias = jnp.concatenate(
                [t2_ref[2 * hp + j, delta + 2 * rp - a, trow, :] for rp in range(NA_KROWS // 2)], axis=1)
            s_nb = s_ref[hp, rows, CTX:] + bias + mask_ref[rq * AT_ROWS:(rq + 1) * AT_ROWS, :]
            s_c = s_ref[hp, rows, 0:CTX]
            m = jnp.maximum(jnp.max(s_nb, axis=-1, keepdims=True), jnp.max(s_c, axis=-1, keepdims=True))
            p_ref[hp, rows, 0:CTX] = jnp.exp((s_c - m).astype(BF16))
            p_ref[hp, rows, CTX:] = jnp.exp((s_nb - m).astype(BF16))


def _na_values(i, p_ref, v_ref, o_ref):
    kstart, _ = _na_window(i)
    lane = lax.broadcasted_iota(jnp.int32, (TB, 128), 1)
    for hp in range(2):
        cs = slice(hp * 128, (hp + 1) * 128)
        vw, vc = v_ref[pl.ds(kstart, NA_KEYS), cs], v_ref[SEQ:T_ALL, cs]
        o2 = (_dot(p_ref[hp, :, 0:CTX], jnp.concatenate([vc, jnp.ones_like(vc)], axis=1))
              + _dot(p_ref[hp, :, CTX:], jnp.concatenate([vw, jnp.ones_like(vw)], axis=1)))
        top = o2[0:TB, 0:128] * (1.0 / o2[0:TB, 128:256])
        bot = o2[TB:2 * TB, 0:128] * (1.0 / o2[TB:2 * TB, 128:256])
        o_ref[:, cs] = jnp.where(lane < 64, top, bot).astype(BF16)


def _swa_window(i):
    return pl.multiple_of(jnp.clip(i * TB - SWA_WINDOW, 0, SEQ - SWA_KEYS), SWA_WINDOW)


def _swa_scores(i, q_ref, k_ref, s_ref):
    kw, kc = k_ref[pl.ds(_swa_window(i), SWA_KEYS), :], k_ref[SEQ:T_ALL, :]
    for g in range(2):
        q2 = q_ref[:, g * 128:(g + 1) * 128].astype(F32)
        qa = jnp.concatenate([jnp.where(_lane_half((TB, 128), g), q2 if j == g else pltpu.roll(q2, 64, 1), 0.0)
                              for j in range(2)], axis=0).astype(BF16)
        s_ref[g, :, 0:CTX] = _dot_nt(qa, kc)
        s_ref[g, :, CTX:] = _dot_nt(qa, kw)


def _swa_softmax(sink_ref, s_ref, mask_ref, p_ref, esink_ref):
    for g in range(2):
        for r in range(2 * TB // AT_ROWS):
            rows = slice(r * AT_ROWS, (r + 1) * AT_ROWS)
            j, rq = divmod(r, TB // AT_ROWS)
            sink = sink_ref[2 * g + j]
            s_l = s_ref[g, rows, CTX:] + mask_ref[rq * AT_ROWS:(rq + 1) * AT_ROWS, :]
            s_c = s_ref[g, rows, 0:CTX]
            m = jnp.maximum(jnp.maximum(jnp.max(s_l, axis=-1, keepdims=True),
                                        jnp.max(s_c, axis=-1, keepdims=True)), sink)
            p_ref[g, rows, 0:CTX] = jnp.exp((s_c - m).astype(BF16))
            p_ref[g, rows, CTX:] = jnp.exp((s_l - m).astype(BF16))
            esink_ref[g, rows, :] = jnp.exp(jnp.broadcast_to(sink - m, (AT_ROWS, 128)))


def _swa_values(i, p_ref, esink_ref, v_ref, o_ref):
    vw, vc = v_ref[pl.ds(_swa_window(i), SWA_KEYS), :], v_ref[SEQ:T_ALL, :]
    lane = lax.broadcasted_iota(jnp.int32, (TB, 128), 1)
    vc2 = jnp.concatenate([vc, jnp.ones_like(vc)], axis=1)
    vw2 = jnp.concatenate([vw, jnp.ones_like(vw)], axis=1)
    for g in range(2):
        o2 = _dot(p_ref[g, :, 0:CTX], vc2) + _dot(p_ref[g, :, CTX:], vw2)
        o = o2[:, 0:128] * (1.0 / (o2[:, 128:256] + esink_ref[g]))
        outs = [o[j * TB:(j + 1) * TB] if j == g else pltpu.roll(o[j * TB:(j + 1) * TB], 64, 1) for j in range(2)]
        o_ref[:, g * 128:(g + 1) * 128] = jnp.where(lane < 64, outs[0], outs[1]).astype(BF16)


def _attn_kernel(sink_ref, qn_ref, kn_ref, vn_ref, maskn_ref, t2_ref, qs_ref, ks_ref, vs_ref, masks_ref,
                 on_ref, os_ref, sn_ref, pn_ref, ss_ref, ps_ref, esink_ref):
    i = pl.program_id(1)
    _na_scores(i, qn_ref, kn_ref, sn_ref)
    _swa_scores(i, qs_ref, ks_ref, ss_ref)
    _na_softmax(i, sn_ref, maskn_ref, t2_ref, pn_ref)
    _swa_softmax(sink_ref, ss_ref, masks_ref, ps_ref, esink_ref)
    _na_values(i, pn_ref, vn_ref, on_ref)
    _swa_values(i, ps_ref, esink_ref, vs_ref, os_ref)


def _attn_call(u_att, sink, na_mask, t2, swa_mask, with_ctx):
    b = u_att.shape[0]
    nb = NBLK if with_ctx else CTX_BLK
    blk = lambda width, col: pl.BlockSpec((None, TB, width), lambda b, i: (b, i, col))
    full = lambda width, col: pl.BlockSpec((None, T_ALL, width), lambda b, i: (b, 0, col))
    out = pl.BlockSpec((None, TB, 256), lambda b, i: (b, i, 0))
    return pl.pallas_call(
        _attn_kernel,
        grid=(b, nb),
        in_specs=[
            pl.BlockSpec(memory_space=pltpu.SMEM),
            blk(256, 0), full(256, 1), full(256, 2),
            pl.BlockSpec((None, TB, NA_KEYS), _attn_block_variant),
            _const_spec((4, NA_DR_ROWS - 1, GRID_W, 128)),
            blk(256, 3), full(128, 8), full(128, 9),
            pl.BlockSpec((None, TB, SWA_KEYS), _attn_block_variant),
        ],
        out_specs=[out, out],
        out_shape=[jax.ShapeDtypeStruct((b, nb * TB, 256), BF16)] * 2,
        scratch_shapes=[pltpu.VMEM((2, 2 * TB, CTX + NA_KEYS), F32), pltpu.VMEM((2, 2 * TB, CTX + NA_KEYS), BF16),
                        pltpu.VMEM((2, 2 * TB, CTX + SWA_KEYS), F32), pltpu.VMEM((2, 2 * TB, CTX + SWA_KEYS), BF16),
                        pltpu.VMEM((2, 2 * TB, 128), F32)],
        compiler_params=_cparams(("arbitrary", "arbitrary")),
        name="attn",
    )(sink, u_att, u_att, u_att, na_mask, t2, u_att, u_att, u_att, swa_mask)


FN_XP = 136
FN_YP = 136
FN_ZP = 72


def _fnet_kernel(x_ref, f64_ref, g_ref, fc_ref, p_ref, q_ref, xz_ref, zq_ref, y_ref):
    sc_lat, sc_ctx = (64.0 * SEQ) ** -0.5, (64.0 * CTX) ** -0.5
    pq = _dot(fc_ref[...], x_ref[SEQ:T_ALL, :].astype(BF16))
    p_ref[SEQ:T_ALL, :] = (pq[0:CTX] * sc_ctx).astype(BF16)
    q_ref[SEQ:T_ALL, :] = (pq[CTX:2 * CTX] * sc_ctx).astype(BF16)
    f64 = f64_ref[...]

    def repitch(t1, carry):
        xz_ref[pl.ds(pl.multiple_of(t1 * FN_XP, 8), 128), :] = x_ref[pl.ds(pl.multiple_of(t1 * 128, 128), 128), :]
        return carry

    lax.fori_loop(0, 64, repitch, 0, unroll=4)

    def stage1(t2, carry):
        xs = xz_ref[pl.ds(t2, 64, stride=FN_XP), :].astype(BF16)
        y_ref[pl.ds(t2, 128, stride=FN_YP), :] = _dot(f64, xs)
        return carry

    lax.fori_loop(0, 128, stage1, 0, unroll=8)

    def stage2(k1, carry):
        yre = y_ref[pl.ds(pl.multiple_of(k1 * FN_YP, 8), 128), :]
        yim = y_ref[pl.ds(pl.multiple_of((64 + k1) * FN_YP, 8), 128), :]
        ycat = jnp.concatenate([yre, yim], axis=0).astype(BF16)
        pq = _dot(g_ref[k1], ycat)
        xz_ref[pl.ds(k1, 128, stride=FN_ZP), :] = pq[0:128] * sc_lat
        zq_ref[pl.ds(k1, 128, stride=FN_ZP), :] = pq[128:256] * sc_lat
        return carry

    lax.fori_loop(0, 64, stage2, 0, unroll=4)

    def emit(k2, carry):
        src = pl.ds(pl.multiple_of(k2 * FN_ZP, 8), 64)
        dst = pl.ds(pl.multiple_of(k2 * 64, 64), 64)
        p_ref[dst, :] = xz_ref[src, :].astype(BF16)
        q_ref[dst, :] = zq_ref[src, :].astype(BF16)
        return carry

    lax.fori_loop(0, 128, emit, 0, unroll=4)


def _fnet_call(u_f32, f64, g2, fc):
    b = u_f32.shape[0]
    spec = pl.BlockSpec((None, T_ALL, 128), lambda b, j: (b, 0, j))
    return pl.pallas_call(
        _fnet_kernel,
        grid=(b, 2),
        in_specs=[spec, _const_spec((128, 64)), _const_spec((64, 256, 256)), _const_spec((2 * CTX, CTX))],
        out_specs=[spec, spec],
        out_shape=[jax.ShapeDtypeStruct((b, T_ALL, 256), BF16)] * 2,
        scratch_shapes=[pltpu.VMEM((128 * FN_ZP, 128), F32), pltpu.VMEM((128 * FN_ZP, 128), F32),
                        pltpu.VMEM((128 * FN_YP, 128), F32)],
        compiler_params=_cparams(("arbitrary", "arbitrary")),
        name="fnet_dft",
    )(u_f32, f64, g2, fc)


def _hgrn_gates(qd, fz, gl, lb_zero):
    q = qd * jax.nn.sigmoid(qd)
    sp = jnp.log(1.0 + jnp.exp(-jnp.abs(fz)))
    ls = jnp.minimum(fz, 0.0) - sp
    lk = jnp.minimum(-fz, 0.0) - sp
    if lb_zero:
        return q, jnp.exp(lk), ls
    b_ = gl[1:2] + ls
    a_ = gl[0:1]
    logf = jnp.maximum(a_, b_) + jnp.log(1.0 + jnp.exp(-jnp.abs(a_ - b_)))
    return q, jnp.exp(gl[1:2] + lk), logf


def _dot01(b01, x):
    h1 = x.astype(BF16)
    r1 = x - h1.astype(F32)
    h2 = r1.astype(BF16)
    h3 = (r1 - h2.astype(F32)).astype(BF16)
    return _dot(b01, h1) + _dot(b01, h2) + _dot(b01, h3)


def _head_masks(rows):
    lane = lax.broadcasted_iota(jnp.int32, (rows, 256), 1)
    return [(lane >= 64 * h) & (lane < 64 * (h + 1)) for h in range(4)]


def _stack_heads(w, hms):
    return jnp.concatenate([jnp.where(hm, w, jnp.zeros_like(w)) for hm in hms], axis=0)


def _state_update(st_ref, st, e_last, vb, ks):
    ktv = _dot_tn(vb, ks)
    r2 = lax.broadcasted_iota(jnp.int32, (256, 256), 0) // 64
    c2 = lax.broadcasted_iota(jnp.int32, (256, 256), 1) // 64
    st_ref[...] = st * e_last + jnp.where(r2 == c2, ktv, 0.0)


def _hgrn_chunk_exact(qd, fz, v, st_ref, gl, ball, lmask_ref, d, rev, lb_zero):
    q, kk, logf = _hgrn_gates(qd, fz, gl, lb_zero)
    e = jnp.exp(_dot01(ball, logf))
    row = lax.broadcasted_iota(jnp.int32, (CHUNK, 256), 0)
    tau = (CHUNK - 1 - row) if rev else row
    hms = _head_masks(CHUNK)
    att = _dot_nt(_stack_heads(q.astype(BF16), hms), kk.astype(BF16)) * lmask_ref[d, 0]
    for li, sh in enumerate((5, 4, 3, 2, 1, 0)):
        up = ((tau >> sh) & 1) == 1
        w = (jnp.where(up, q, kk) * e[128 + 64 * li:192 + 64 * li]).astype(BF16)
        att = att + _dot_nt(_stack_heads(w, hms), w) * lmask_ref[d, 1 + li]
    vb = v.astype(BF16)
    st = st_ref[...]
    o = _dot_nt((q * e[0:64]).astype(BF16), st.astype(BF16))
    for h in range(4):
        o = o + _dot(att[64 * h:64 * (h + 1)].astype(BF16), jnp.where(hms[h], vb, jnp.zeros_like(vb)))
    tl = 0 if rev else CHUNK - 1
    _state_update(st_ref, st, e[tl:tl + 1], vb, (kk * e[64:128]).astype(BF16))
    return o


def _hgrn_chunk_fast(c, qt, kt, cum_mid, cum_last, v_ref, st_ref, o_ref, rev):
    hms = _head_masks(CHUNK)
    r = lax.broadcasted_iota(jnp.int32, (4 * CHUNK, CHUNK), 0) & (CHUNK - 1)
    s = lax.broadcasted_iota(jnp.int32, (4 * CHUNK, CHUNK), 1)
    causal = (s >= r) if rev else (s <= r)
    sl = slice(c * CHUNK, (c + 1) * CHUNK)
    qt, kt = qt[sl], kt[sl]
    vb = v_ref[sl, :].astype(BF16)
    att = jnp.where(causal, _dot_nt(_stack_heads(qt.astype(BF16), hms), kt.astype(BF16)), 0.0)
    st = st_ref[...]
    o = _dot_nt((qt * jnp.exp(cum_mid[c])).astype(BF16), st.astype(BF16))
    for h in range(4):
        o = o + _dot(att[64 * h:64 * (h + 1)].astype(BF16), jnp.where(hms[h], vb, jnp.zeros_like(vb)))
    o_ref[sl, :] = o
    _state_update(st_ref, st, jnp.exp(cum_last[c]), vb, (kt * jnp.exp(cum_last[c] - cum_mid[c])).astype(BF16))


def _hgrn_kernel(qf_ref, ff_ref, vf_ref, qb_ref, fb_ref, vb_ref, gl_ref, ball_ref, lmask_ref, bblk_ref,
                 of_ref, ob_ref, sf_ref, sb_ref, sf0_ref, sb0_ref, *, lb_zero):
    @pl.when(pl.program_id(1) == 0)
    def _():
        sf_ref[...] = jnp.zeros_like(sf_ref)
        sb_ref[...] = jnp.zeros_like(sb_ref)

    sf0_ref[...] = sf_ref[...]
    sb0_ref[...] = sb_ref[...]
    dirs = ((qf_ref, ff_ref, vf_ref, sf_ref, of_ref, False), (qb_ref, fb_ref, vb_ref, sb_ref, ob_ref, True))
    rng, pre = 0.0, []
    for d, (q_ref, f_ref, v_ref, st_ref, o_ref, rev) in enumerate(dirs):
        q, kk, logf = _hgrn_gates(q_ref[...], f_ref[...], gl_ref[d], lb_zero)
        cum = _dot01(bblk_ref[d], logf)
        mid_row, last_row = (CHUNK // 2, 0) if rev else (CHUNK // 2 - 1, CHUNK - 1)
        cum_mid = [cum[c * CHUNK + mid_row:c * CHUNK + mid_row + 1] for c in range(NCHUNK)]
        cum_last = [cum[c * CHUNK + last_row:c * CHUNK + last_row + 1] for c in range(NCHUNK)]
        a = jnp.concatenate([cum[c * CHUNK:(c + 1) * CHUNK] - cum_mid[c] for c in range(NCHUNK)], axis=0)
        rng = jnp.maximum(rng, jnp.max(jnp.abs(a)))
        pre.append((q * jnp.exp(a), kk * jnp.exp(-a), cum_mid, cum_last))
    for step in range(NCHUNK):
        for d, (q_ref, f_ref, v_ref, st_ref, o_ref, rev) in enumerate(dirs):
            qt, kt, cum_mid, cum_last = pre[d]
            c = NCHUNK - 1 - step if rev else step
            _hgrn_chunk_fast(c, qt, kt, cum_mid, cum_last, v_ref, st_ref, o_ref, rev)

    @pl.when(jnp.logical_not(rng <= HGRN_FAST_RANGE))
    def _():
        sf_ref[...] = sf0_ref[...]
        sb_ref[...] = sb0_ref[...]

        def body(c, carry):
            rf = pl.ds(pl.multiple_of(c * CHUNK, CHUNK), CHUNK)
            rb = pl.ds(pl.multiple_of((NCHUNK - 1 - c) * CHUNK, CHUNK), CHUNK)
            of_ref[rf, :] = _hgrn_chunk_exact(qf_ref[rf, :], ff_ref[rf, :], vf_ref[rf, :], sf_ref,
                                              gl_ref[0], ball_ref[0], lmask_ref, 0, False, lb_zero)
            ob_ref[rb, :] = _hgrn_chunk_exact(qb_ref[rb, :], fb_ref[rb, :], vb_ref[rb, :], sb_ref,
                                              gl_ref[1], ball_ref[1], lmask_ref, 1, True, lb_zero)
            return carry

        lax.fori_loop(0, NCHUNK, body, 0)


def _hgrn_call(u_f32, gl, ball, lmask, bblk, lb_zero):
    b = u_f32.shape[0]

    def fwd(col):
        return pl.BlockSpec((None, TB, 256), lambda b, i: (b, jnp.where(i == 0, CTX_BLK, i - 1), col))

    def bwd(col):
        return pl.BlockSpec((None, TB, 256), lambda b, i: (b, jnp.where(i == 0, CTX_BLK, CTX_BLK - i), col))

    return pl.pallas_call(
        functools.partial(_hgrn_kernel, lb_zero=lb_zero),
        grid=(b, NBLK),
        in_specs=[fwd(1), fwd(2), fwd(4), bwd(1), bwd(3), bwd(4),
                  _const_spec((2, 2, 256)), _const_spec((2, 8 * CHUNK, CHUNK)),
                  _const_spec((2, 7, 4 * CHUNK, CHUNK)), _const_spec((2, TB, TB))],
        out_specs=[fwd(0), bwd(0)],
        out_shape=[jax.ShapeDtypeStruct((b, T_ALL, 256), F32)] * 2,
        scratch_shapes=[pltpu.VMEM((256, 256), F32)] * 4,
        compiler_params=_cparams(("arbitrary", "arbitrary")),
        name="hgrn_scan",
    )(u_f32, u_f32, u_f32, u_f32, u_f32, u_f32, gl, ball, lmask, bblk)


def _outproj_kernel(*refs, nx, with_ctx):
    (oa_ref, p_ref, q_ref, oc_ref, of_ref, ob_ref, gd_ref, hg_ref, hh_ref, cs_ref,
     wf_ref, wo_ref, gp_ref, gtc_ref, gtb_ref, o_ref) = refs[nx:]
    for part, (r0, r1) in enumerate(_row_parts(o_ref.shape[0], with_ctx)):
        rows = slice(r0, r1)
        z = _dot(p_ref[rows, :], cs_ref[0:256, :]) + _dot(q_ref[rows, :], cs_ref[256:512, :])
        o_b = _dot(z.astype(BF16), wf_ref[...])
        od = of_ref[rows, :] + ob_ref[rows, :]
        sq = od * od
        hi = sq.astype(BF16)
        lo = (sq - hi.astype(F32)).astype(BF16)
        ms = (_dot(hi, hh_ref[...]) + _dot(lo, hh_ref[...])) * (1.0 / HEAD_DIM)
        gd = gd_ref[rows, :]
        o_d = od * lax.rsqrt(ms + EPS) * hg_ref[...] * (gd * jax.nn.sigmoid(gd))
        y = (_dot(oa_ref[rows, :], wo_ref[0:256, :]) + _dot(o_b.astype(BF16), wo_ref[256:512, :])
             + _dot(oc_ref[rows, :], wo_ref[512:768, :]) + _dot(o_d.astype(BF16), wo_ref[768:1024, :]))
        x = _x_part(refs[:nx], part, r0, r1)
        o_ref[rows, :] = _gated_residual(x, y, gp_ref, (gtc_ref, gtb_ref), part, with_ctx)


def _outproj_call(x_parts, o_a, p, q, o_c, o_f, o_bk, u_f32, hg, hh, cs, w_fnet, w_out, l, g_post, mods, with_ctx):
    b = x_parts[0].shape[0]
    tm, t = (TM_ALL, T_ALL) if with_ctx else (TM_LAT, SEQ)
    col = lambda c: pl.BlockSpec((None, tm, 256), lambda b, i: (b, i, c))
    split = len(x_parts) == 2
    x_args = (x_parts[0],) * 3 + (x_parts[1],) if split else x_parts
    x_specs = _x_specs(True) if split else [pl.BlockSpec((None, tm, D_MODEL), lambda b, i: (b, i, 0))]
    return pl.pallas_call(
        functools.partial(_outproj_kernel, nx=len(x_args), with_ctx=with_ctx),
        grid=(b, t // tm),
        in_specs=[
            *x_specs,
            col(0), col(0), col(0), col(0), col(0), col(0), col(5),
            _const_spec((1, 256)), _const_spec((256, 256)), _const_spec((512, 256)),
            _layer_spec((256, 256), l), _layer_spec((D_MODEL, D_MODEL), l), _const_spec((1, D_MODEL)),
            *_mod_specs(2),
        ],
        out_specs=pl.BlockSpec((None, tm, D_MODEL), lambda b, i: (b, i, 0)),
        out_shape=jax.ShapeDtypeStruct((b, t, D_MODEL), F32),
        compiler_params=_cparams(("arbitrary", "arbitrary")),
        name="out_proj",
    )(*x_args, o_a, p, q, o_c, o_f, o_bk, u_f32, hg, hh, cs, w_fnet, w_out, g_post, mods, mods)


def _ffn_kernel(x_ref, g_ref, shc_ref, shb_ref, scc_ref, scb_ref, w1_ref, w2_ref, gp_ref, gtc_ref, gtb_ref,
                o_ref, *, with_ctx):
    fc = 256

    def swiglu(h):
        acc = jnp.zeros((h.shape[0], D_MODEL), F32)
        for c in range(D_FF // fc):
            a = _dot(h, w1_ref[:, c * fc:(c + 1) * fc])
            g = _dot(h, w1_ref[:, D_FF + c * fc:D_FF + (c + 1) * fc])
            act = (a * jax.nn.sigmoid(a) * g).astype(BF16)
            acc = acc + _dot(act, w2_ref[c * fc:(c + 1) * fc, :])
        return acc

    parts = list(enumerate(_row_parts(x_ref.shape[0], with_ctx)))
    pre = lambda part, r0, r1: _modulated(x_ref[r0:r1, :], g_ref, (shc_ref, shb_ref), (scc_ref, scb_ref), part, with_ctx)
    post = lambda part, r0, r1, acc: _gated_residual(x_ref[r0:r1, :], acc, gp_ref, (gtc_ref, gtb_ref), part, with_ctx)
    if with_ctx:
        acc = swiglu(jnp.concatenate([pre(part, r0, r1) for part, (r0, r1) in parts], axis=0))
        for part, (r0, r1) in parts:
            o_ref[r0:r1, :] = post(part, r0, r1, acc[r0:r1])
    else:
        for part, (r0, r1) in parts:
            o_ref[r0:r1, :] = post(part, r0, r1, swiglu(pre(part, r0, r1)))


def _ffn_call(xs, g_pre, w1, w2, l, g_post, mods, with_ctx):
    b, t, _ = xs.shape
    tm = TM_ALL if with_ctx else TM_LAT
    return pl.pallas_call(
        functools.partial(_ffn_kernel, with_ctx=with_ctx),
        grid=(b, t // tm),
        in_specs=[
            pl.BlockSpec((None, tm, D_MODEL), lambda b, i: (b, i, 0)),
            _const_spec((1, D_MODEL)),
            *_mod_specs(3), *_mod_specs(4),
            _layer_spec((D_MODEL, 2 * D_FF), l), _layer_spec((D_FF, D_MODEL), l), _const_spec((1, D_MODEL)),
            *_mod_specs(5),
        ],
        out_specs=pl.BlockSpec((None, tm, D_MODEL), lambda b, i: (b, i, 0)),
        out_shape=jax.ShapeDtypeStruct(xs.shape, F32),
        compiler_params=_cparams(("arbitrary", "arbitrary")),
        name="ffn",
    )(xs, g_pre, mods, mods, mods, mods, w1, w2, g_post, mods, mods)


def kernel(x, c, ctx, c_ctx, w_ada, b_ada, g_pre_mix, g_post_mix, g_pre_ffn, g_post_ffn, w_in, w_out,
           na_rpb, w_fnet, swa_sink, hgrn_lb_logits, hgrn_norm_g, w_ffn_in, w_ffn_out):
    tb = _tables()
    bf = lambda name: jnp.asarray(tb[name]).astype(BF16)
    cos, sin = jnp.asarray(tb["rope_cos"]), jnp.asarray(tb["rope_sin"])
    f64, g2, fc, chan_cs, hh = bf("f64"), bf("g_stage2"), bf("f_ctx"), bf("chan_cs"), bf("head_ones")
    ball, bblk, lmask = bf("hgrn_b"), bf("hgrn_bblk"), jnp.asarray(tb["hgrn_mask"])
    na_mask, swa_mask = jnp.asarray(tb["na_mask"]), jnp.asarray(tb["swa_mask"])

    cc = jnp.concatenate([c, c_ctx[None, :]], axis=0)
    mods = _ada_call(jnp.broadcast_to(cc[:, :, None], (3, D_MODEL, 128)), w_ada, b_ada)

    lb = jnp.cumsum(jax.nn.softmax(hgrn_lb_logits.astype(F32), axis=1), axis=1)
    lb = lb - lb[:, :1]
    gl_all = jnp.stack([jnp.log(lb), jnp.log1p(-lb)], axis=2)

    w_in_b, w_out_b, w_fnet_b = w_in.astype(BF16), w_out.astype(BF16), w_fnet.astype(BF16)
    w_ffn_in_b, w_ffn_out_b = w_ffn_in.astype(BF16), w_ffn_out.astype(BF16)

    xs = None
    for l in range(DEPTH):
        with_ctx = l < DEPTH - 1
        row = lambda a: a[l][None, :]
        x_all = (x, ctx) if l == 0 else (xs,)
        x_res = x_all if with_ctx else x_all[:1]
        u_att, u_f32 = _proj_call(x_all, row(g_pre_mix), mods[l], w_in_b, l, cos, sin)

        rpb = jnp.pad(na_rpb[l], ((0, 0), (NA_DR_PAD, NA_DR_ROWS - NA_DR_PAD - (2 * NA_WIN_R - 1)), (0, 0)))
        t2 = _na_bias_call(rpb[:, :, tb["na_lane_l"]], rpb[:, :, tb["na_lane_r"]])
        o_a, o_c = _attn_call(u_att, swa_sink[l], na_mask, t2, swa_mask, with_ctx)
        p, q = _fnet_call(u_f32, f64, g2, fc)
        o_f, o_bk = _hgrn_call(u_f32, gl_all[:, l], ball, lmask, bblk, lb_zero=(l == 0))

        xs = _outproj_call(x_res, o_a, p, q, o_c, o_f, o_bk, u_f32, row(hgrn_norm_g), hh, chan_cs,
                           w_fnet_b, w_out_b, l, row(g_post_mix), mods[l], with_ctx)
        xs = _ffn_call(xs, row(g_pre_ffn), w_ffn_in_b, w_ffn_out_b, l, row(g_post_ffn), mods[l], with_ctx)
    return xs
```

```python
import functools

import numpy as np
import jax
import jax.numpy as jnp
from jax import lax
from jax.experimental import pallas as pl
from jax.experimental.pallas import tpu as pltpu

F32, BF16 = jnp.float32, jnp.bfloat16

D_MODEL = 1024
SEQ = 8192
DEPTH = 2
GRID_W = 64
GRID_ROWS = SEQ // GRID_W
CTX = 256
HEAD_DIM = 64
NA_WIN_R, NA_WIN_C = 8, 16
SWA_WINDOW = 128
ROPE_THETA = 10000.0
EPS = 1e-6
D_FF = 2816
D_IN = 2816

TB = 256
T_ALL = SEQ + CTX
NBLK = T_ALL // TB
CTX_BLK = SEQ // TB
TM_ALL = 768
TM_LAT = 1024
NA_KROWS = 12
NA_KEYS = NA_KROWS * GRID_W
NA_DR_PAD = 4
NA_DR_ROWS = 24
SWA_KEYS = TB + 2 * SWA_WINDOW
CHUNK = 64
NCHUNK = TB // CHUNK
HGRN_FAST_RANGE = 60.0
Q_SCALE = HEAD_DIM ** -0.5
VMEM_LIMIT = 56 * 1024 * 1024

ATT_W, F32_W = 1280, 1536


def _dot(a, b):
    return jnp.dot(a, b, preferred_element_type=F32)


def _dot_nt(a, b):
    return lax.dot_general(a, b, (((1,), (1,)), ((), ())), preferred_element_type=F32)


def _dot_tn(a, b):
    return lax.dot_general(a, b, (((0,), (0,)), ((), ())), preferred_element_type=F32)


def _cparams(sem):
    return pltpu.CompilerParams(dimension_semantics=sem, vmem_limit_bytes=VMEM_LIMIT)


def _const_spec(shape):
    nd = len(shape)
    return pl.BlockSpec(shape, lambda *_: (0,) * nd, pipeline_mode=pl.Buffered(1))


def _layer_spec(shape, l):
    nd = len(shape)
    return pl.BlockSpec((None,) + shape, lambda *_: (l,) + (0,) * nd, pipeline_mode=pl.Buffered(1))


@functools.lru_cache(maxsize=None)
def _tables():
    t = {}
    lane = np.arange(128)
    d = lane % 64
    freq = ROPE_THETA ** (-(d % 16) / 16.0)
    pos = np.arange(SEQ)
    p = np.where(d[None, :] < 32, (pos // GRID_W)[:, None], (pos % GRID_W)[:, None]).astype(np.float64)
    ang = p * freq[None, :]
    sign = np.where((d % 32) < 16, -1.0, 1.0)
    cos = np.concatenate([np.cos(ang), np.ones((CTX, 128))], axis=0)
    sin = np.concatenate([np.sin(ang) * sign[None, :], np.zeros((CTX, 128))], axis=0)
    t["rope_cos"], t["rope_sin"] = cos.astype(np.float32), sin.astype(np.float32)

    masks = []
    a, cq = np.arange(4), np.arange(GRID_W)
    rho, ck = np.arange(NA_KROWS), np.arange(GRID_W)
    for r0 in (0, 8, GRID_ROWS - 4):
        kr0 = int(np.clip(r0 - 4, 0, GRID_ROWS - NA_KROWS))
        rq, rk = r0 + a, kr0 + rho
        start = np.clip(rq - NA_WIN_R // 2, 0, GRID_ROWS - NA_WIN_R)
        vr = (rk[None, :] >= start[:, None]) & (rk[None, :] < start[:, None] + NA_WIN_R)
        cs = np.clip(cq - NA_WIN_C // 2, 0, GRID_W - NA_WIN_C)
        vc = (ck[None, :] >= cs[:, None]) & (ck[None, :] < cs[:, None] + NA_WIN_C)
        val = vr[:, None, :, None] & vc[None, :, None, :]
        masks.append(np.where(val, 0.0, -np.inf).reshape(TB, NA_KEYS))
    masks.append(np.full((TB, NA_KEYS), -np.inf))
    t["na_mask"] = np.stack(masks).astype(np.float32)
    n = np.arange(128)
    clipc = lambda m: np.clip(m, -(NA_WIN_C - 1), NA_WIN_C - 1) + NA_WIN_C - 1
    t["na_lane_l"] = clipc(np.where(n < 64, n, n - 128)).astype(np.int32)
    t["na_lane_r"] = clipc(n - 64).astype(np.int32)

    rel = np.arange(SWA_KEYS)[None, :] - np.arange(TB)[:, None]
    swa = [np.where(np.abs(rel + off) <= SWA_WINDOW, 0.0, -np.inf) for off in (0, -SWA_WINDOW, -2 * SWA_WINDOW)]
    swa.append(np.full((TB, SWA_KEYS), -np.inf))
    t["swa_mask"] = np.stack(swa).astype(np.float32)

    def cs_tab(num, den):
        ang = 2.0 * np.pi * (num % den) / den
        return np.cos(ang), np.sin(ang)

    k1, t1 = np.arange(64)[:, None], np.arange(64)[None, :]
    c64, s64 = cs_tab(k1 * t1, 64)
    t["f64"] = np.concatenate([c64, -s64], axis=0).astype(np.float32)
    k2, t2 = np.arange(128)[:, None], np.arange(128)[None, :]
    g = []
    for kk in range(64):
        gc, gs = cs_tab((64 * k2 + kk) * t2, SEQ)
        g.append(np.block([[gc, gs], [-gs, gc]]))
    t["g_stage2"] = np.stack(g).astype(np.float32)
    kc, tc = np.arange(CTX)[:, None], np.arange(CTX)[None, :]
    cc, sc = cs_tab(kc * tc, CTX)
    t["f_ctx"] = np.concatenate([cc, -sc], axis=0).astype(np.float32)
    ch = np.arange(256)
    same = (ch[:, None] // 64) == (ch[None, :] // 64)
    c_ch, s_ch = cs_tab((ch[:, None] % 64) * (ch[None, :] % 64), 64)
    t["chan_cs"] = np.concatenate([np.where(same, c_ch, 0.0), np.where(same, s_ch, 0.0)], axis=0).astype(np.float32)
    t["head_ones"] = same.astype(np.float32)

    balls, lmasks, bblk = [], [], []
    for rev in (False, True):
        tau = (CHUNK - 1 - np.arange(CHUNK)) if rev else np.arange(CHUNK)
        tt, tj = tau[:, None], tau[None, :]
        blocks = [(tj <= tt), (tj > tt)]
        masks = [tt == tj]
        for h in (32, 16, 8, 4, 2, 1):
            up = ((tt // h) % 2) == 1
            mid = (tt // (2 * h)) * (2 * h) + h - 1
            blocks.append(np.where(up, (tj > mid) & (tj <= tt), (tj > tt) & (tj <= mid)))
            up_s = ((tj // h) % 2) == 1
            masks.append((tt // (2 * h) == tj // (2 * h)) & up & ~up_s)
        balls.append(np.concatenate(blocks, axis=0))
        lmasks.append(np.stack([np.tile(m, (4, 1)) for m in masks]))
        bblk.append(np.kron(np.eye(NCHUNK), blocks[0]))
    t["hgrn_b"] = np.stack(balls).astype(np.float32)
    t["hgrn_mask"] = np.stack(lmasks).astype(np.float32)
    t["hgrn_bblk"] = np.stack(bblk).astype(np.float32)
    return t


def _ada_kernel(cb_ref, w_ref, b_ref, o_ref, s_ref):
    tn = w_ref.shape[1]
    nrep = tn // 128

    @pl.when((pl.program_id(0) == 0) & (pl.program_id(1) == 0))
    def _():
        cb = cb_ref[...]
        s_ref[...] = cb * jax.nn.sigmoid(cb)

    def body(kc, accs):
        k0 = pl.multiple_of(kc * 8, 8)
        wt = w_ref[pl.ds(k0, 8), :]
        return tuple(accs[r] + wt * jnp.concatenate([s_ref[r, pl.ds(k0, 8), :]] * nrep, axis=1) for r in range(3))

    accs = lax.fori_loop(0, D_MODEL // 8, body, tuple(jnp.zeros((8, tn), F32) for _ in range(3)), unroll=8)
    for r in range(3):
        o_ref[r] = jnp.sum(accs[r], axis=0, keepdims=True) + b_ref[...]


def _ada_call(cb, w_ada, b_ada):
    tn = 1536
    return pl.pallas_call(
        _ada_kernel,
        grid=(DEPTH, 6 * D_MODEL // tn),
        in_specs=[
            pl.BlockSpec((3, D_MODEL, 128), lambda l, j: (0, 0, 0)),
            pl.BlockSpec((None, D_MODEL, tn), lambda l, j: (l, 0, j)),
            pl.BlockSpec((None, 1, tn), lambda l, j: (l, 0, j)),
        ],
        out_specs=pl.BlockSpec((None, 3, 1, tn), lambda l, j: (l, 0, 0, j)),
        out_shape=jax.ShapeDtypeStruct((DEPTH, 3, 1, 6 * D_MODEL), F32),
        scratch_shapes=[pltpu.VMEM((3, D_MODEL, 128), F32)],
        compiler_params=_cparams(("arbitrary", "arbitrary")),
        name="adaln",
    )(cb, w_ada, b_ada.reshape(DEPTH, 1, 6 * D_MODEL))


def _mod_specs(col):
    return [pl.BlockSpec((None, 1, D_MODEL), lambda b, i: (2, 0, col)),
            pl.BlockSpec((None, 1, D_MODEL), lambda b, i: (b, 0, col))]


def _row_parts(tm, with_ctx):
    return [(0, tm - CTX), (tm - CTX, tm)] if with_ctx else [(0, tm // 2), (tm // 2, tm)]


def _mod(mc_ref, mb_ref, part, with_ctx):
    if not with_ctx or part == 0:
        return mb_ref[...]
    last = pl.program_id(1) == pl.num_programs(1) - 1
    return jnp.where(last, mc_ref[...], mb_ref[...])


def _modulated(x, g_ref, sh_refs, sc_refs, part, with_ctx):
    gs = g_ref[...] * (1.0 + _mod(*sc_refs, part, with_ctx))
    return (_rms(x) * gs + _mod(*sh_refs, part, with_ctx)).astype(BF16)


def _gated_residual(x, y, gp_ref, gt_refs, part, with_ctx):
    return x + (_mod(*gt_refs, part, with_ctx) * gp_ref[...]) * _rms(y)


def _rms(x):
    return x * lax.rsqrt(jnp.mean(x * x, axis=-1, keepdims=True) + EPS)


def _x_specs(split):
    if not split:
        return [pl.BlockSpec((None, TM_ALL, D_MODEL), lambda b, i: (b, i, 0))]
    nx = SEQ // TB
    return [pl.BlockSpec((None, TB, D_MODEL), lambda b, i, j=j: (b, jnp.minimum(3 * i + j, nx - 1), 0))
            for j in range(TM_ALL // TB)] + [pl.BlockSpec((None, CTX, D_MODEL), lambda b, i: (b, 0, 0))]


def _x_part(x_refs, part, lo, hi):
    if len(x_refs) == 1:
        return x_refs[0][lo:hi, :]
    xa, xb, xc, cx = x_refs
    if part == 0:
        return jnp.concatenate([xa[...], xb[...]], axis=0)
    last = pl.program_id(1) == pl.num_programs(1) - 1
    return jnp.where(last, cx[...], xc[...])


def _proj_kernel(*refs, nx):
    g_ref, shc_ref, shb_ref, scc_ref, scb_ref, w_ref, cos_ref, sin_ref, att_ref, f32_ref = refs[nx:]
    for part, (lo, hi) in enumerate(_row_parts(TM_ALL, True)):
        rows = slice(lo, hi)
        h = _modulated(_x_part(refs[:nx], part, lo, hi), g_ref, (shc_ref, shb_ref), (scc_ref, scb_ref), part, True)
        cos, sin = cos_ref[rows, :], sin_ref[rows, :]
        first = (lax.broadcasted_iota(jnp.int32, (hi - lo, 128), 1) % 32) < 16

        def rope(v):
            return v * cos + jnp.where(first, pltpu.roll(v, 112, 1), pltpu.roll(v, 16, 1)) * sin

        a = _dot(h, w_ref[:, 0:768])
        att_ref[rows, 0:256] = (a[:, 0:256] * Q_SCALE).astype(BF16)
        att_ref[rows, 256:768] = a[:, 256:768].astype(BF16)
        f32_ref[rows, 0:256] = _dot(h, w_ref[:, 768:1024])
        s = _dot(h, w_ref[:, 1024:1536])
        att_ref[rows, 768:896] = (rope(s[:, 0:128]) * Q_SCALE).astype(BF16)
        att_ref[rows, 896:1024] = (rope(s[:, 128:256]) * Q_SCALE).astype(BF16)
        att_ref[rows, 1024:1152] = rope(s[:, 256:384]).astype(BF16)
        att_ref[rows, 1152:1280] = s[:, 384:512].astype(BF16)
        f32_ref[rows, 256:1536] = _dot(h, w_ref[:, 1536:2816])


def _proj_call(x_parts, g, mods, w_in, l, cos, sin):
    b = x_parts[0].shape[0]
    tm = TM_ALL
    split = len(x_parts) == 2
    x_args = (x_parts[0],) * 3 + (x_parts[1],) if split else x_parts
    return pl.pallas_call(
        functools.partial(_proj_kernel, nx=len(x_args)),
        grid=(b, T_ALL // tm),
        in_specs=[
            *_x_specs(split),
            _const_spec((1, D_MODEL)),
            *_mod_specs(0), *_mod_specs(1),
            _layer_spec((D_MODEL, D_IN), l),
            pl.BlockSpec((tm, 128), lambda b, i: (i, 0)),
            pl.BlockSpec((tm, 128), lambda b, i: (i, 0)),
        ],
        out_specs=[
            pl.BlockSpec((None, tm, ATT_W), lambda b, i: (b, i, 0)),
            pl.BlockSpec((None, tm, F32_W), lambda b, i: (b, i, 0)),
        ],
        out_shape=[jax.ShapeDtypeStruct((b, T_ALL, ATT_W), BF16), jax.ShapeDtypeStruct((b, T_ALL, F32_W), F32)],
        compiler_params=_cparams(("arbitrary", "arbitrary")),
        name="proj_in",
    )(*x_args, g, mods, mods, mods, mods, w_in, cos, sin)


def _na_bias_kernel(vl_ref, vr_ref, o_ref):
    lane = lax.broadcasted_iota(jnp.int32, (GRID_W, 128), 1)
    for e in range(NA_DR_ROWS - 1):
        left = pltpu.roll(jnp.broadcast_to(vl_ref[e:e + 1, :], (GRID_W, 128)), 0, 1, stride=1, stride_axis=0)
        right = pltpu.roll(jnp.broadcast_to(vr_ref[e + 1:e + 2, :], (GRID_W, 128)), 0, 1, stride=1, stride_axis=0)
        o_ref[e] = jnp.where(lane < 64, left, right)


def _na_bias_call(vl, vr):
    spec = pl.BlockSpec((None, NA_DR_ROWS, 128), lambda h: (h, 0, 0))
    return pl.pallas_call(
        _na_bias_kernel,
        grid=(4,),
        in_specs=[spec, spec],
        out_specs=pl.BlockSpec((None, NA_DR_ROWS - 1, GRID_W, 128), lambda h: (h, 0, 0, 0)),
        out_shape=jax.ShapeDtypeStruct((4, NA_DR_ROWS - 1, GRID_W, 128), F32),
        compiler_params=_cparams(("arbitrary",)),
        name="na_bias",
    )(vl, vr)


AT_ROWS = 16


def _attn_block_variant(b, i):
    return (jnp.where(i == CTX_BLK, 3, jnp.where(i == 0, 0, jnp.where(i == CTX_BLK - 1, 2, 1))), 0, 0)


def _lane_half(shape, j):
    lane = lax.broadcasted_iota(jnp.int32, shape, 1)
    return (lane >= 64 * j) & (lane < 64 * (j + 1))


def _na_window(i):
    is_ctx = i == CTX_BLK
    r0 = i * 4
    kr0 = jnp.where(is_ctx, 0, jnp.clip(r0 - 4, 0, GRID_ROWS - NA_KROWS))
    delta = jnp.where(is_ctx, 3, kr0 - r0 + NA_WIN_R - 1) + NA_DR_PAD
    return pl.multiple_of(kr0 * GRID_W, GRID_W), delta


def _na_scores(i, q_ref, k_ref, s_ref):
    kstart, _ = _na_window(i)
    for hp in range(2):
        cs = slice(hp * 128, (hp + 1) * 128)
        q2 = q_ref[:, cs]
        kw, kc = k_ref[pl.ds(kstart, NA_KEYS), cs], k_ref[SEQ:T_ALL, cs]
        qm = jnp.concatenate([jnp.where(_lane_half((TB, 128), j), q2, jnp.zeros_like(q2)) for j in range(2)], axis=0)
        s_ref[hp, :, 0:CTX] = _dot_nt(qm, kc)
        s_ref[hp, :, CTX:] = _dot_nt(qm, kw)


def _na_softmax(i, s_ref, mask_ref, t2_ref, p_ref):
    _, delta = _na_window(i)
    for hp in range(2):
        for r in range(2 * TB // AT_ROWS):
            rows = slice(r * AT_ROWS, (r + 1) * AT_ROWS)
            j, rq = divmod(r, TB // AT_ROWS)
            a, sub = divmod(rq, GRID_W // AT_ROWS)
            trow = slice(sub * AT_ROWS, (sub + 1) * AT_ROWS)
            bias = jnp.concatenate(
                [t2_ref[2 * hp + j, delta + 2 * rp - a, trow, :] for rp in range(NA_KROWS // 2)], axis=1)
            s_nb = s_ref[hp, rows, CTX:] + bias + mask_ref[rq * AT_ROWS:(rq + 1) * AT_ROWS, :]
            s_c = s_ref[hp, rows, 0:CTX]
            m = jnp.maximum(jnp.max(s_nb, axis=-1, keepdims=True), jnp.max(s_c, axis=-1, keepdims=True))
            p_ref[hp, rows, 0:CTX] = jnp.exp((s_c - m).astype(BF16))
            p_ref[hp, rows, CTX:] = jnp.exp((s_nb - m).astype(BF16))


def _na_values(i, p_ref, v_ref, o_ref):
    kstart, _ = _na_window(i)
    lane = lax.broadcasted_iota(jnp.int32, (TB, 128), 1)
    for hp in range(2):
        cs = slice(hp * 128, (hp + 1) * 128)
        vw, vc = v_ref[pl.ds(kstart, NA_KEYS), cs], v_ref[SEQ:T_ALL, cs]
        o2 = (_dot(p_ref[hp, :, 0:CTX], jnp.concatenate([vc, jnp.ones_like(vc)], axis=1))
              + _dot(p_ref[hp, :, CTX:], jnp.concatenate([vw, jnp.ones_like(vw)], axis=1)))
        top = o2[0:TB, 0:128] * (1.0 / o2[0:TB, 128:256])
        bot = o2[TB:2 * TB, 0:128] * (1.0 / o2[TB:2 * TB, 128:256])
        o_ref[:, cs] = jnp.where(lane < 64, top, bot).astype(BF16)


def _swa_window(i):
    return pl.multiple_of(jnp.clip(i * TB - SWA_WINDOW, 0, SEQ - SWA_KEYS), SWA_WINDOW)


def _swa_scores(i, q_ref, k_ref, s_ref):
    kw, kc = k_ref[pl.ds(_swa_window(i), SWA_KEYS), :], k_ref[SEQ:T_ALL, :]
    for g in range(2):
        q2 = q_ref[:, g * 128:(g + 1) * 128].astype(F32)
        qa = jnp.concatenate([jnp.where(_lane_half((TB, 128), g), q2 if j == g else pltpu.roll(q2, 64, 1), 0.0)
                              for j in range(2)], axis=0).astype(BF16)
        s_ref[g, :, 0:CTX] = _dot_nt(qa, kc)
        s_ref[g, :, CTX:] = _dot_nt(qa, kw)


def _swa_softmax(sink_ref, s_ref, mask_ref, p_ref, esink_ref):
    for g in range(2):
        for r in range(2 * TB // AT_ROWS):
            rows = slice(r * AT_ROWS, (r + 1) * AT_ROWS)
            j, rq = divmod(r, TB // AT_ROWS)
            sink = sink_ref[2 * g + j]
            s_l = s_ref[g, rows, CTX:] + mask_ref[rq * AT_ROWS:(rq + 1) * AT_ROWS, :]
            s_c = s_ref[g, rows, 0:CTX]
            m = jnp.maximum(jnp.maximum(jnp.max(s_l, axis=-1, keepdims=True),
                                        jnp.max(s_c, axis=-1, keepdims=True)), sink)
            p_ref[g, rows, 0:CTX] = jnp.exp((s_c - m).astype(BF16))
            p_ref[g, rows, CTX:] = jnp.exp((s_l - m).astype(BF16))
            esink_ref[g, rows, :] = jnp.exp(jnp.broadcast_to(sink - m, (AT_ROWS, 128)))


def _swa_values(i, p_ref, esink_ref, v_ref, o_ref):
    vw, vc = v_ref[pl.ds(_swa_window(i), SWA_KEYS), :], v_ref[SEQ:T_ALL, :]
    lane = lax.broadcasted_iota(jnp.int32, (TB, 128), 1)
    vc2 = jnp.concatenate([vc, jnp.ones_like(vc)], axis=1)
    vw2 = jnp.concatenate([vw, jnp.ones_like(vw)], axis=1)
    for g in range(2):
        o2 = _dot(p_ref[g, :, 0:CTX], vc2) + _dot(p_ref[g, :, CTX:], vw2)
        o = o2[:, 0:128] * (1.0 / (o2[:, 128:256] + esink_ref[g]))
        outs = [o[j * TB:(j + 1) * TB] if j == g else pltpu.roll(o[j * TB:(j + 1) * TB], 64, 1) for j in range(2)]
        o_ref[:, g * 128:(g + 1) * 128] = jnp.where(lane < 64, outs[0], outs[1]).astype(BF16)


def _attn_kernel(sink_ref, qn_ref, kn_ref, vn_ref, maskn_ref, t2_ref, qs_ref, ks_ref, vs_ref, masks_ref,
                 on_ref, os_ref, sn_ref, pn_ref, ss_ref, ps_ref, esink_ref):
    i = pl.program_id(1)
    _na_scores(i, qn_ref, kn_ref, sn_ref)
    _swa_scores(i, qs_ref, ks_ref, ss_ref)
    _na_softmax(i, sn_ref, maskn_ref, t2_ref, pn_ref)
    _swa_softmax(sink_ref, ss_ref, masks_ref, ps_ref, esink_ref)
    _na_values(i, pn_ref, vn_ref, on_ref)
    _swa_values(i, ps_ref, esink_ref, vs_ref, os_ref)


def _attn_call(u_att, sink, na_mask, t2, swa_mask, with_ctx):
    b = u_att.shape[0]
    nb = NBLK if with_ctx else CTX_BLK
    blk = lambda width, col: pl.BlockSpec((None, TB, width), lambda b, i: (b, i, col))
    full = lambda width, col: pl.BlockSpec((None, T_ALL, width), lambda b, i: (b, 0, col))
    out = pl.BlockSpec((None, TB, 256), lambda b, i: (b, i, 0))
    return pl.pallas_call(
        _attn_kernel,
        grid=(b, nb),
        in_specs=[
            pl.BlockSpec(memory_space=pltpu.SMEM),
            blk(256, 0), full(256, 1), full(256, 2),
            pl.BlockSpec((None, TB, NA_KEYS), _attn_block_variant),
            _const_spec((4, NA_DR_ROWS - 1, GRID_W, 128)),
            blk(256, 3), full(128, 8), full(128, 9),
            pl.BlockSpec((None, TB, SWA_KEYS), _attn_block_variant),
        ],
        out_specs=[out, out],
        out_shape=[jax.ShapeDtypeStruct((b, nb * TB, 256), BF16)] * 2,
        scratch_shapes=[pltpu.VMEM((2, 2 * TB, CTX + NA_KEYS), F32), pltpu.VMEM((2, 2 * TB, CTX + NA_KEYS), BF16),
                        pltpu.VMEM((2, 2 * TB, CTX + SWA_KEYS), F32), pltpu.VMEM((2, 2 * TB, CTX + SWA_KEYS), BF16),
                        pltpu.VMEM((2, 2 * TB, 128), F32)],
        compiler_params=_cparams(("arbitrary", "arbitrary")),
        name="attn",
    )(sink, u_att, u_att, u_att, na_mask, t2, u_att, u_att, u_att, swa_mask)


FN_XP = 136
FN_YP = 136
FN_ZP = 72


def _fnet_kernel(x_ref, f64_ref, g_ref, fc_ref, p_ref, q_ref, xz_ref, zq_ref, y_ref):
    sc_lat, sc_ctx = (64.0 * SEQ) ** -0.5, (64.0 * CTX) ** -0.5
    pq = _dot(fc_ref[...], x_ref[SEQ:T_ALL, :].astype(BF16))
    p_ref[SEQ:T_ALL, :] = (pq[0:CTX] * sc_ctx).astype(BF16)
    q_ref[SEQ:T_ALL, :] = (pq[CTX:2 * CTX] * sc_ctx).astype(BF16)
    f64 = f64_ref[...]

    def repitch(t1, carry):
        xz_ref[pl.ds(pl.multiple_of(t1 * FN_XP, 8), 128), :] = x_ref[pl.ds(pl.multiple_of(t1 * 128, 128), 128), :]
        return carry

    lax.fori_loop(0, 64, repitch, 0, unroll=4)

    def stage1(t2, carry):
        xs = xz_ref[pl.ds(t2, 64, stride=FN_XP), :].astype(BF16)
        y_ref[pl.ds(t2, 128, stride=FN_YP), :] = _dot(f64, xs)
        return carry

    lax.fori_loop(0, 128, stage1, 0, unroll=8)

    def stage2(k1, carry):
        yre = y_ref[pl.ds(pl.multiple_of(k1 * FN_YP, 8), 128), :]
        yim = y_ref[pl.ds(pl.multiple_of((64 + k1) * FN_YP, 8), 128), :]
        ycat = jnp.concatenate([yre, yim], axis=0).astype(BF16)
        pq = _dot(g_ref[k1], ycat)
        xz_ref[pl.ds(k1, 128, stride=FN_ZP), :] = pq[0:128] * sc_lat
        zq_ref[pl.ds(k1, 128, stride=FN_ZP), :] = pq[128:256] * sc_lat
        return carry

    lax.fori_loop(0, 64, stage2, 0, unroll=4)

    def emit(k2, carry):
        src = pl.ds(pl.multiple_of(k2 * FN_ZP, 8), 64)
        dst = pl.ds(pl.multiple_of(k2 * 64, 64), 64)
        p_ref[dst, :] = xz_ref[src, :].astype(BF16)
        q_ref[dst, :] = zq_ref[src, :].astype(BF16)
        return carry

    lax.fori_loop(0, 128, emit, 0, unroll=4)


def _fnet_call(u_f32, f64, g2, fc):
    b = u_f32.shape[0]
    spec = pl.BlockSpec((None, T_ALL, 128), lambda b, j: (b, 0, j))
    return pl.pallas_call(
        _fnet_kernel,
        grid=(b, 2),
        in_specs=[spec, _const_spec((128, 64)), _const_spec((64, 256, 256)), _const_spec((2 * CTX, CTX))],
        out_specs=[spec, spec],
        out_shape=[jax.ShapeDtypeStruct((b, T_ALL, 256), BF16)] * 2,
        scratch_shapes=[pltpu.VMEM((128 * FN_ZP, 128), F32), pltpu.VMEM((128 * FN_ZP, 128), F32),
                        pltpu.VMEM((128 * FN_YP, 128), F32)],
        compiler_params=_cparams(("arbitrary", "arbitrary")),
        name="fnet_dft",
    )(u_f32, f64, g2, fc)


def _hgrn_gates(qd, fz, gl, lb_zero):
    q = qd * jax.nn.sigmoid(qd)
    sp = jnp.log(1.0 + jnp.exp(-jnp.abs(fz)))
    ls = jnp.minimum(fz, 0.0) - sp
    lk = jnp.minimum(-fz, 0.0) - sp
    if lb_zero:
        return q, jnp.exp(lk), ls
    b_ = gl[1:2] + ls
    a_ = gl[0:1]
    logf = jnp.maximum(a_, b_) + jnp.log(1.0 + jnp.exp(-jnp.abs(a_ - b_)))
    return q, jnp.exp(gl[1:2] + lk), logf


def _dot01(b01, x):
    h1 = x.astype(BF16)
    r1 = x - h1.astype(F32)
    h2 = r1.astype(BF16)
    h3 = (r1 - h2.astype(F32)).astype(BF16)
    return _dot(b01, h1) + _dot(b01, h2) + _dot(b01, h3)


def _head_masks(rows):
    lane = lax.broadcasted_iota(jnp.int32, (rows, 256), 1)
    return [(lane >= 64 * h) & (lane < 64 * (h + 1)) for h in range(4)]


def _stack_heads(w, hms):
    return jnp.concatenate([jnp.where(hm, w, jnp.zeros_like(w)) for hm in hms], axis=0)


def _state_update(st_ref, st, e_last, vb, ks):
    ktv = _dot_tn(vb, ks)
    r2 = lax.broadcasted_iota(jnp.int32, (256, 256), 0) // 64
    c2 = lax.broadcasted_iota(jnp.int32, (256, 256), 1) // 64
    st_ref[...] = st * e_last + jnp.where(r2 == c2, ktv, 0.0)


def _hgrn_chunk_exact(qd, fz, v, st_ref, gl, ball, lmask_ref, d, rev, lb_zero):
    q, kk, logf = _hgrn_gates(qd, fz, gl, lb_zero)
    e = jnp.exp(_dot01(ball, logf))
    row = lax.broadcasted_iota(jnp.int32, (CHUNK, 256), 0)
    tau = (CHUNK - 1 - row) if rev else row
    hms = _head_masks(CHUNK)
    att = _dot_nt(_stack_heads(q.astype(BF16), hms), kk.astype(BF16)) * lmask_ref[d, 0]
    for li, sh in enumerate((5, 4, 3, 2, 1, 0)):
        up = ((tau >> sh) & 1) == 1
        w = (jnp.where(up, q, kk) * e[128 + 64 * li:192 + 64 * li]).astype(BF16)
        att = att + _dot_nt(_stack_heads(w, hms), w) * lmask_ref[d, 1 + li]
    vb = v.astype(BF16)
    st = st_ref[...]
    o = _dot_nt((q * e[0:64]).astype(BF16), st.astype(BF16))
    for h in range(4):
        o = o + _dot(att[64 * h:64 * (h + 1)].astype(BF16), jnp.where(hms[h], vb, jnp.zeros_like(vb)))
    tl = 0 if rev else CHUNK - 1
    _state_update(st_ref, st, e[tl:tl + 1], vb, (kk * e[64:128]).astype(BF16))
    return o


def _hgrn_chunk_prep(sl, q_ref, f_ref, gl, tri, rev, lb_zero):
    q, kk, logf = _hgrn_gates(q_ref[sl, :], f_ref[sl, :], gl, lb_zero)
    cum = _dot01(tri, logf)
    mid_row, last_row = (CHUNK // 2, 0) if rev else (CHUNK // 2 - 1, CHUNK - 1)
    cum_mid, cum_last = cum[mid_row:mid_row + 1], cum[last_row:last_row + 1]
    a = cum - cum_mid
    return q * jnp.exp(a), kk * jnp.exp(-a), cum_mid, cum_last, jnp.max(jnp.abs(a))


def _hgrn_chunk_fast(sl, qt, kt, cum_mid, cum_last, v_ref, st_ref, o_ref, rev):
    hms = _head_masks(CHUNK)
    r = lax.broadcasted_iota(jnp.int32, (4 * CHUNK, CHUNK), 0) & (CHUNK - 1)
    s = lax.broadcasted_iota(jnp.int32, (4 * CHUNK, CHUNK), 1)
    causal = (s >= r) if rev else (s <= r)
    vb = v_ref[sl, :].astype(BF16)
    att = jnp.where(causal, _dot_nt(_stack_heads(qt.astype(BF16), hms), kt.astype(BF16)), 0.0)
    st = st_ref[...]
    o = _dot_nt((qt * jnp.exp(cum_mid)).astype(BF16), st.astype(BF16))
    for h in range(4):
        o = o + _dot(att[64 * h:64 * (h + 1)].astype(BF16), jnp.where(hms[h], vb, jnp.zeros_like(vb)))
    o_ref[sl, :] = o
    _state_update(st_ref, st, jnp.exp(cum_last), vb, (kt * jnp.exp(cum_last - cum_mid)).astype(BF16))


def _hgrn_kernel(qf_ref, ff_ref, vf_ref, qb_ref, fb_ref, vb_ref, gl_ref, ball_ref, lmask_ref, bblk_ref,
                 of_ref, ob_ref, sf_ref, sb_ref, sf0_ref, sb0_ref, *, lb_zero):
    @pl.when(pl.program_id(1) == 0)
    def _():
        sf_ref[...] = jnp.zeros_like(sf_ref)
        sb_ref[...] = jnp.zeros_like(sb_ref)

    sf0_ref[...] = sf_ref[...]
    sb0_ref[...] = sb_ref[...]
    dirs = ((qf_ref, ff_ref, vf_ref, sf_ref, of_ref, False), (qb_ref, fb_ref, vb_ref, sb_ref, ob_ref, True))
    rows = lambda rev, s: slice((NCHUNK - 1 - s if rev else s) * CHUNK, (NCHUNK - s if rev else s + 1) * CHUNK)

    def prep(s):
        return [_hgrn_chunk_prep(rows(rev, s), q_ref, f_ref, gl_ref[d], ball_ref[d, 0:CHUNK, :], rev, lb_zero)
                for d, (q_ref, f_ref, v_ref, st_ref, o_ref, rev) in enumerate(dirs)]

    rng, ready = 0.0, prep(0)
    for s in range(NCHUNK):
        nxt = prep(s + 1) if s + 1 < NCHUNK else None
        for d, (q_ref, f_ref, v_ref, st_ref, o_ref, rev) in enumerate(dirs):
            qt, kt, cum_mid, cum_last, amax = ready[d]
            rng = jnp.maximum(rng, amax)
            _hgrn_chunk_fast(rows(rev, s), qt, kt, cum_mid, cum_last, v_ref, st_ref, o_ref, rev)
        ready = nxt

    @pl.when(jnp.logical_not(rng <= HGRN_FAST_RANGE))
    def _():
        sf_ref[...] = sf0_ref[...]
        sb_ref[...] = sb0_ref[...]

        def body(c, carry):
            rf = pl.ds(pl.multiple_of(c * CHUNK, CHUNK), CHUNK)
            rb = pl.ds(pl.multiple_of((NCHUNK - 1 - c) * CHUNK, CHUNK), CHUNK)
            of_ref[rf, :] = _hgrn_chunk_exact(qf_ref[rf, :], ff_ref[rf, :], vf_ref[rf, :], sf_ref,
                                              gl_ref[0], ball_ref[0], lmask_ref, 0, False, lb_zero)
            ob_ref[rb, :] = _hgrn_chunk_exact(qb_ref[rb, :], fb_ref[rb, :], vb_ref[rb, :], sb_ref,
                                              gl_ref[1], ball_ref[1], lmask_ref, 1, True, lb_zero)
            return carry

        lax.fori_loop(0, NCHUNK, body, 0)


def _hgrn_call(u_f32, gl, ball, lmask, bblk, lb_zero):
    b = u_f32.shape[0]

    def fwd(col):
        return pl.BlockSpec((None, TB, 256), lambda b, i: (b, jnp.where(i == 0, CTX_BLK, i - 1), col))

    def bwd(col):
        return pl.BlockSpec((None, TB, 256), lambda b, i: (b, jnp.where(i == 0, CTX_BLK, CTX_BLK - i), col))

    return pl.pallas_call(
        functools.partial(_hgrn_kernel, lb_zero=lb_zero),
        grid=(b, NBLK),
        in_specs=[fwd(1), fwd(2), fwd(4), bwd(1), bwd(3), bwd(4),
                  _const_spec((2, 2, 256)), _const_spec((2, 8 * CHUNK, CHUNK)),
                  _const_spec((2, 7, 4 * CHUNK, CHUNK)), _const_spec((2, TB, TB))],
        out_specs=[fwd(0), bwd(0)],
        out_shape=[jax.ShapeDtypeStruct((b, T_ALL, 256), F32)] * 2,
        scratch_shapes=[pltpu.VMEM((256, 256), F32)] * 4,
        compiler_params=_cparams(("arbitrary", "arbitrary")),
        name="hgrn_scan",
    )(u_f32, u_f32, u_f32, u_f32, u_f32, u_f32, gl, ball, lmask, bblk)


def _outproj_kernel(*refs, nx, with_ctx):
    (oa_ref, p_ref, q_ref, oc_ref, of_ref, ob_ref, gd_ref, hg_ref, hh_ref, cs_ref,
     wf_ref, wo_ref, gp_ref, gtc_ref, gtb_ref, o_ref) = refs[nx:]
    for part, (r0, r1) in enumerate(_row_parts(o_ref.shape[0], with_ctx)):
        rows = slice(r0, r1)
        z = _dot(p_ref[rows, :], cs_ref[0:256, :]) + _dot(q_ref[rows, :], cs_ref[256:512, :])
        o_b = _dot(z.astype(BF16), wf_ref[...])
        od = of_ref[rows, :] + ob_ref[rows, :]
        sq = od * od
        hi = sq.astype(BF16)
        lo = (sq - hi.astype(F32)).astype(BF16)
        ms = (_dot(hi, hh_ref[...]) + _dot(lo, hh_ref[...])) * (1.0 / HEAD_DIM)
        gd = gd_ref[rows, :]
        o_d = od * lax.rsqrt(ms + EPS) * hg_ref[...] * (gd * jax.nn.sigmoid(gd))
        y = (_dot(oa_ref[rows, :], wo_ref[0:256, :]) + _dot(o_b.astype(BF16), wo_ref[256:512, :])
             + _dot(oc_ref[rows, :], wo_ref[512:768, :]) + _dot(o_d.astype(BF16), wo_ref[768:1024, :]))
        x = _x_part(refs[:nx], part, r0, r1)
        o_ref[rows, :] = _gated_residual(x, y, gp_ref, (gtc_ref, gtb_ref), part, with_ctx)


def _outproj_call(x_parts, o_a, p, q, o_c, o_f, o_bk, u_f32, hg, hh, cs, w_fnet, w_out, l, g_post, mods, with_ctx):
    b = x_parts[0].shape[0]
    tm, t = (TM_ALL, T_ALL) if with_ctx else (TM_LAT, SEQ)
    col = lambda c: pl.BlockSpec((None, tm, 256), lambda b, i: (b, i, c))
    split = len(x_parts) == 2
    x_args = (x_parts[0],) * 3 + (x_parts[1],) if split else x_parts
    x_specs = _x_specs(True) if split else [pl.BlockSpec((None, tm, D_MODEL), lambda b, i: (b, i, 0))]
    return pl.pallas_call(
        functools.partial(_outproj_kernel, nx=len(x_args), with_ctx=with_ctx),
        grid=(b, t // tm),
        in_specs=[
            *x_specs,
            col(0), col(0), col(0), col(0), col(0), col(0), col(5),
            _const_spec((1, 256)), _const_spec((256, 256)), _const_spec((512, 256)),
            _layer_spec((256, 256), l), _layer_spec((D_MODEL, D_MODEL), l), _const_spec((1, D_MODEL)),
            *_mod_specs(2),
        ],
        out_specs=pl.BlockSpec((None, tm, D_MODEL), lambda b, i: (b, i, 0)),
        out_shape=jax.ShapeDtypeStruct((b, t, D_MODEL), F32),
        compiler_params=_cparams(("arbitrary", "arbitrary")),
        name="out_proj",
    )(*x_args, o_a, p, q, o_c, o_f, o_bk, u_f32, hg, hh, cs, w_fnet, w_out, g_post, mods, mods)


def _ffn_kernel(x_ref, g_ref, shc_ref, shb_ref, scc_ref, scb_ref, w1_ref, w2_ref, gp_ref, gtc_ref, gtb_ref,
                o_ref, *, with_ctx):
    fc = 256

    def swiglu(h):
        acc = jnp.zeros((h.shape[0], D_MODEL), F32)
        for c in range(D_FF // fc):
            a = _dot(h, w1_ref[:, c * fc:(c + 1) * fc])
            g = _dot(h, w1_ref[:, D_FF + c * fc:D_FF + (c + 1) * fc])
            act = (a * jax.nn.sigmoid(a) * g).astype(BF16)
            acc = acc + _dot(act, w2_ref[c * fc:(c + 1) * fc, :])
        return acc

    parts = list(enumerate(_row_parts(x_ref.shape[0], with_ctx)))
    pre = lambda part, r0, r1: _modulated(x_ref[r0:r1, :], g_ref, (shc_ref, shb_ref), (scc_ref, scb_ref), part, with_ctx)
    post = lambda part, r0, r1, acc: _gated_residual(x_ref[r0:r1, :], acc, gp_ref, (gtc_ref, gtb_ref), part, with_ctx)
    if with_ctx:
        acc = swiglu(jnp.concatenate([pre(part, r0, r1) for part, (r0, r1) in parts], axis=0))
        for part, (r0, r1) in parts:
            o_ref[r0:r1, :] = post(part, r0, r1, acc[r0:r1])
    else:
        for part, (r0, r1) in parts:
            o_ref[r0:r1, :] = post(part, r0, r1, swiglu(pre(part, r0, r1)))


def _ffn_call(xs, g_pre, w1, w2, l, g_post, mods, with_ctx):
    b, t, _ = xs.shape
    tm = TM_ALL if with_ctx else TM_LAT
    return pl.pallas_call(
        functools.partial(_ffn_kernel, with_ctx=with_ctx),
        grid=(b, t // tm),
        in_specs=[
            pl.BlockSpec((None, tm, D_MODEL), lambda b, i: (b, i, 0)),
            _const_spec((1, D_MODEL)),
            *_mod_specs(3), *_mod_specs(4),
            _layer_spec((D_MODEL, 2 * D_FF), l), _layer_spec((D_FF, D_MODEL), l), _const_spec((1, D_MODEL)),
            *_mod_specs(5),
        ],
        out_specs=pl.BlockSpec((None, tm, D_MODEL), lambda b, i: (b, i, 0)),
        out_shape=jax.ShapeDtypeStruct(xs.shape, F32),
        compiler_params=_cparams(("arbitrary", "arbitrary")),
        name="ffn",
    )(xs, g_pre, mods, mods, mods, mods, w1, w2, g_post, mods, mods)


def kernel(x, c, ctx, c_ctx, w_ada, b_ada, g_pre_mix, g_post_mix, g_pre_ffn, g_post_ffn, w_in, w_out,
           na_rpb, w_fnet, swa_sink, hgrn_lb_logits, hgrn_norm_g, w_ffn_in, w_ffn_out):
    tb = _tables()
    bf = lambda name: jnp.asarray(tb[name]).astype(BF16)
    cos, sin = jnp.asarray(tb["rope_cos"]), jnp.asarray(tb["rope_sin"])
    f64, g2, fc, chan_cs, hh = bf("f64"), bf("g_stage2"), bf("f_ctx"), bf("chan_cs"), bf("head_ones")
    ball, bblk, lmask = bf("hgrn_b"), bf("hgrn_bblk"), jnp.asarray(tb["hgrn_mask"])
    na_mask, swa_mask = jnp.asarray(tb["na_mask"]), jnp.asarray(tb["swa_mask"])

    cc = jnp.concatenate([c, c_ctx[None, :]], axis=0)
    mods = _ada_call(jnp.broadcast_to(cc[:, :, None], (3, D_MODEL, 128)), w_ada, b_ada)

    lb = jnp.cumsum(jax.nn.softmax(hgrn_lb_logits.astype(F32), axis=1), axis=1)
    lb = lb - lb[:, :1]
    gl_all = jnp.stack([jnp.log(lb), jnp.log1p(-lb)], axis=2)

    w_in_b, w_out_b, w_fnet_b = w_in.astype(BF16), w_out.astype(BF16), w_fnet.astype(BF16)
    w_ffn_in_b, w_ffn_out_b = w_ffn_in.astype(BF16), w_ffn_out.astype(BF16)

    xs = None
    for l in range(DEPTH):
        with_ctx = l < DEPTH - 1
        row = lambda a: a[l][None, :]
        x_all = (x, ctx) if l == 0 else (xs,)
        x_res = x_all if with_ctx else x_all[:1]
        u_att, u_f32 = _proj_call(x_all, row(g_pre_mix), mods[l], w_in_b, l, cos, sin)

        rpb = jnp.pad(na_rpb[l], ((0, 0), (NA_DR_PAD, NA_DR_ROWS - NA_DR_PAD - (2 * NA_WIN_R - 1)), (0, 0)))
        t2 = _na_bias_call(rpb[:, :, tb["na_lane_l"]], rpb[:, :, tb["na_lane_r"]])
        o_a, o_c = _attn_call(u_att, swa_sink[l], na_mask, t2, swa_mask, with_ctx)
        p, q = _fnet_call(u_f32, f64, g2, fc)
        o_f, o_bk = _hgrn_call(u_f32, gl_all[:, l], ball, lmask, bblk, lb_zero=(l == 0))

        xs = _outproj_call(x_res, o_a, p, q, o_c, o_f, o_bk, u_f32, row(hgrn_norm_g), hh, chan_cs,
                           w_fnet_b, w_out_b, l, row(g_post_mix), mods[l], with_ctx)
        xs = _ffn_call(xs, row(g_pre_ffn), w_ffn_in_b, w_ffn_out_b, l, row(g_post_ffn), mods[l], with_ctx)
    return xs
```

```python
import functools

import numpy as np
import jax
import jax.numpy as jnp
from jax import lax
from jax.experimental import pallas as pl
from jax.experimental.pallas import tpu as pltpu

F32, BF16 = jnp.float32, jnp.bfloat16

D_MODEL = 1024
SEQ = 8192
DEPTH = 2
GRID_W = 64
GRID_ROWS = SEQ // GRID_W
CTX = 256
HEAD_DIM = 64
NA_WIN_R, NA_WIN_C = 8, 16
SWA_WINDOW = 128
ROPE_THETA = 10000.0
EPS = 1e-6
D_FF = 2816
D_IN = 2816

TB = 256
T_ALL = SEQ + CTX
NBLK = T_ALL // TB
CTX_BLK = SEQ // TB
TM_ALL = 768
TM_LAT = 1024
NA_KROWS = 12
NA_KEYS = NA_KROWS * GRID_W
NA_DR_PAD = 4
NA_DR_ROWS = 24
SWA_KEYS = TB + 2 * SWA_WINDOW
CHUNK = 64
NCHUNK = TB // CHUNK
HGRN_FAST_RANGE = 60.0
Q_SCALE = HEAD_DIM ** -0.5
VMEM_LIMIT = 56 * 1024 * 1024

ATT_W, F32_W = 1280, 1536


def _dot(a, b):
    return jnp.dot(a, b, preferred_element_type=F32)


def _dot_nt(a, b):
    return lax.dot_general(a, b, (((1,), (1,)), ((), ())), preferred_element_type=F32)


def _dot_tn(a, b):
    return lax.dot_general(a, b, (((0,), (0,)), ((), ())), preferred_element_type=F32)


def _cparams(sem):
    return pltpu.CompilerParams(dimension_semantics=sem, vmem_limit_bytes=VMEM_LIMIT)


def _const_spec(shape):
    nd = len(shape)
    return pl.BlockSpec(shape, lambda *_: (0,) * nd, pipeline_mode=pl.Buffered(1))


def _layer_spec(shape, l):
    nd = len(shape)
    return pl.BlockSpec((None,) + shape, lambda *_: (l,) + (0,) * nd, pipeline_mode=pl.Buffered(1))


@functools.lru_cache(maxsize=None)
def _tables():
    t = {}
    lane = np.arange(128)
    d = lane % 64
    freq = ROPE_THETA ** (-(d % 16) / 16.0)
    pos = np.arange(SEQ)
    p = np.where(d[None, :] < 32, (pos // GRID_W)[:, None], (pos % GRID_W)[:, None]).astype(np.float64)
    ang = p * freq[None, :]
    sign = np.where((d % 32) < 16, -1.0, 1.0)
    cos = np.concatenate([np.cos(ang), np.ones((CTX, 128))], axis=0)
    sin = np.concatenate([np.sin(ang) * sign[None, :], np.zeros((CTX, 128))], axis=0)
    t["rope_cos"], t["rope_sin"] = cos.astype(np.float32), sin.astype(np.float32)

    masks = []
    a, cq = np.arange(4), np.arange(GRID_W)
    rho, ck = np.arange(NA_KROWS), np.arange(GRID_W)
    for r0 in (0, 8, GRID_ROWS - 4):
        kr0 = int(np.clip(r0 - 4, 0, GRID_ROWS - NA_KROWS))
        rq, rk = r0 + a, kr0 + rho
        start = np.clip(rq - NA_WIN_R // 2, 0, GRID_ROWS - NA_WIN_R)
        vr = (rk[None, :] >= start[:, None]) & (rk[None, :] < start[:, None] + NA_WIN_R)
        cs = np.clip(cq - NA_WIN_C // 2, 0, GRID_W - NA_WIN_C)
        vc = (ck[None, :] >= cs[:, None]) & (ck[None, :] < cs[:, None] + NA_WIN_C)
        val = vr[:, None, :, None] & vc[None, :, None, :]
        masks.append(np.where(val, 0.0, -np.inf).reshape(TB, NA_KEYS))
    masks.append(np.full((TB, NA_KEYS), -np.inf))
    t["na_mask"] = np.stack(masks).astype(np.float32)
    n = np.arange(128)
    clipc = lambda m: np.clip(m, -(NA_WIN_C - 1), NA_WIN_C - 1) + NA_WIN_C - 1
    t["na_lane_l"] = clipc(np.where(n < 64, n, n - 128)).astype(np.int32)
    t["na_lane_r"] = clipc(n - 64).astype(np.int32)

    rel = np.arange(SWA_KEYS)[None, :] - np.arange(TB)[:, None]
    swa = [np.where(np.abs(rel + off) <= SWA_WINDOW, 0.0, -np.inf) for off in (0, -SWA_WINDOW, -2 * SWA_WINDOW)]
    swa.append(np.full((TB, SWA_KEYS), -np.inf))
    t["swa_mask"] = np.stack(swa).astype(np.float32)

    def cs_tab(num, den):
        ang = 2.0 * np.pi * (num % den) / den
        return np.cos(ang), np.sin(ang)

    k1, t1 = np.arange(64)[:, None], np.arange(64)[None, :]
    c64, s64 = cs_tab(k1 * t1, 64)
    t["f64"] = np.concatenate([c64, -s64], axis=0).astype(np.float32)
    k2, t2 = np.arange(128)[:, None], np.arange(128)[None, :]
    g = []
    for kk in range(64):
        gc, gs = cs_tab((64 * k2 + kk) * t2, SEQ)
        g.append(np.block([[gc, gs], [-gs, gc]]))
    t["g_stage2"] = np.stack(g).astype(np.float32)
    kc, tc = np.arange(CTX)[:, None], np.arange(CTX)[None, :]
    cc, sc = cs_tab(kc * tc, CTX)
    t["f_ctx"] = np.concatenate([cc, -sc], axis=0).astype(np.float32)
    ch = np.arange(256)
    same = (ch[:, None] // 64) == (ch[None, :] // 64)
    c_ch, s_ch = cs_tab((ch[:, None] % 64) * (ch[None, :] % 64), 64)
    t["chan_cs"] = np.concatenate([np.where(same, c_ch, 0.0), np.where(same, s_ch, 0.0)], axis=0).astype(np.float32)
    t["head_ones"] = same.astype(np.float32)

    balls, lmasks, bblk = [], [], []
    for rev in (False, True):
        tau = (CHUNK - 1 - np.arange(CHUNK)) if rev else np.arange(CHUNK)
        tt, tj = tau[:, None], tau[None, :]
        blocks = [(tj <= tt), (tj > tt)]
        masks = [tt == tj]
        for h in (32, 16, 8, 4, 2, 1):
            up = ((tt // h) % 2) == 1
            mid = (tt // (2 * h)) * (2 * h) + h - 1
            blocks.append(np.where(up, (tj > mid) & (tj <= tt), (tj > tt) & (tj <= mid)))
            up_s = ((tj // h) % 2) == 1
            masks.append((tt // (2 * h) == tj // (2 * h)) & up & ~up_s)
        balls.append(np.concatenate(blocks, axis=0))
        lmasks.append(np.stack([np.tile(m, (4, 1)) for m in masks]))
        bblk.append(np.kron(np.eye(NCHUNK), blocks[0]))
    t["hgrn_b"] = np.stack(balls).astype(np.float32)
    t["hgrn_mask"] = np.stack(lmasks).astype(np.float32)
    t["hgrn_bblk"] = np.stack(bblk).astype(np.float32)
    return t


def _ada_kernel(cb_ref, w_ref, b_ref, o_ref, s_ref):
    tn = w_ref.shape[1]
    nrep = tn // 128

    @pl.when((pl.program_id(0) == 0) & (pl.program_id(1) == 0))
    def _():
        cb = cb_ref[...]
        s_ref[...] = cb * jax.nn.sigmoid(cb)

    def body(kc, accs):
        k0 = pl.multiple_of(kc * 8, 8)
        wt = w_ref[pl.ds(k0, 8), :]
        return tuple(accs[r] + wt * jnp.concatenate([s_ref[r, pl.ds(k0, 8), :]] * nrep, axis=1) for r in range(3))

    accs = lax.fori_loop(0, D_MODEL // 8, body, tuple(jnp.zeros((8, tn), F32) for _ in range(3)), unroll=8)
    for r in range(3):
        o_ref[r] = jnp.sum(accs[r], axis=0, keepdims=True) + b_ref[...]


def _ada_call(cb, w_ada, b_ada):
    tn = 1536
    return pl.pallas_call(
        _ada_kernel,
        grid=(DEPTH, 6 * D_MODEL // tn),
        in_specs=[
            pl.BlockSpec((3, D_MODEL, 128), lambda l, j: (0, 0, 0)),
            pl.BlockSpec((None, D_MODEL, tn), lambda l, j: (l, 0, j)),
            pl.BlockSpec((None, 1, tn), lambda l, j: (l, 0, j)),
        ],
        out_specs=pl.BlockSpec((None, 3, 1, tn), lambda l, j: (l, 0, 0, j)),
        out_shape=jax.ShapeDtypeStruct((DEPTH, 3, 1, 6 * D_MODEL), F32),
        scratch_shapes=[pltpu.VMEM((3, D_MODEL, 128), F32)],
        compiler_params=_cparams(("arbitrary", "arbitrary")),
        name="adaln",
    )(cb, w_ada, b_ada.reshape(DEPTH, 1, 6 * D_MODEL))


def _mod_specs(col):
    return [pl.BlockSpec((None, 1, D_MODEL), lambda b, i: (2, 0, col)),
            pl.BlockSpec((None, 1, D_MODEL), lambda b, i: (b, 0, col))]


def _row_parts(tm, with_ctx):
    return [(0, tm - CTX), (tm - CTX, tm)] if with_ctx else [(0, tm // 2), (tm // 2, tm)]


def _mod(mc_ref, mb_ref, part, with_ctx):
    if not with_ctx or part == 0:
        return mb_ref[...]
    last = pl.program_id(1) == pl.num_programs(1) - 1
    return jnp.where(last, mc_ref[...], mb_ref[...])


def _modulated(x, g_ref, sh_refs, sc_refs, part, with_ctx):
    gs = g_ref[...] * (1.0 + _mod(*sc_refs, part, with_ctx))
    return (_rms(x) * gs + _mod(*sh_refs, part, with_ctx)).astype(BF16)


def _gated_residual(x, y, gp_ref, gt_refs, part, with_ctx):
    return x + (_mod(*gt_refs, part, with_ctx) * gp_ref[...]) * _rms(y)


def _rms(x):
    return x * lax.rsqrt(jnp.mean(x * x, axis=-1, keepdims=True) + EPS)


def _x_specs(split):
    if not split:
        return [pl.BlockSpec((None, TM_ALL, D_MODEL), lambda b, i: (b, i, 0))]
    nx = SEQ // TB
    return [pl.BlockSpec((None, TB, D_MODEL), lambda b, i, j=j: (b, jnp.minimum(3 * i + j, nx - 1), 0))
            for j in range(TM_ALL // TB)] + [pl.BlockSpec((None, CTX, D_MODEL), lambda b, i: (b, 0, 0))]


def _x_part(x_refs, part, lo, hi):
    if len(x_refs) == 1:
        return x_refs[0][lo:hi, :]
    xa, xb, xc, cx = x_refs
    if part == 0:
        return jnp.concatenate([xa[...], xb[...]], axis=0)
    last = pl.program_id(1) == pl.num_programs(1) - 1
    return jnp.where(last, cx[...], xc[...])


def _proj_kernel(*refs, nx):
    g_ref, shc_ref, shb_ref, scc_ref, scb_ref, w_ref, cos_ref, sin_ref, att_ref, f32_ref = refs[nx:]
    for part, (lo, hi) in enumerate(_row_parts(TM_ALL, True)):
        rows = slice(lo, hi)
        h = _modulated(_x_part(refs[:nx], part, lo, hi), g_ref, (shc_ref, shb_ref), (scc_ref, scb_ref), part, True)
        cos, sin = cos_ref[rows, :], sin_ref[rows, :]
        first = (lax.broadcasted_iota(jnp.int32, (hi - lo, 128), 1) % 32) < 16

        def rope(v):
            return v * cos + jnp.where(first, pltpu.roll(v, 112, 1), pltpu.roll(v, 16, 1)) * sin

        a = _dot(h, w_ref[:, 0:768])
        att_ref[rows, 0:256] = (a[:, 0:256] * Q_SCALE).astype(BF16)
        att_ref[rows, 256:768] = a[:, 256:768].astype(BF16)
        f32_ref[rows, 0:256] = _dot(h, w_ref[:, 768:1024])
        s = _dot(h, w_ref[:, 1024:1536])
        att_ref[rows, 768:896] = (rope(s[:, 0:128]) * Q_SCALE).astype(BF16)
        att_ref[rows, 896:1024] = (rope(s[:, 128:256]) * Q_SCALE).astype(BF16)
        att_ref[rows, 1024:1152] = rope(s[:, 256:384]).astype(BF16)
        att_ref[rows, 1152:1280] = s[:, 384:512].astype(BF16)
        f32_ref[rows, 256:1536] = _dot(h, w_ref[:, 1536:2816])


def _proj_call(x_parts, g, mods, w_in, l, cos, sin):
    b = x_parts[0].shape[0]
    tm = TM_ALL
    split = len(x_parts) == 2
    x_args = (x_parts[0],) * 3 + (x_parts[1],) if split else x_parts
    return pl.pallas_call(
        functools.partial(_proj_kernel, nx=len(x_args)),
        grid=(b, T_ALL // tm),
        in_specs=[
            *_x_specs(split),
            _const_spec((1, D_MODEL)),
            *_mod_specs(0), *_mod_specs(1),
            _layer_spec((D_MODEL, D_IN), l),
            pl.BlockSpec((tm, 128), lambda b, i: (i, 0)),
            pl.BlockSpec((tm, 128), lambda b, i: (i, 0)),
        ],
        out_specs=[
            pl.BlockSpec((None, tm, ATT_W), lambda b, i: (b, i, 0)),
            pl.BlockSpec((None, tm, F32_W), lambda b, i: (b, i, 0)),
        ],
        out_shape=[jax.ShapeDtypeStruct((b, T_ALL, ATT_W), BF16), jax.ShapeDtypeStruct((b, T_ALL, F32_W), F32)],
        compiler_params=_cparams(("arbitrary", "arbitrary")),
        name="proj_in",
    )(*x_args, g, mods, mods, mods, mods, w_in, cos, sin)


def _na_bias_kernel(vl_ref, vr_ref, o_ref):
    lane = lax.broadcasted_iota(jnp.int32, (GRID_W, 128), 1)
    for e in range(NA_DR_ROWS - 1):
        left = pltpu.roll(jnp.broadcast_to(vl_ref[e:e + 1, :], (GRID_W, 128)), 0, 1, stride=1, stride_axis=0)
        right = pltpu.roll(jnp.broadcast_to(vr_ref[e + 1:e + 2, :], (GRID_W, 128)), 0, 1, stride=1, stride_axis=0)
        o_ref[e] = jnp.where(lane < 64, left, right)


def _na_bias_call(vl, vr):
    spec = pl.BlockSpec((None, NA_DR_ROWS, 128), lambda h: (h, 0, 0))
    return pl.pallas_call(
        _na_bias_kernel,
        grid=(4,),
        in_specs=[spec, spec],
        out_specs=pl.BlockSpec((None, NA_DR_ROWS - 1, GRID_W, 128), lambda h: (h, 0, 0, 0)),
        out_shape=jax.ShapeDtypeStruct((4, NA_DR_ROWS - 1, GRID_W, 128), F32),
        compiler_params=_cparams(("arbitrary",)),
        name="na_bias",
    )(vl, vr)


AT_ROWS = 16


def _attn_block_variant(b, i):
    return (jnp.where(i == CTX_BLK, 3, jnp.where(i == 0, 0, jnp.where(i == CTX_BLK - 1, 2, 1))), 0, 0)


def _lane_half(shape, j):
    lane = lax.broadcasted_iota(jnp.int32, shape, 1)
    return (lane >= 64 * j) & (lane < 64 * (j + 1))


def _na_window(i):
    is_ctx = i == CTX_BLK
    r0 = i * 4
    kr0 = jnp.where(is_ctx, 0, jnp.clip(r0 - 4, 0, GRID_ROWS - NA_KROWS))
    delta = jnp.where(is_ctx, 3, kr0 - r0 + NA_WIN_R - 1) + NA_DR_PAD
    return pl.multiple_of(kr0 * GRID_W, GRID_W), delta


def _na_scores(i, q_ref, k_ref, s_ref):
    kstart, _ = _na_window(i)
    for hp in range(2):
        cs = slice(hp * 128, (hp + 1) * 128)
        q2 = q_ref[:, cs]
        kw, kc = k_ref[pl.ds(kstart, NA_KEYS), cs], k_ref[SEQ:T_ALL, cs]
        qm = jnp.concatenate([jnp.where(_lane_half((TB, 128), j), q2, jnp.zeros_like(q2)) for j in range(2)], axis=0)
        s_ref[hp, :, 0:CTX] = _dot_nt(qm, kc)
        s_ref[hp, :, CTX:] = _dot_nt(qm, kw)


def _na_softmax(i, s_ref, mask_ref, t2_ref, p_ref):
    _, delta = _na_window(i)
    for hp in range(2):
        for r in range(2 * TB // AT_ROWS):
            rows = slice(r * AT_ROWS, (r + 1) * AT_ROWS)
            j, rq = divmod(r, TB // AT_ROWS)
            a, sub = divmod(rq, GRID_W // AT_ROWS)
            trow = slice(sub * AT_ROWS, (sub + 1) * AT_ROWS)
            bias = jnp.concatenate(
                [t2_ref[2 * hp + j, delta + 2 * rp - a, trow, :] for rp in range(NA_KROWS // 2)], axis=1)
            s_nb = s_ref[hp, rows, CTX:] + bias + mask_ref[rq * AT_ROWS:(rq + 1) * AT_ROWS, :]
            s_c = s_ref[hp, rows, 0:CTX]
            m = jnp.maximum(jnp.max(s_nb, axis=-1, keepdims=True), jnp.max(s_c, axis=-1, keepdims=True))
            p_ref[hp, rows, 0:CTX] = jnp.exp((s_c - m).astype(BF16))
            p_ref[hp, rows, CTX:] = jnp.exp((s_nb - m).astype(BF16))


def _na_values(i, p_ref, v_ref, o_ref):
    kstart, _ = _na_window(i)
    lane = lax.broadcasted_iota(jnp.int32, (TB, 128), 1)
    for hp in range(2):
        cs = slice(hp * 128, (hp + 1) * 128)
        vw, vc = v_ref[pl.ds(kstart, NA_KEYS), cs], v_ref[SEQ:T_ALL, cs]
        o2 = (_dot(p_ref[hp, :, 0:CTX], jnp.concatenate([vc, jnp.ones_like(vc)], axis=1))
              + _dot(p_ref[hp, :, CTX:], jnp.concatenate([vw, jnp.ones_like(vw)], axis=1)))
        top = o2[0:TB, 0:128] * (1.0 / o2[0:TB, 128:256])
        bot = o2[TB:2 * TB, 0:128] * (1.0 / o2[TB:2 * TB, 128:256])
        o_ref[:, cs] = jnp.where(lane < 64, top, bot).astype(BF16)


def _swa_window(i):
    return pl.multiple_of(jnp.clip(i * TB - SWA_WINDOW, 0, SEQ - SWA_KEYS), SWA_WINDOW)


def _swa_scores(i, q_ref, k_ref, s_ref):
    kw, kc = k_ref[pl.ds(_swa_window(i), SWA_KEYS), :], k_ref[SEQ:T_ALL, :]
    for g in range(2):
        q2 = q_ref[:, g * 128:(g + 1) * 128].astype(F32)
        qa = jnp.concatenate([jnp.where(_lane_half((TB, 128), g), q2 if j == g else pltpu.roll(q2, 64, 1), 0.0)
                              for j in range(2)], axis=0).astype(BF16)
        s_ref[g, :, 0:CTX] = _dot_nt(qa, kc)
        s_ref[g, :, CTX:] = _dot_nt(qa, kw)


def _swa_softmax(sink_ref, s_ref, mask_ref, p_ref, esink_ref):
    for g in range(2):
        for r in range(2 * TB // AT_ROWS):
            rows = slice(r * AT_ROWS, (r + 1) * AT_ROWS)
            j, rq = divmod(r, TB // AT_ROWS)
            sink = sink_ref[2 * g + j]
            s_l = s_ref[g, rows, CTX:] + mask_ref[rq * AT_ROWS:(rq + 1) * AT_ROWS, :]
            s_c = s_ref[g, rows, 0:CTX]
            m = jnp.maximum(jnp.maximum(jnp.max(s_l, axis=-1, keepdims=True),
                                        jnp.max(s_c, axis=-1, keepdims=True)), sink)
            p_ref[g, rows, 0:CTX] = jnp.exp((s_c - m).astype(BF16))
            p_ref[g, rows, CTX:] = jnp.exp((s_l - m).astype(BF16))
            esink_ref[g, rows, :] = jnp.exp(jnp.broadcast_to(sink - m, (AT_ROWS, 128)))


def _swa_values(i, p_ref, esink_ref, v_ref, o_ref):
    vw, vc = v_ref[pl.ds(_swa_window(i), SWA_KEYS), :], v_ref[SEQ:T_ALL, :]
    lane = lax.broadcasted_iota(jnp.int32, (TB, 128), 1)
    vc2 = jnp.concatenate([vc, jnp.ones_like(vc)], axis=1)
    vw2 = jnp.concatenate([vw, jnp.ones_like(vw)], axis=1)
    for g in range(2):
        o2 = _dot(p_ref[g, :, 0:CTX], vc2) + _dot(p_ref[g, :, CTX:], vw2)
        o = o2[:, 0:128] * (1.0 / (o2[:, 128:256] + esink_ref[g]))
        outs = [o[j * TB:(j + 1) * TB] if j == g else pltpu.roll(o[j * TB:(j + 1) * TB], 64, 1) for j in range(2)]
        o_ref[:, g * 128:(g + 1) * 128] = jnp.where(lane < 64, outs[0], outs[1]).astype(BF16)


def _attn_kernel(sink_ref, qn_ref, kn_ref, vn_ref, maskn_ref, t2_ref, qs_ref, ks_ref, vs_ref, masks_ref,
                 on_ref, os_ref, sn_ref, pn_ref, ss_ref, ps_ref, esink_ref):
    i = pl.program_id(1)
    _na_scores(i, qn_ref, kn_ref, sn_ref)
    _swa_scores(i, qs_ref, ks_ref, ss_ref)
    _na_softmax(i, sn_ref, maskn_ref, t2_ref, pn_ref)
    _swa_softmax(sink_ref, ss_ref, masks_ref, ps_ref, esink_ref)
    _na_values(i, pn_ref, vn_ref, on_ref)
    _swa_values(i, ps_ref, esink_ref, vs_ref, os_ref)


def _attn_call(u_att, sink, na_mask, t2, swa_mask, with_ctx):
    b = u_att.shape[0]
    nb = NBLK if with_ctx else CTX_BLK
    blk = lambda width, col: pl.BlockSpec((None, TB, width), lambda b, i: (b, i, col))
    full = lambda width, col: pl.BlockSpec((None, T_ALL, width), lambda b, i: (b, 0, col))
    out = pl.BlockSpec((None, TB, 256), lambda b, i: (b, i, 0))
    return pl.pallas_call(
        _attn_kernel,
        grid=(b, nb),
        in_specs=[
            pl.BlockSpec(memory_space=pltpu.SMEM),
            blk(256, 0), full(256, 1), full(256, 2),
            pl.BlockSpec((None, TB, NA_KEYS), _attn_block_variant),
            _const_spec((4, NA_DR_ROWS - 1, GRID_W, 128)),
            blk(256, 3), full(128, 8), full(128, 9),
            pl.BlockSpec((None, TB, SWA_KEYS), _attn_block_variant),
        ],
        out_specs=[out, out],
        out_shape=[jax.ShapeDtypeStruct((b, nb * TB, 256), BF16)] * 2,
        scratch_shapes=[pltpu.VMEM((2, 2 * TB, CTX + NA_KEYS), F32), pltpu.VMEM((2, 2 * TB, CTX + NA_KEYS), BF16),
                        pltpu.VMEM((2, 2 * TB, CTX + SWA_KEYS), F32), pltpu.VMEM((2, 2 * TB, CTX + SWA_KEYS), BF16),
                        pltpu.VMEM((2, 2 * TB, 128), F32)],
        compiler_params=_cparams(("arbitrary", "arbitrary")),
        name="attn",
    )(sink, u_att, u_att, u_att, na_mask, t2, u_att, u_att, u_att, swa_mask)


FN_XP = 136
FN_YP = 136
FN_ZP = 72


def _fnet_kernel(x_ref, f64_ref, g_ref, fc_ref, p_ref, q_ref, xz_ref, zq_ref, y_ref):
    sc_lat, sc_ctx = (64.0 * SEQ) ** -0.5, (64.0 * CTX) ** -0.5
    pq = _dot(fc_ref[...], x_ref[SEQ:T_ALL, :].astype(BF16))
    p_ref[SEQ:T_ALL, :] = (pq[0:CTX] * sc_ctx).astype(BF16)
    q_ref[SEQ:T_ALL, :] = (pq[CTX:2 * CTX] * sc_ctx).astype(BF16)
    f64 = f64_ref[...]

    def repitch(t1, carry):
        xz_ref[pl.ds(pl.multiple_of(t1 * FN_XP, 8), 128), :] = x_ref[pl.ds(pl.multiple_of(t1 * 128, 128), 128), :]
        return carry

    lax.fori_loop(0, 64, repitch, 0, unroll=4)

    def stage1(t2, carry):
        xs = xz_ref[pl.ds(t2, 64, stride=FN_XP), :].astype(BF16)
        y_ref[pl.ds(t2, 128, stride=FN_YP), :] = _dot(f64, xs)
        return carry

    lax.fori_loop(0, 128, stage1, 0, unroll=16)

    def stage2(k1, carry):
        yre = y_ref[pl.ds(pl.multiple_of(k1 * FN_YP, 8), 128), :]
        yim = y_ref[pl.ds(pl.multiple_of((64 + k1) * FN_YP, 8), 128), :]
        ycat = jnp.concatenate([yre, yim], axis=0).astype(BF16)
        pq = _dot(g_ref[k1], ycat)
        xz_ref[pl.ds(k1, 128, stride=FN_ZP), :] = pq[0:128] * sc_lat
        zq_ref[pl.ds(k1, 128, stride=FN_ZP), :] = pq[128:256] * sc_lat
        return carry

    lax.fori_loop(0, 64, stage2, 0, unroll=16)

    def emit(k2, carry):
        src = pl.ds(pl.multiple_of(k2 * FN_ZP, 8), 64)
        dst = pl.ds(pl.multiple_of(k2 * 64, 64), 64)
        p_ref[dst, :] = xz_ref[src, :].astype(BF16)
        q_ref[dst, :] = zq_ref[src, :].astype(BF16)
        return carry

    lax.fori_loop(0, 128, emit, 0, unroll=4)


def _fnet_call(u_f32, f64, g2, fc):
    b = u_f32.shape[0]
    spec = pl.BlockSpec((None, T_ALL, 128), lambda b, j: (b, 0, j))
    return pl.pallas_call(
        _fnet_kernel,
        grid=(b, 2),
        in_specs=[spec, _const_spec((128, 64)), _const_spec((64, 256, 256)), _const_spec((2 * CTX, CTX))],
        out_specs=[spec, spec],
        out_shape=[jax.ShapeDtypeStruct((b, T_ALL, 256), BF16)] * 2,
        scratch_shapes=[pltpu.VMEM((128 * FN_ZP, 128), F32), pltpu.VMEM((128 * FN_ZP, 128), F32),
                        pltpu.VMEM((128 * FN_YP, 128), F32)],
        compiler_params=_cparams(("arbitrary", "arbitrary")),
        name="fnet_dft",
    )(u_f32, f64, g2, fc)


def _hgrn_gates(qd, fz, gl, lb_zero):
    q = qd * jax.nn.sigmoid(qd)
    sp = jnp.log(1.0 + jnp.exp(-jnp.abs(fz)))
    ls = jnp.minimum(fz, 0.0) - sp
    lk = jnp.minimum(-fz, 0.0) - sp
    if lb_zero:
        return q, jnp.exp(lk), ls
    b_ = gl[1:2] + ls
    a_ = gl[0:1]
    logf = jnp.maximum(a_, b_) + jnp.log(1.0 + jnp.exp(-jnp.abs(a_ - b_)))
    return q, jnp.exp(gl[1:2] + lk), logf


def _dot01(b01, x):
    h1 = x.astype(BF16)
    r1 = x - h1.astype(F32)
    h2 = r1.astype(BF16)
    h3 = (r1 - h2.astype(F32)).astype(BF16)
    return _dot(b01, h1) + _dot(b01, h2) + _dot(b01, h3)


def _head_masks(rows):
    lane = lax.broadcasted_iota(jnp.int32, (rows, 256), 1)
    return [(lane >= 64 * h) & (lane < 64 * (h + 1)) for h in range(4)]


def _stack_heads(w, hms):
    return jnp.concatenate([jnp.where(hm, w, jnp.zeros_like(w)) for hm in hms], axis=0)


def _state_update(st_ref, st, e_last, vb, ks):
    ktv = _dot_tn(vb, ks)
    r2 = lax.broadcasted_iota(jnp.int32, (256, 256), 0) // 64
    c2 = lax.broadcasted_iota(jnp.int32, (256, 256), 1) // 64
    st_ref[...] = st * e_last + jnp.where(r2 == c2, ktv, 0.0)


def _hgrn_chunk_exact(qd, fz, v, st_ref, gl, ball, lmask_ref, d, rev, lb_zero):
    q, kk, logf = _hgrn_gates(qd, fz, gl, lb_zero)
    e = jnp.exp(_dot01(ball, logf))
    row = lax.broadcasted_iota(jnp.int32, (CHUNK, 256), 0)
    tau = (CHUNK - 1 - row) if rev else row
    hms = _head_masks(CHUNK)
    att = _dot_nt(_stack_heads(q.astype(BF16), hms), kk.astype(BF16)) * lmask_ref[d, 0]
    for li, sh in enumerate((5, 4, 3, 2, 1, 0)):
        up = ((tau >> sh) & 1) == 1
        w = (jnp.where(up, q, kk) * e[128 + 64 * li:192 + 64 * li]).astype(BF16)
        att = att + _dot_nt(_stack_heads(w, hms), w) * lmask_ref[d, 1 + li]
    vb = v.astype(BF16)
    st = st_ref[...]
    o = _dot_nt((q * e[0:64]).astype(BF16), st.astype(BF16))
    for h in range(4):
        o = o + _dot(att[64 * h:64 * (h + 1)].astype(BF16), jnp.where(hms[h], vb, jnp.zeros_like(vb)))
    tl = 0 if rev else CHUNK - 1
    _state_update(st_ref, st, e[tl:tl + 1], vb, (kk * e[64:128]).astype(BF16))
    return o


def _hgrn_chunk_prep(sl, q_ref, f_ref, gl, tri, rev, lb_zero):
    q, kk, logf = _hgrn_gates(q_ref[sl, :], f_ref[sl, :], gl, lb_zero)
    cum = _dot01(tri, logf)
    first_row, mid_row, last_row = (CHUNK - 1, CHUNK // 2, 0) if rev else (0, CHUNK // 2 - 1, CHUNK - 1)
    cum_mid, cum_last = cum[mid_row:mid_row + 1], cum[last_row:last_row + 1]
    a = cum - cum_mid
    a_range = jnp.maximum(jnp.max(a[first_row:first_row + 1]), jnp.max(-a[last_row:last_row + 1]))
    return q * jnp.exp(a), kk * jnp.exp(-a), cum_mid, cum_last, a_range


def _hgrn_chunk_fast(sl, qt, kt, cum_mid, cum_last, v_ref, st_ref, o_ref, rev):
    hms = _head_masks(CHUNK)
    r = lax.broadcasted_iota(jnp.int32, (4 * CHUNK, CHUNK), 0) & (CHUNK - 1)
    s = lax.broadcasted_iota(jnp.int32, (4 * CHUNK, CHUNK), 1)
    causal = (s >= r) if rev else (s <= r)
    vb = v_ref[sl, :].astype(BF16)
    att = jnp.where(causal, _dot_nt(_stack_heads(qt.astype(BF16), hms), kt.astype(BF16)), 0.0)
    st = st_ref[...]
    o = _dot_nt((qt * jnp.exp(cum_mid)).astype(BF16), st.astype(BF16))
    for h in range(4):
        o = o + _dot(att[64 * h:64 * (h + 1)].astype(BF16), jnp.where(hms[h], vb, jnp.zeros_like(vb)))
    o_ref[sl, :] = o
    _state_update(st_ref, st, jnp.exp(cum_last), vb, (kt * jnp.exp(cum_last - cum_mid)).astype(BF16))


def _hgrn_kernel(qf_ref, ff_ref, vf_ref, qb_ref, fb_ref, vb_ref, gl_ref, ball_ref, lmask_ref, bblk_ref,
                 of_ref, ob_ref, sf_ref, sb_ref, sf0_ref, sb0_ref, *, lb_zero):
    @pl.when(pl.program_id(1) == 0)
    def _():
        sf_ref[...] = jnp.zeros_like(sf_ref)
        sb_ref[...] = jnp.zeros_like(sb_ref)

    sf0_ref[...] = sf_ref[...]
    sb0_ref[...] = sb_ref[...]
    dirs = ((qf_ref, ff_ref, vf_ref, sf_ref, of_ref, False), (qb_ref, fb_ref, vb_ref, sb_ref, ob_ref, True))
    rows = lambda rev, s: slice((NCHUNK - 1 - s if rev else s) * CHUNK, (NCHUNK - s if rev else s + 1) * CHUNK)

    def prep(s):
        return [_hgrn_chunk_prep(rows(rev, s), q_ref, f_ref, gl_ref[d], ball_ref[d, 0:CHUNK, :], rev, lb_zero)
                for d, (q_ref, f_ref, v_ref, st_ref, o_ref, rev) in enumerate(dirs)]

    rng, ready = 0.0, prep(0)
    for s in range(NCHUNK):
        nxt = prep(s + 1) if s + 1 < NCHUNK else None
        for d, (q_ref, f_ref, v_ref, st_ref, o_ref, rev) in enumerate(dirs):
            qt, kt, cum_mid, cum_last, amax = ready[d]
            rng = jnp.maximum(rng, amax)
            _hgrn_chunk_fast(rows(rev, s), qt, kt, cum_mid, cum_last, v_ref, st_ref, o_ref, rev)
        ready = nxt

    @pl.when(jnp.logical_not(rng <= HGRN_FAST_RANGE))
    def _():
        sf_ref[...] = sf0_ref[...]
        sb_ref[...] = sb0_ref[...]

        def body(c, carry):
            rf = pl.ds(pl.multiple_of(c * CHUNK, CHUNK), CHUNK)
            rb = pl.ds(pl.multiple_of((NCHUNK - 1 - c) * CHUNK, CHUNK), CHUNK)
            of_ref[rf, :] = _hgrn_chunk_exact(qf_ref[rf, :], ff_ref[rf, :], vf_ref[rf, :], sf_ref,
                                              gl_ref[0], ball_ref[0], lmask_ref, 0, False, lb_zero)
            ob_ref[rb, :] = _hgrn_chunk_exact(qb_ref[rb, :], fb_ref[rb, :], vb_ref[rb, :], sb_ref,
                                              gl_ref[1], ball_ref[1], lmask_ref, 1, True, lb_zero)
            return carry

        lax.fori_loop(0, NCHUNK, body, 0)


def _hgrn_call(u_f32, gl, ball, lmask, bblk, lb_zero):
    b = u_f32.shape[0]

    def fwd(col):
        return pl.BlockSpec((None, TB, 256), lambda b, i: (b, jnp.where(i == 0, CTX_BLK, i - 1), col))

    def bwd(col):
        return pl.BlockSpec((None, TB, 256), lambda b, i: (b, jnp.where(i == 0, CTX_BLK, CTX_BLK - i), col))

    return pl.pallas_call(
        functools.partial(_hgrn_kernel, lb_zero=lb_zero),
        grid=(b, NBLK),
        in_specs=[fwd(1), fwd(2), fwd(4), bwd(1), bwd(3), bwd(4),
                  _const_spec((2, 2, 256)), _const_spec((2, 8 * CHUNK, CHUNK)),
                  _const_spec((2, 7, 4 * CHUNK, CHUNK)), _const_spec((2, TB, TB))],
        out_specs=[fwd(0), bwd(0)],
        out_shape=[jax.ShapeDtypeStruct((b, T_ALL, 256), F32)] * 2,
        scratch_shapes=[pltpu.VMEM((256, 256), F32)] * 4,
        compiler_params=_cparams(("arbitrary", "arbitrary")),
        name="hgrn_scan",
    )(u_f32, u_f32, u_f32, u_f32, u_f32, u_f32, gl, ball, lmask, bblk)


def _outproj_kernel(*refs, nx, with_ctx):
    (oa_ref, p_ref, q_ref, oc_ref, of_ref, ob_ref, gd_ref, hg_ref, hh_ref, cs_ref,
     wf_ref, wo_ref, gp_ref, gtc_ref, gtb_ref, o_ref) = refs[nx:]
    for part, (r0, r1) in enumerate(_row_parts(o_ref.shape[0], with_ctx)):
        rows = slice(r0, r1)
        z = _dot(p_ref[rows, :], cs_ref[0:256, :]) + _dot(q_ref[rows, :], cs_ref[256:512, :])
        o_b = _dot(z.astype(BF16), wf_ref[...])
        od = of_ref[rows, :] + ob_ref[rows, :]
        sq = od * od
        hi = sq.astype(BF16)
        lo = (sq - hi.astype(F32)).astype(BF16)
        ms = (_dot(hi, hh_ref[...]) + _dot(lo, hh_ref[...])) * (1.0 / HEAD_DIM)
        gd = gd_ref[rows, :]
        o_d = od * lax.rsqrt(ms + EPS) * hg_ref[...] * (gd * jax.nn.sigmoid(gd))
        y = (_dot(oa_ref[rows, :], wo_ref[0:256, :]) + _dot(o_b.astype(BF16), wo_ref[256:512, :])
             + _dot(oc_ref[rows, :], wo_ref[512:768, :]) + _dot(o_d.astype(BF16), wo_ref[768:1024, :]))
        x = _x_part(refs[:nx], part, r0, r1)
        o_ref[rows, :] = _gated_residual(x, y, gp_ref, (gtc_ref, gtb_ref), part, with_ctx)


def _outproj_call(x_parts, o_a, p, q, o_c, o_f, o_bk, u_f32, hg, hh, cs, w_fnet, w_out, l, g_post, mods, with_ctx):
    b = x_parts[0].shape[0]
    tm, t = (TM_ALL, T_ALL) if with_ctx else (TM_LAT, SEQ)
    col = lambda c: pl.BlockSpec((None, tm, 256), lambda b, i: (b, i, c))
    split = len(x_parts) == 2
    x_args = (x_parts[0],) * 3 + (x_parts[1],) if split else x_parts
    x_specs = _x_specs(True) if split else [pl.BlockSpec((None, tm, D_MODEL), lambda b, i: (b, i, 0))]
    return pl.pallas_call(
        functools.partial(_outproj_kernel, nx=len(x_args), with_ctx=with_ctx),
        grid=(b, t // tm),
        in_specs=[
            *x_specs,
            col(0), col(0), col(0), col(0), col(0), col(0), col(5),
            _const_spec((1, 256)), _const_spec((256, 256)), _const_spec((512, 256)),
            _layer_spec((256, 256), l), _layer_spec((D_MODEL, D_MODEL), l), _const_spec((1, D_MODEL)),
            *_mod_specs(2),
        ],
        out_specs=pl.BlockSpec((None, tm, D_MODEL), lambda b, i: (b, i, 0)),
        out_shape=jax.ShapeDtypeStruct((b, t, D_MODEL), F32),
        compiler_params=_cparams(("arbitrary", "arbitrary")),
        name="out_proj",
    )(*x_args, o_a, p, q, o_c, o_f, o_bk, u_f32, hg, hh, cs, w_fnet, w_out, g_post, mods, mods)


def _ffn_kernel(x_ref, g_ref, shc_ref, shb_ref, scc_ref, scb_ref, w1_ref, w2_ref, gp_ref, gtc_ref, gtb_ref,
                o_ref, *, with_ctx):
    fc = 256

    def swiglu(h):
        acc = jnp.zeros((h.shape[0], D_MODEL), F32)
        for c in range(D_FF // fc):
            a = _dot(h, w1_ref[:, c * fc:(c + 1) * fc])
            g = _dot(h, w1_ref[:, D_FF + c * fc:D_FF + (c + 1) * fc])
            act = (a * jax.nn.sigmoid(a) * g).astype(BF16)
            acc = acc + _dot(act, w2_ref[c * fc:(c + 1) * fc, :])
        return acc

    parts = list(enumerate(_row_parts(x_ref.shape[0], with_ctx)))
    pre = lambda part, r0, r1: _modulated(x_ref[r0:r1, :], g_ref, (shc_ref, shb_ref), (scc_ref, scb_ref), part, with_ctx)
    post = lambda part, r0, r1, acc: _gated_residual(x_ref[r0:r1, :], acc, gp_ref, (gtc_ref, gtb_ref), part, with_ctx)
    if with_ctx:
        acc = swiglu(jnp.concatenate([pre(part, r0, r1) for part, (r0, r1) in parts], axis=0))
        for part, (r0, r1) in parts:
            o_ref[r0:r1, :] = post(part, r0, r1, acc[r0:r1])
    else:
        for part, (r0, r1) in parts:
            o_ref[r0:r1, :] = post(part, r0, r1, swiglu(pre(part, r0, r1)))


def _ffn_call(xs, g_pre, w1, w2, l, g_post, mods, with_ctx):
    b, t, _ = xs.shape
    tm = TM_ALL if with_ctx else TM_LAT
    return pl.pallas_call(
        functools.partial(_ffn_kernel, with_ctx=with_ctx),
        grid=(b, t // tm),
        in_specs=[
            pl.BlockSpec((None, tm, D_MODEL), lambda b, i: (b, i, 0)),
            _const_spec((1, D_MODEL)),
            *_mod_specs(3), *_mod_specs(4),
            _layer_spec((D_MODEL, 2 * D_FF), l), _layer_spec((D_FF, D_MODEL), l), _const_spec((1, D_MODEL)),
            *_mod_specs(5),
        ],
        out_specs=pl.BlockSpec((None, tm, D_MODEL), lambda b, i: (b, i, 0)),
        out_shape=jax.ShapeDtypeStruct(xs.shape, F32),
        compiler_params=_cparams(("arbitrary", "arbitrary")),
        name="ffn",
    )(xs, g_pre, mods, mods, mods, mods, w1, w2, g_post, mods, mods)


def kernel(x, c, ctx, c_ctx, w_ada, b_ada, g_pre_mix, g_post_mix, g_pre_ffn, g_post_ffn, w_in, w_out,
           na_rpb, w_fnet, swa_sink, hgrn_lb_logits, hgrn_norm_g, w_ffn_in, w_ffn_out):
    tb = _tables()
    bf = lambda name: jnp.asarray(tb[name]).astype(BF16)
    cos, sin = jnp.asarray(tb["rope_cos"]), jnp.asarray(tb["rope_sin"])
    f64, g2, fc, chan_cs, hh = bf("f64"), bf("g_stage2"), bf("f_ctx"), bf("chan_cs"), bf("head_ones")
    ball, bblk, lmask = bf("hgrn_b"), bf("hgrn_bblk"), jnp.asarray(tb["hgrn_mask"])
    na_mask, swa_mask = jnp.asarray(tb["na_mask"]), jnp.asarray(tb["swa_mask"])

    cc = jnp.concatenate([c, c_ctx[None, :]], axis=0)
    mods = _ada_call(jnp.broadcast_to(cc[:, :, None], (3, D_MODEL, 128)), w_ada, b_ada)

    lb = jnp.cumsum(jax.nn.softmax(hgrn_lb_logits.astype(F32), axis=1), axis=1)
    lb = lb - lb[:, :1]
    gl_all = jnp.stack([jnp.log(lb), jnp.log1p(-lb)], axis=2)

    w_in_b, w_out_b, w_fnet_b = w_in.astype(BF16), w_out.astype(BF16), w_fnet.astype(BF16)
    w_ffn_in_b, w_ffn_out_b = w_ffn_in.astype(BF16), w_ffn_out.astype(BF16)

    xs = None
    for l in range(DEPTH):
        with_ctx = l < DEPTH - 1
        row = lambda a: a[l][None, :]
        x_all = (x, ctx) if l == 0 else (xs,)
        x_res = x_all if with_ctx else x_all[:1]
        u_att, u_f32 = _proj_call(x_all, row(g_pre_mix), mods[l], w_in_b, l, cos, sin)

        rpb = jnp.pad(na_rpb[l], ((0, 0), (NA_DR_PAD, NA_DR_ROWS - NA_DR_PAD - (2 * NA_WIN_R - 1)), (0, 0)))
        t2 = _na_bias_call(rpb[:, :, tb["na_lane_l"]], rpb[:, :, tb["na_lane_r"]])
        o_a, o_c = _attn_call(u_att, swa_sink[l], na_mask, t2, swa_mask, with_ctx)
        p, q = _fnet_call(u_f32, f64, g2, fc)
        o_f, o_bk = _hgrn_call(u_f32, gl_all[:, l], ball, lmask, bblk, lb_zero=(l == 0))

        xs = _outproj_call(x_res, o_a, p, q, o_c, o_f, o_bk, u_f32, row(hgrn_norm_g), hh, chan_cs,
                           w_fnet_b, w_out_b, l, row(g_post_mix), mods[l], with_ctx)
        xs = _ffn_call(xs, row(g_pre_ffn), w_ffn_in_b, w_ffn_out_b, l, row(g_post_ffn), mods[l], with_ctx)
    return xs
```

```python
import functools

import numpy as np
import jax
import jax.numpy as jnp
from jax import lax
from jax.experimental import pallas as pl
from jax.experimental.pallas import tpu as pltpu

F32, BF16 = jnp.float32, jnp.bfloat16

D_MODEL = 1024
SEQ = 8192
DEPTH = 2
GRID_W = 64
GRID_ROWS = SEQ // GRID_W
CTX = 256
HEAD_DIM = 64
NA_WIN_R, NA_WIN_C = 8, 16
SWA_WINDOW = 128
ROPE_THETA = 10000.0
EPS = 1e-6
D_FF = 2816
D_IN = 2816

TB = 256
T_ALL = SEQ + CTX
NBLK = T_ALL // TB
CTX_BLK = SEQ // TB
TM_ALL = 768
TM_LAT = 1024
NA_KROWS = 12
NA_KEYS = NA_KROWS * GRID_W
NA_DR_PAD = 4
NA_DR_ROWS = 24
SWA_KEYS = TB + 2 * SWA_WINDOW
CHUNK = 64
NCHUNK = TB // CHUNK
HGRN_FAST_RANGE = 60.0
Q_SCALE = HEAD_DIM ** -0.5
VMEM_LIMIT = 56 * 1024 * 1024

ATT_W, F32_W = 896, 1536


def _dot(a, b):
    return jnp.dot(a, b, preferred_element_type=F32)


def _dot_nt(a, b):
    return lax.dot_general(a, b, (((1,), (1,)), ((), ())), preferred_element_type=F32)


def _dot_tn(a, b):
    return lax.dot_general(a, b, (((0,), (0,)), ((), ())), preferred_element_type=F32)


def _cparams(sem):
    return pltpu.CompilerParams(dimension_semantics=sem, vmem_limit_bytes=VMEM_LIMIT)


def _const_spec(shape):
    nd = len(shape)
    return pl.BlockSpec(shape, lambda *_: (0,) * nd, pipeline_mode=pl.Buffered(1))


def _layer_spec(shape, l):
    nd = len(shape)
    return pl.BlockSpec((None,) + shape, lambda *_: (l,) + (0,) * nd, pipeline_mode=pl.Buffered(1))


@functools.lru_cache(maxsize=None)
def _tables():
    t = {}
    lane = np.arange(128)
    d = lane % 64
    freq = ROPE_THETA ** (-(d % 16) / 16.0)
    pos = np.arange(SEQ)
    p = np.where(d[None, :] < 32, (pos // GRID_W)[:, None], (pos % GRID_W)[:, None]).astype(np.float64)
    ang = p * freq[None, :]
    sign = np.where((d % 32) < 16, -1.0, 1.0)
    cos = np.concatenate([np.cos(ang), np.ones((CTX, 128))], axis=0)
    sin = np.concatenate([np.sin(ang) * sign[None, :], np.zeros((CTX, 128))], axis=0)
    t["rope_cos"], t["rope_sin"] = cos.astype(np.float32), sin.astype(np.float32)

    masks = []
    a, cq = np.arange(4), np.arange(GRID_W)
    rho, ck = np.arange(NA_KROWS), np.arange(GRID_W)
    for r0 in (0, 8, GRID_ROWS - 4):
        kr0 = int(np.clip(r0 - 4, 0, GRID_ROWS - NA_KROWS))
        rq, rk = r0 + a, kr0 + rho
        start = np.clip(rq - NA_WIN_R // 2, 0, GRID_ROWS - NA_WIN_R)
        vr = (rk[None, :] >= start[:, None]) & (rk[None, :] < start[:, None] + NA_WIN_R)
        cs = np.clip(cq - NA_WIN_C // 2, 0, GRID_W - NA_WIN_C)
        vc = (ck[None, :] >= cs[:, None]) & (ck[None, :] < cs[:, None] + NA_WIN_C)
        val = vr[:, None, :, None] & vc[None, :, None, :]
        masks.append(np.where(val, 0.0, -np.inf).reshape(TB, NA_KEYS))
    masks.append(np.full((TB, NA_KEYS), -np.inf))
    t["na_mask"] = np.stack(masks).astype(np.float32)
    n = np.arange(128)
    clipc = lambda m: np.clip(m, -(NA_WIN_C - 1), NA_WIN_C - 1) + NA_WIN_C - 1
    t["na_lane_l"] = clipc(np.where(n < 64, n, n - 128)).astype(np.int32)
    t["na_lane_r"] = clipc(n - 64).astype(np.int32)

    rel = np.arange(SWA_KEYS)[None, :] - np.arange(TB)[:, None]
    swa = [np.where(np.abs(rel + off) <= SWA_WINDOW, 0.0, -np.inf) for off in (0, -SWA_WINDOW, -2 * SWA_WINDOW)]
    swa.append(np.full((TB, SWA_KEYS), -np.inf))
    t["swa_mask"] = np.stack(swa).astype(np.float32)

    def cs_tab(num, den):
        ang = 2.0 * np.pi * (num % den) / den
        return np.cos(ang), np.sin(ang)

    k1, t1 = np.arange(64)[:, None], np.arange(64)[None, :]
    c64, s64 = cs_tab(k1 * t1, 64)
    t["f64"] = np.concatenate([c64, -s64], axis=0).astype(np.float32)
    k2, t2 = np.arange(128)[:, None], np.arange(128)[None, :]
    g = []
    for kk in range(64):
        gc, gs = cs_tab((64 * k2 + kk) * t2, SEQ)
        g.append(np.block([[gc, gs], [-gs, gc]]))
    t["g_stage2"] = np.stack(g).astype(np.float32)
    kc, tc = np.arange(CTX)[:, None], np.arange(CTX)[None, :]
    cc, sc = cs_tab(kc * tc, CTX)
    t["f_ctx"] = np.concatenate([cc, -sc], axis=0).astype(np.float32)
    ch = np.arange(256)
    same = (ch[:, None] // 64) == (ch[None, :] // 64)
    c_ch, s_ch = cs_tab((ch[:, None] % 64) * (ch[None, :] % 64), 64)
    t["chan_cs"] = np.concatenate([np.where(same, c_ch, 0.0), np.where(same, s_ch, 0.0)], axis=0).astype(np.float32)
    t["head_ones"] = same.astype(np.float32)

    balls, lmasks, bblk = [], [], []
    for rev in (False, True):
        tau = (CHUNK - 1 - np.arange(CHUNK)) if rev else np.arange(CHUNK)
        tt, tj = tau[:, None], tau[None, :]
        blocks = [(tj <= tt), (tj > tt)]
        masks = [tt == tj]
        for h in (32, 16, 8, 4, 2, 1):
            up = ((tt // h) % 2) == 1
            mid = (tt // (2 * h)) * (2 * h) + h - 1
            blocks.append(np.where(up, (tj > mid) & (tj <= tt), (tj > tt) & (tj <= mid)))
            up_s = ((tj // h) % 2) == 1
            masks.append((tt // (2 * h) == tj // (2 * h)) & up & ~up_s)
        balls.append(np.concatenate(blocks, axis=0))
        lmasks.append(np.stack([np.tile(m, (4, 1)) for m in masks]))
        bblk.append(np.kron(np.eye(NCHUNK), blocks[0]))
    t["hgrn_b"] = np.stack(balls).astype(np.float32)
    t["hgrn_mask"] = np.stack(lmasks).astype(np.float32)
    t["hgrn_bblk"] = np.stack(bblk).astype(np.float32)
    return t


def _ada_kernel(cb_ref, w_ref, b_ref, o_ref, s_ref):
    tn = w_ref.shape[1]
    nrep = tn // 128

    @pl.when((pl.program_id(0) == 0) & (pl.program_id(1) == 0))
    def _():
        cb = cb_ref[...]
        s_ref[...] = cb * jax.nn.sigmoid(cb)

    def body(kc, accs):
        k0 = pl.multiple_of(kc * 8, 8)
        wt = w_ref[pl.ds(k0, 8), :]
        return tuple(accs[r] + wt * jnp.concatenate([s_ref[r, pl.ds(k0, 8), :]] * nrep, axis=1) for r in range(3))

    accs = lax.fori_loop(0, D_MODEL // 8, body, tuple(jnp.zeros((8, tn), F32) for _ in range(3)), unroll=8)
    for r in range(3):
        o_ref[r] = jnp.sum(accs[r], axis=0, keepdims=True) + b_ref[...]


def _ada_call(cb, w_ada, b_ada):
    tn = 1536
    return pl.pallas_call(
        _ada_kernel,
        grid=(DEPTH, 6 * D_MODEL // tn),
        in_specs=[
            pl.BlockSpec((3, D_MODEL, 128), lambda l, j: (0, 0, 0)),
            pl.BlockSpec((None, D_MODEL, tn), lambda l, j: (l, 0, j)),
            pl.BlockSpec((None, 1, tn), lambda l, j: (l, 0, j)),
        ],
        out_specs=pl.BlockSpec((None, 3, 1, tn), lambda l, j: (l, 0, 0, j)),
        out_shape=jax.ShapeDtypeStruct((DEPTH, 3, 1, 6 * D_MODEL), F32),
        scratch_shapes=[pltpu.VMEM((3, D_MODEL, 128), F32)],
        compiler_params=_cparams(("arbitrary", "arbitrary")),
        name="adaln",
    )(cb, w_ada, b_ada.reshape(DEPTH, 1, 6 * D_MODEL))


def _mod_specs(col):
    return [pl.BlockSpec((None, 1, D_MODEL), lambda b, i: (2, 0, col)),
            pl.BlockSpec((None, 1, D_MODEL), lambda b, i: (b, 0, col))]


def _row_parts(tm, with_ctx):
    return [(0, tm - CTX), (tm - CTX, tm)] if with_ctx else [(0, tm // 2), (tm // 2, tm)]


def _mod(mc_ref, mb_ref, part, with_ctx):
    if not with_ctx or part == 0:
        return mb_ref[...]
    last = pl.program_id(1) == pl.num_programs(1) - 1
    return jnp.where(last, mc_ref[...], mb_ref[...])


def _modulated(x, g_ref, sh_refs, sc_refs, part, with_ctx):
    gs = g_ref[...] * (1.0 + _mod(*sc_refs, part, with_ctx))
    return (_rms(x) * gs + _mod(*sh_refs, part, with_ctx)).astype(BF16)


def _gated_residual(x, y, gp_ref, gt_refs, part, with_ctx):
    return x + (_mod(*gt_refs, part, with_ctx) * gp_ref[...]) * _rms(y)


def _rms(x):
    return x * lax.rsqrt(jnp.mean(x * x, axis=-1, keepdims=True) + EPS)


def _x_specs(split):
    if not split:
        return [pl.BlockSpec((None, TM_ALL, D_MODEL), lambda b, i: (b, i, 0))]
    nx = SEQ // TB
    return [pl.BlockSpec((None, TB, D_MODEL), lambda b, i, j=j: (b, jnp.minimum(3 * i + j, nx - 1), 0))
            for j in range(TM_ALL // TB)] + [pl.BlockSpec((None, CTX, D_MODEL), lambda b, i: (b, 0, 0))]


def _x_part(x_refs, part, lo, hi):
    if len(x_refs) == 1:
        return x_refs[0][lo:hi, :]
    xa, xb, xc, cx = x_refs
    if part == 0:
        return jnp.concatenate([xa[...], xb[...]], axis=0)
    last = pl.program_id(1) == pl.num_programs(1) - 1
    return jnp.where(last, cx[...], xc[...])


def _first_step():
    return (pl.program_id(0) == 0) & (pl.program_id(1) == 0)


def _proj_kernel(*refs, nx):
    (g_ref, shc_ref, shb_ref, scc_ref, scb_ref, w32_ref, cos_ref, sin_ref,
     att_ref, f32_ref, knt_ref, kst_ref, w_ref) = refs[nx:]

    @pl.when(_first_step())
    def _():
        w_ref[...] = w32_ref[...].astype(BF16)

    for part, (lo, hi) in enumerate(_row_parts(TM_ALL, True)):
        rows = slice(lo, hi)
        h = _modulated(_x_part(refs[:nx], part, lo, hi), g_ref, (shc_ref, shb_ref), (scc_ref, scb_ref), part, True)
        cos, sin = cos_ref[rows, :], sin_ref[rows, :]
        first = (lax.broadcasted_iota(jnp.int32, (hi - lo, 128), 1) % 32) < 16

        def rope(v):
            return v * cos + jnp.where(first, pltpu.roll(v, 112, 1), pltpu.roll(v, 16, 1)) * sin

        a = _dot(h, w_ref[:, 0:768])
        att_ref[rows, 0:256] = (a[:, 0:256] * Q_SCALE).astype(BF16)
        knt_ref[:, rows] = a[:, 256:512].T.astype(BF16)
        att_ref[rows, 256:512] = a[:, 512:768].astype(BF16)
        f32_ref[rows, 0:256] = _dot(h, w_ref[:, 768:1024])
        s = _dot(h, w_ref[:, 1024:1536])
        att_ref[rows, 512:640] = (rope(s[:, 0:128]) * Q_SCALE).astype(BF16)
        att_ref[rows, 640:768] = (rope(s[:, 128:256]) * Q_SCALE).astype(BF16)
        kst_ref[:, rows] = rope(s[:, 256:384]).T.astype(BF16)
        att_ref[rows, 768:896] = s[:, 384:512].astype(BF16)
        f32_ref[rows, 256:1536] = _dot(h, w_ref[:, 1536:2816])


def _proj_call(x_parts, g, mods, w_in, l, cos, sin):
    b = x_parts[0].shape[0]
    tm = TM_ALL
    split = len(x_parts) == 2
    x_args = (x_parts[0],) * 3 + (x_parts[1],) if split else x_parts
    return pl.pallas_call(
        functools.partial(_proj_kernel, nx=len(x_args)),
        grid=(b, T_ALL // tm),
        in_specs=[
            *_x_specs(split),
            _const_spec((1, D_MODEL)),
            *_mod_specs(0), *_mod_specs(1),
            _layer_spec((D_MODEL, D_IN), l),
            pl.BlockSpec((tm, 128), lambda b, i: (i, 0)),
            pl.BlockSpec((tm, 128), lambda b, i: (i, 0)),
        ],
        out_specs=[
            pl.BlockSpec((None, tm, ATT_W), lambda b, i: (b, i, 0)),
            pl.BlockSpec((None, tm, F32_W), lambda b, i: (b, i, 0)),
            pl.BlockSpec((None, 256, tm), lambda b, i: (b, 0, i)),
            pl.BlockSpec((None, 128, tm), lambda b, i: (b, 0, i)),
        ],
        out_shape=[jax.ShapeDtypeStruct((b, T_ALL, ATT_W), BF16), jax.ShapeDtypeStruct((b, T_ALL, F32_W), F32),
                   jax.ShapeDtypeStruct((b, 256, T_ALL), BF16), jax.ShapeDtypeStruct((b, 128, T_ALL), BF16)],
        scratch_shapes=[pltpu.VMEM((D_MODEL, D_IN), BF16)],
        compiler_params=_cparams(("arbitrary", "arbitrary")),
        name="proj_in",
    )(*x_args, g, mods, mods, mods, mods, w_in, cos, sin)


def _na_bias_kernel(vl_ref, vr_ref, o_ref):
    lane = lax.broadcasted_iota(jnp.int32, (GRID_W, 128), 1)
    for e in range(NA_DR_ROWS - 1):
        left = pltpu.roll(jnp.broadcast_to(vl_ref[e:e + 1, :], (GRID_W, 128)), 0, 1, stride=1, stride_axis=0)
        right = pltpu.roll(jnp.broadcast_to(vr_ref[e + 1:e + 2, :], (GRID_W, 128)), 0, 1, stride=1, stride_axis=0)
        o_ref[e] = jnp.where(lane < 64, left, right)


def _na_bias_call(vl, vr):
    spec = pl.BlockSpec((None, NA_DR_ROWS, 128), lambda h: (h, 0, 0))
    return pl.pallas_call(
        _na_bias_kernel,
        grid=(4,),
        in_specs=[spec, spec],
        out_specs=pl.BlockSpec((None, NA_DR_ROWS - 1, GRID_W, 128), lambda h: (h, 0, 0, 0)),
        out_shape=jax.ShapeDtypeStruct((4, NA_DR_ROWS - 1, GRID_W, 128), F32),
        compiler_params=_cparams(("arbitrary",)),
        name="na_bias",
    )(vl, vr)


AT_ROWS = 16


def _attn_block_variant(b, i):
    return (jnp.where(i == CTX_BLK, 3, jnp.where(i == 0, 0, jnp.where(i == CTX_BLK - 1, 2, 1))), 0, 0)


def _lane_half(shape, j):
    lane = lax.broadcasted_iota(jnp.int32, shape, 1)
    return (lane >= 64 * j) & (lane < 64 * (j + 1))


def _na_window(i):
    is_ctx = i == CTX_BLK
    r0 = i * 4
    kr0 = jnp.where(is_ctx, 0, jnp.clip(r0 - 4, 0, GRID_ROWS - NA_KROWS))
    delta = jnp.where(is_ctx, 3, kr0 - r0 + NA_WIN_R - 1) + NA_DR_PAD
    return pl.multiple_of(kr0 * GRID_W, 128), delta


def _na_scores(i, q_ref, k_ref, s_ref):
    kstart, _ = _na_window(i)
    for hp in range(2):
        cs = slice(hp * 128, (hp + 1) * 128)
        q2 = q_ref[:, cs]
        kw, kc = k_ref[cs, pl.ds(kstart, NA_KEYS)], k_ref[cs, SEQ:T_ALL]
        qm = jnp.concatenate([jnp.where(_lane_half((TB, 128), j), q2, jnp.zeros_like(q2)) for j in range(2)], axis=0)
        s_ref[hp, :, 0:CTX] = _dot(qm, kc)
        s_ref[hp, :, CTX:] = _dot(qm, kw)


def _na_softmax(i, s_ref, mask_ref, t2_ref, p_ref):
    _, delta = _na_window(i)
    for hp in range(2):
        for r in range(2 * TB // AT_ROWS):
            rows = slice(r * AT_ROWS, (r + 1) * AT_ROWS)
            j, rq = divmod(r, TB // AT_ROWS)
            a, sub = divmod(rq, GRID_W // AT_ROWS)
            trow = slice(sub * AT_ROWS, (sub + 1) * AT_ROWS)
            bias = jnp.concatenate(
                [t2_ref[2 * hp + j, delta + 2 * rp - a, trow, :] for rp in range(NA_KROWS // 2)], axis=1)
            s_nb = s_ref[hp, rows, CTX:] + bias + mask_ref[rq * AT_ROWS:(rq + 1) * AT_ROWS, :]
            s_c = s_ref[hp, rows, 0:CTX]
            m = jnp.maximum(jnp.max(s_nb, axis=-1, keepdims=True), jnp.max(s_c, axis=-1, keepdims=True))
            p_ref[hp, rows, 0:CTX] = jnp.exp((s_c - m).astype(BF16))
            p_ref[hp, rows, CTX:] = jnp.exp((s_nb - m).astype(BF16))


def _na_values(i, p_ref, v_ref, o_ref):
    kstart, _ = _na_window(i)
    lane = lax.broadcasted_iota(jnp.int32, (TB, 128), 1)
    for hp in range(2):
        cs = slice(hp * 128, (hp + 1) * 128)
        vw, vc = v_ref[pl.ds(kstart, NA_KEYS), cs], v_ref[SEQ:T_ALL, cs]
        o2 = (_dot(p_ref[hp, :, 0:CTX], jnp.concatenate([vc, jnp.ones_like(vc)], axis=1))
              + _dot(p_ref[hp, :, CTX:], jnp.concatenate([vw, jnp.ones_like(vw)], axis=1)))
        top = o2[0:TB, 0:128] * (1.0 / o2[0:TB, 128:256])
        bot = o2[TB:2 * TB, 0:128] * (1.0 / o2[TB:2 * TB, 128:256])
        o_ref[:, cs] = jnp.where(lane < 64, top, bot).astype(BF16)


def _swa_window(i):
    return pl.multiple_of(jnp.clip(i * TB - SWA_WINDOW, 0, SEQ - SWA_KEYS), SWA_WINDOW)


def _swa_scores(i, q_ref, k_ref, s_ref):
    kw, kc = k_ref[:, pl.ds(_swa_window(i), SWA_KEYS)], k_ref[:, SEQ:T_ALL]
    for g in range(2):
        q2 = q_ref[:, g * 128:(g + 1) * 128].astype(F32)
        qa = jnp.concatenate([jnp.where(_lane_half((TB, 128), g), q2 if j == g else pltpu.roll(q2, 64, 1), 0.0)
                              for j in range(2)], axis=0).astype(BF16)
        s_ref[g, :, 0:CTX] = _dot(qa, kc)
        s_ref[g, :, CTX:] = _dot(qa, kw)


def _swa_softmax(sink_ref, s_ref, mask_ref, p_ref, esink_ref):
    for g in range(2):
        for r in range(2 * TB // AT_ROWS):
            rows = slice(r * AT_ROWS, (r + 1) * AT_ROWS)
            j, rq = divmod(r, TB // AT_ROWS)
            sink = sink_ref[2 * g + j]
            s_l = s_ref[g, rows, CTX:] + mask_ref[rq * AT_ROWS:(rq + 1) * AT_ROWS, :]
            s_c = s_ref[g, rows, 0:CTX]
            m = jnp.maximum(jnp.maximum(jnp.max(s_l, axis=-1, keepdims=True),
                                        jnp.max(s_c, axis=-1, keepdims=True)), sink)
            p_ref[g, rows, 0:CTX] = jnp.exp((s_c - m).astype(BF16))
            p_ref[g, rows, CTX:] = jnp.exp((s_l - m).astype(BF16))
            esink_ref[g, rows, :] = jnp.exp(jnp.broadcast_to(sink - m, (AT_ROWS, 128)))


def _swa_values(i, p_ref, esink_ref, v_ref, o_ref):
    vw, vc = v_ref[pl.ds(_swa_window(i), SWA_KEYS), :], v_ref[SEQ:T_ALL, :]
    lane = lax.broadcasted_iota(jnp.int32, (TB, 128), 1)
    vc2 = jnp.concatenate([vc, jnp.ones_like(vc)], axis=1)
    vw2 = jnp.concatenate([vw, jnp.ones_like(vw)], axis=1)
    for g in range(2):
        o2 = _dot(p_ref[g, :, 0:CTX], vc2) + _dot(p_ref[g, :, CTX:], vw2)
        o = o2[:, 0:128] * (1.0 / (o2[:, 128:256] + esink_ref[g]))
        outs = [o[j * TB:(j + 1) * TB] if j == g else pltpu.roll(o[j * TB:(j + 1) * TB], 64, 1) for j in range(2)]
        o_ref[:, g * 128:(g + 1) * 128] = jnp.where(lane < 64, outs[0], outs[1]).astype(BF16)


def _attn_kernel(sink_ref, qn_ref, kn_ref, vn_ref, maskn_ref, t2_ref, qs_ref, ks_ref, vs_ref, masks_ref,
                 on_ref, os_ref, sn_ref, pn_ref, ss_ref, ps_ref, esink_ref):
    i = pl.program_id(1)
    _na_scores(i, qn_ref, kn_ref, sn_ref)
    _swa_scores(i, qs_ref, ks_ref, ss_ref)
    _na_softmax(i, sn_ref, maskn_ref, t2_ref, pn_ref)
    _swa_softmax(sink_ref, ss_ref, masks_ref, ps_ref, esink_ref)
    _na_values(i, pn_ref, vn_ref, on_ref)
    _swa_values(i, ps_ref, esink_ref, vs_ref, os_ref)


def _attn_call(u_att, kn_t, ks_t, sink, na_mask, t2, swa_mask, with_ctx):
    b = u_att.shape[0]
    nb = NBLK if with_ctx else CTX_BLK
    blk = lambda width, col: pl.BlockSpec((None, TB, width), lambda b, i: (b, i, col))
    full = lambda width, col: pl.BlockSpec((None, T_ALL, width), lambda b, i: (b, 0, col))
    keys = lambda width: pl.BlockSpec((None, width, T_ALL), lambda b, i: (b, 0, 0))
    out = pl.BlockSpec((None, TB, 256), lambda b, i: (b, i, 0))
    return pl.pallas_call(
        _attn_kernel,
        grid=(b, nb),
        in_specs=[
            pl.BlockSpec(memory_space=pltpu.SMEM),
            blk(256, 0), keys(256), full(256, 1),
            pl.BlockSpec((None, TB, NA_KEYS), _attn_block_variant),
            _const_spec((4, NA_DR_ROWS - 1, GRID_W, 128)),
            blk(256, 2), keys(128), full(128, 6),
            pl.BlockSpec((None, TB, SWA_KEYS), _attn_block_variant),
        ],
        out_specs=[out, out],
        out_shape=[jax.ShapeDtypeStruct((b, nb * TB, 256), BF16)] * 2,
        scratch_shapes=[pltpu.VMEM((2, 2 * TB, CTX + NA_KEYS), F32), pltpu.VMEM((2, 2 * TB, CTX + NA_KEYS), BF16),
                        pltpu.VMEM((2, 2 * TB, CTX + SWA_KEYS), F32), pltpu.VMEM((2, 2 * TB, CTX + SWA_KEYS), BF16),
                        pltpu.VMEM((2, 2 * TB, 128), F32)],
        compiler_params=_cparams(("arbitrary", "arbitrary")),
        name="attn",
    )(sink, u_att, kn_t, u_att, na_mask, t2, u_att, ks_t, u_att, swa_mask)


FN_XP = 136
FN_YP = 136
FN_ZP = 72


def _fnet_kernel(x_ref, f64_ref, g_ref, fc_ref, p_ref, q_ref, xz_ref, zq_ref, y_ref):
    sc_lat, sc_ctx = (64.0 * SEQ) ** -0.5, (64.0 * CTX) ** -0.5
    pq = _dot(fc_ref[...], x_ref[SEQ:T_ALL, :].astype(BF16))
    p_ref[SEQ:T_ALL, :] = (pq[0:CTX] * sc_ctx).astype(BF16)
    q_ref[SEQ:T_ALL, :] = (pq[CTX:2 * CTX] * sc_ctx).astype(BF16)
    f64 = f64_ref[...]

    def repitch(t1, carry):
        xz_ref[pl.ds(pl.multiple_of(t1 * FN_XP, 8), 128), :] = x_ref[pl.ds(pl.multiple_of(t1 * 128, 128), 128), :]
        return carry

    lax.fori_loop(0, 64, repitch, 0, unroll=4)

    def stage1(t2, carry):
        xs = xz_ref[pl.ds(t2, 64, stride=FN_XP), :].astype(BF16)
        y_ref[pl.ds(t2, 128, stride=FN_YP), :] = _dot(f64, xs)
        return carry

    lax.fori_loop(0, 128, stage1, 0, unroll=16)

    def stage2(k1, carry):
        yre = y_ref[pl.ds(pl.multiple_of(k1 * FN_YP, 8), 128), :]
        yim = y_ref[pl.ds(pl.multiple_of((64 + k1) * FN_YP, 8), 128), :]
        ycat = jnp.concatenate([yre, yim], axis=0).astype(BF16)
        pq = _dot(g_ref[k1], ycat)
        xz_ref[pl.ds(k1, 128, stride=FN_ZP), :] = pq[0:128] * sc_lat
        zq_ref[pl.ds(k1, 128, stride=FN_ZP), :] = pq[128:256] * sc_lat
        return carry

    lax.fori_loop(0, 64, stage2, 0, unroll=16)

    def emit(k2, carry):
        src = pl.ds(pl.multiple_of(k2 * FN_ZP, 8), 64)
        dst = pl.ds(pl.multiple_of(k2 * 64, 64), 64)
        p_ref[dst, :] = xz_ref[src, :].astype(BF16)
        q_ref[dst, :] = zq_ref[src, :].astype(BF16)
        return carry

    lax.fori_loop(0, 128, emit, 0, unroll=4)


def _fnet_call(u_f32, f64, g2, fc):
    b = u_f32.shape[0]
    spec = pl.BlockSpec((None, T_ALL, 128), lambda b, j: (b, 0, j))
    return pl.pallas_call(
        _fnet_kernel,
        grid=(b, 2),
        in_specs=[spec, _const_spec((128, 64)), _const_spec((64, 256, 256)), _const_spec((2 * CTX, CTX))],
        out_specs=[spec, spec],
        out_shape=[jax.ShapeDtypeStruct((b, T_ALL, 256), BF16)] * 2,
        scratch_shapes=[pltpu.VMEM((128 * FN_ZP, 128), F32), pltpu.VMEM((128 * FN_ZP, 128), F32),
                        pltpu.VMEM((128 * FN_YP, 128), F32)],
        compiler_params=_cparams(("arbitrary", "arbitrary")),
        name="fnet_dft",
    )(u_f32, f64, g2, fc)


def _hgrn_gates(qd, fz, gl, lb_zero):
    q = qd * jax.nn.sigmoid(qd)
    sp = jnp.log(1.0 + jnp.exp(-jnp.abs(fz)))
    ls = jnp.minimum(fz, 0.0) - sp
    lk = jnp.minimum(-fz, 0.0) - sp
    if lb_zero:
        return q, jnp.exp(lk), ls
    b_ = gl[1:2] + ls
    a_ = gl[0:1]
    logf = jnp.maximum(a_, b_) + jnp.log(1.0 + jnp.exp(-jnp.abs(a_ - b_)))
    return q, jnp.exp(gl[1:2] + lk), logf


def _dot01(b01, x):
    h1 = x.astype(BF16)
    r1 = x - h1.astype(F32)
    h2 = r1.astype(BF16)
    h3 = (r1 - h2.astype(F32)).astype(BF16)
    return _dot(b01, h1) + _dot(b01, h2) + _dot(b01, h3)


def _head_masks(rows):
    lane = lax.broadcasted_iota(jnp.int32, (rows, 256), 1)
    return [(lane >= 64 * h) & (lane < 64 * (h + 1)) for h in range(4)]


def _stack_heads(w, hms):
    return jnp.concatenate([jnp.where(hm, w, jnp.zeros_like(w)) for hm in hms], axis=0)


def _state_update(st_ref, st, e_last, vb, ks):
    ktv = _dot_tn(vb, ks)
    r2 = lax.broadcasted_iota(jnp.int32, (256, 256), 0) // 64
    c2 = lax.broadcasted_iota(jnp.int32, (256, 256), 1) // 64
    st_ref[...] = st * e_last + jnp.where(r2 == c2, ktv, 0.0)


def _hgrn_chunk_exact(qd, fz, v, st_ref, gl, ball, lmask_ref, d, rev, lb_zero):
    q, kk, logf = _hgrn_gates(qd, fz, gl, lb_zero)
    e = jnp.exp(_dot01(ball, logf))
    row = lax.broadcasted_iota(jnp.int32, (CHUNK, 256), 0)
    tau = (CHUNK - 1 - row) if rev else row
    hms = _head_masks(CHUNK)
    att = _dot_nt(_stack_heads(q.astype(BF16), hms), kk.astype(BF16)) * lmask_ref[d, 0]
    for li, sh in enumerate((5, 4, 3, 2, 1, 0)):
        up = ((tau >> sh) & 1) == 1
        w = (jnp.where(up, q, kk) * e[128 + 64 * li:192 + 64 * li]).astype(BF16)
        att = att + _dot_nt(_stack_heads(w, hms), w) * lmask_ref[d, 1 + li]
    vb = v.astype(BF16)
    st = st_ref[...]
    o = _dot_nt((q * e[0:64]).astype(BF16), st.astype(BF16))
    for h in range(4):
        o = o + _dot(att[64 * h:64 * (h + 1)].astype(BF16), jnp.where(hms[h], vb, jnp.zeros_like(vb)))
    tl = 0 if rev else CHUNK - 1
    _state_update(st_ref, st, e[tl:tl + 1], vb, (kk * e[64:128]).astype(BF16))
    return o


def _hgrn_chunk_prep(sl, q_ref, f_ref, gl, tri, rev, lb_zero):
    q, kk, logf = _hgrn_gates(q_ref[sl, :], f_ref[sl, :], gl, lb_zero)
    cum = _dot01(tri, logf)
    first_row, mid_row, last_row = (CHUNK - 1, CHUNK // 2, 0) if rev else (0, CHUNK // 2 - 1, CHUNK - 1)
    cum_mid, cum_last = cum[mid_row:mid_row + 1], cum[last_row:last_row + 1]
    a = cum - cum_mid
    a_range = jnp.maximum(jnp.max(a[first_row:first_row + 1]), jnp.max(-a[last_row:last_row + 1]))
    return q * jnp.exp(a), kk * jnp.exp(-a), cum_mid, cum_last, a_range


def _hgrn_chunk_fast(sl, qt, kt, cum_mid, cum_last, v_ref, st_ref, o_ref, rev):
    hms = _head_masks(CHUNK)
    r = lax.broadcasted_iota(jnp.int32, (4 * CHUNK, CHUNK), 0) & (CHUNK - 1)
    s = lax.broadcasted_iota(jnp.int32, (4 * CHUNK, CHUNK), 1)
    causal = (s >= r) if rev else (s <= r)
    vb = v_ref[sl, :].astype(BF16)
    att = jnp.where(causal, _dot_nt(_stack_heads(qt.astype(BF16), hms), kt.astype(BF16)), 0.0)
    st = st_ref[...]
    o = _dot_nt((qt * jnp.exp(cum_mid)).astype(BF16), st.astype(BF16))
    for h in range(4):
        o = o + _dot(att[64 * h:64 * (h + 1)].astype(BF16), jnp.where(hms[h], vb, jnp.zeros_like(vb)))
    o_ref[sl, :] = o
    _state_update(st_ref, st, jnp.exp(cum_last), vb, (kt * jnp.exp(cum_last - cum_mid)).astype(BF16))


def _hgrn_kernel(qf_ref, ff_ref, vf_ref, qb_ref, fb_ref, vb_ref, gl_ref, ball_ref, lmask_ref, bblk_ref,
                 of_ref, ob_ref, sf_ref, sb_ref, sf0_ref, sb0_ref, *, lb_zero):
    @pl.when(pl.program_id(1) == 0)
    def _():
        sf_ref[...] = jnp.zeros_like(sf_ref)
        sb_ref[...] = jnp.zeros_like(sb_ref)

    sf0_ref[...] = sf_ref[...]
    sb0_ref[...] = sb_ref[...]
    dirs = ((qf_ref, ff_ref, vf_ref, sf_ref, of_ref, False), (qb_ref, fb_ref, vb_ref, sb_ref, ob_ref, True))
    rows = lambda rev, s: slice((NCHUNK - 1 - s if rev else s) * CHUNK, (NCHUNK - s if rev else s + 1) * CHUNK)

    def prep(s):
        return [_hgrn_chunk_prep(rows(rev, s), q_ref, f_ref, gl_ref[d], ball_ref[d, 0:CHUNK, :], rev, lb_zero)
                for d, (q_ref, f_ref, v_ref, st_ref, o_ref, rev) in enumerate(dirs)]

    rng, ready = 0.0, prep(0)
    for s in range(NCHUNK):
        nxt = prep(s + 1) if s + 1 < NCHUNK else None
        for d, (q_ref, f_ref, v_ref, st_ref, o_ref, rev) in enumerate(dirs):
            qt, kt, cum_mid, cum_last, amax = ready[d]
            rng = jnp.maximum(rng, amax)
            _hgrn_chunk_fast(rows(rev, s), qt, kt, cum_mid, cum_last, v_ref, st_ref, o_ref, rev)
        ready = nxt

    @pl.when(jnp.logical_not(rng <= HGRN_FAST_RANGE))
    def _():
        sf_ref[...] = sf0_ref[...]
        sb_ref[...] = sb0_ref[...]

        def body(c, carry):
            rf = pl.ds(pl.multiple_of(c * CHUNK, CHUNK), CHUNK)
            rb = pl.ds(pl.multiple_of((NCHUNK - 1 - c) * CHUNK, CHUNK), CHUNK)
            of_ref[rf, :] = _hgrn_chunk_exact(qf_ref[rf, :], ff_ref[rf, :], vf_ref[rf, :], sf_ref,
                                              gl_ref[0], ball_ref[0], lmask_ref, 0, False, lb_zero)
            ob_ref[rb, :] = _hgrn_chunk_exact(qb_ref[rb, :], fb_ref[rb, :], vb_ref[rb, :], sb_ref,
                                              gl_ref[1], ball_ref[1], lmask_ref, 1, True, lb_zero)
            return carry

        lax.fori_loop(0, NCHUNK, body, 0)


def _hgrn_call(u_f32, gl, ball, lmask, bblk, lb_zero):
    b = u_f32.shape[0]

    def fwd(col):
        return pl.BlockSpec((None, TB, 256), lambda b, i: (b, jnp.where(i == 0, CTX_BLK, i - 1), col))

    def bwd(col):
        return pl.BlockSpec((None, TB, 256), lambda b, i: (b, jnp.where(i == 0, CTX_BLK, CTX_BLK - i), col))

    return pl.pallas_call(
        functools.partial(_hgrn_kernel, lb_zero=lb_zero),
        grid=(b, NBLK),
        in_specs=[fwd(1), fwd(2), fwd(4), bwd(1), bwd(3), bwd(4),
                  _const_spec((2, 2, 256)), _const_spec((2, 8 * CHUNK, CHUNK)),
                  _const_spec((2, 7, 4 * CHUNK, CHUNK)), _const_spec((2, TB, TB))],
        out_specs=[fwd(0), bwd(0)],
        out_shape=[jax.ShapeDtypeStruct((b, T_ALL, 256), F32)] * 2,
        scratch_shapes=[pltpu.VMEM((256, 256), F32)] * 4,
        compiler_params=_cparams(("arbitrary", "arbitrary")),
        name="hgrn_scan",
    )(u_f32, u_f32, u_f32, u_f32, u_f32, u_f32, gl, ball, lmask, bblk)


def _outproj_kernel(*refs, nx, with_ctx):
    (oa_ref, p_ref, q_ref, oc_ref, of_ref, ob_ref, gd_ref, hg_ref, hh_ref, cs_ref,
     wf32_ref, wo32_ref, gp_ref, gtc_ref, gtb_ref, o_ref, wf_ref, wo_ref) = refs[nx:]

    @pl.when(_first_step())
    def _():
        wf_ref[...] = wf32_ref[...].astype(BF16)
        wo_ref[...] = wo32_ref[...].astype(BF16)

    for part, (r0, r1) in enumerate(_row_parts(o_ref.shape[0], with_ctx)):
        rows = slice(r0, r1)
        z = _dot(p_ref[rows, :], cs_ref[0:256, :]) + _dot(q_ref[rows, :], cs_ref[256:512, :])
        o_b = _dot(z.astype(BF16), wf_ref[...])
        od = of_ref[rows, :] + ob_ref[rows, :]
        sq = od * od
        hi = sq.astype(BF16)
        lo = (sq - hi.astype(F32)).astype(BF16)
        ms = (_dot(hi, hh_ref[...]) + _dot(lo, hh_ref[...])) * (1.0 / HEAD_DIM)
        gd = gd_ref[rows, :]
        o_d = od * lax.rsqrt(ms + EPS) * hg_ref[...] * (gd * jax.nn.sigmoid(gd))
        y = (_dot(oa_ref[rows, :], wo_ref[0:256, :]) + _dot(o_b.astype(BF16), wo_ref[256:512, :])
             + _dot(oc_ref[rows, :], wo_ref[512:768, :]) + _dot(o_d.astype(BF16), wo_ref[768:1024, :]))
        x = _x_part(refs[:nx], part, r0, r1)
        o_ref[rows, :] = _gated_residual(x, y, gp_ref, (gtc_ref, gtb_ref), part, with_ctx)


def _outproj_call(x_parts, o_a, p, q, o_c, o_f, o_bk, u_f32, hg, hh, cs, w_fnet, w_out, l, g_post, mods, with_ctx):
    b = x_parts[0].shape[0]
    tm, t = (TM_ALL, T_ALL) if with_ctx else (TM_LAT, SEQ)
    col = lambda c: pl.BlockSpec((None, tm, 256), lambda b, i: (b, i, c))
    split = len(x_parts) == 2
    x_args = (x_parts[0],) * 3 + (x_parts[1],) if split else x_parts
    x_specs = _x_specs(True) if split else [pl.BlockSpec((None, tm, D_MODEL), lambda b, i: (b, i, 0))]
    return pl.pallas_call(
        functools.partial(_outproj_kernel, nx=len(x_args), with_ctx=with_ctx),
        grid=(b, t // tm),
        in_specs=[
            *x_specs,
            col(0), col(0), col(0), col(0), col(0), col(0), col(5),
            _const_spec((1, 256)), _const_spec((256, 256)), _const_spec((512, 256)),
            _layer_spec((256, 256), l), _layer_spec((D_MODEL, D_MODEL), l), _const_spec((1, D_MODEL)),
            *_mod_specs(2),
        ],
        out_specs=pl.BlockSpec((None, tm, D_MODEL), lambda b, i: (b, i, 0)),
        out_shape=jax.ShapeDtypeStruct((b, t, D_MODEL), F32),
        scratch_shapes=[pltpu.VMEM((256, 256), BF16), pltpu.VMEM((D_MODEL, D_MODEL), BF16)],
        compiler_params=_cparams(("arbitrary", "arbitrary")),
        name="out_proj",
    )(*x_args, o_a, p, q, o_c, o_f, o_bk, u_f32, hg, hh, cs, w_fnet, w_out, g_post, mods, mods)


def _ffn_kernel(x_ref, g_ref, shc_ref, shb_ref, scc_ref, scb_ref, w1_ref, w2_ref, gp_ref, gtc_ref, gtb_ref,
                o_ref, *, with_ctx):
    fc = 256

    def swiglu(h):
        acc = jnp.zeros((h.shape[0], D_MODEL), F32)
        for c in range(D_FF // fc):
            a = _dot(h, w1_ref[:, c * fc:(c + 1) * fc])
            g = _dot(h, w1_ref[:, D_FF + c * fc:D_FF + (c + 1) * fc])
            act = (a * jax.nn.sigmoid(a) * g).astype(BF16)
            acc = acc + _dot(act, w2_ref[c * fc:(c + 1) * fc, :])
        return acc

    parts = list(enumerate(_row_parts(x_ref.shape[0], with_ctx)))
    pre = lambda part, r0, r1: _modulated(x_ref[r0:r1, :], g_ref, (shc_ref, shb_ref), (scc_ref, scb_ref), part, with_ctx)
    post = lambda part, r0, r1, acc: _gated_residual(x_ref[r0:r1, :], acc, gp_ref, (gtc_ref, gtb_ref), part, with_ctx)
    if with_ctx:
        acc = swiglu(jnp.concatenate([pre(part, r0, r1) for part, (r0, r1) in parts], axis=0))
        for part, (r0, r1) in parts:
            o_ref[r0:r1, :] = post(part, r0, r1, acc[r0:r1])
    else:
        for part, (r0, r1) in parts:
            o_ref[r0:r1, :] = post(part, r0, r1, swiglu(pre(part, r0, r1)))


def _ffn_call(xs, g_pre, w1, w2, l, g_post, mods, with_ctx):
    b, t, _ = xs.shape
    tm = TM_ALL if with_ctx else TM_LAT
    return pl.pallas_call(
        functools.partial(_ffn_kernel, with_ctx=with_ctx),
        grid=(b, t // tm),
        in_specs=[
            pl.BlockSpec((None, tm, D_MODEL), lambda b, i: (b, i, 0)),
            _const_spec((1, D_MODEL)),
            *_mod_specs(3), *_mod_specs(4),
            _layer_spec((D_MODEL, 2 * D_FF), l), _layer_spec((D_FF, D_MODEL), l), _const_spec((1, D_MODEL)),
            *_mod_specs(5),
        ],
        out_specs=pl.BlockSpec((None, tm, D_MODEL), lambda b, i: (b, i, 0)),
        out_shape=jax.ShapeDtypeStruct(xs.shape, F32),
        compiler_params=_cparams(("arbitrary", "arbitrary")),
        name="ffn",
    )(xs, g_pre, mods, mods, mods, mods, w1, w2, g_post, mods, mods)


def kernel(x, c, ctx, c_ctx, w_ada, b_ada, g_pre_mix, g_post_mix, g_pre_ffn, g_post_ffn, w_in, w_out,
           na_rpb, w_fnet, swa_sink, hgrn_lb_logits, hgrn_norm_g, w_ffn_in, w_ffn_out):
    tb = _tables()
    bf = lambda name: jnp.asarray(tb[name]).astype(BF16)
    cos, sin = jnp.asarray(tb["rope_cos"]), jnp.asarray(tb["rope_sin"])
    f64, g2, fc, chan_cs, hh = bf("f64"), bf("g_stage2"), bf("f_ctx"), bf("chan_cs"), bf("head_ones")
    ball, bblk, lmask = bf("hgrn_b"), bf("hgrn_bblk"), jnp.asarray(tb["hgrn_mask"])
    na_mask, swa_mask = jnp.asarray(tb["na_mask"]), jnp.asarray(tb["swa_mask"])

    cc = jnp.concatenate([c, c_ctx[None, :]], axis=0)
    mods = _ada_call(jnp.broadcast_to(cc[:, :, None], (3, D_MODEL, 128)), w_ada, b_ada)

    lb = jnp.cumsum(jax.nn.softmax(hgrn_lb_logits.astype(F32), axis=1), axis=1)
    lb = lb - lb[:, :1]
    gl_all = jnp.stack([jnp.log(lb), jnp.log1p(-lb)], axis=2)

    w_ffn_in_b, w_ffn_out_b = w_ffn_in.astype(BF16), w_ffn_out.astype(BF16)

    xs = None
    for l in range(DEPTH):
        with_ctx = l < DEPTH - 1
        row = lambda a: a[l][None, :]
        x_all = (x, ctx) if l == 0 else (xs,)
        x_res = x_all if with_ctx else x_all[:1]
        u_att, u_f32, kn_t, ks_t = _proj_call(x_all, row(g_pre_mix), mods[l], w_in, l, cos, sin)

        rpb = jnp.pad(na_rpb[l], ((0, 0), (NA_DR_PAD, NA_DR_ROWS - NA_DR_PAD - (2 * NA_WIN_R - 1)), (0, 0)))
        t2 = _na_bias_call(rpb[:, :, tb["na_lane_l"]], rpb[:, :, tb["na_lane_r"]])
        o_a, o_c = _attn_call(u_att, kn_t, ks_t, swa_sink[l], na_mask, t2, swa_mask, with_ctx)
        p, q = _fnet_call(u_f32, f64, g2, fc)
        o_f, o_bk = _hgrn_call(u_f32, gl_all[:, l], ball, lmask, bblk, lb_zero=(l == 0))

        xs = _outproj_call(x_res, o_a, p, q, o_c, o_f, o_bk, u_f32, row(hgrn_norm_g), hh, chan_cs,
                           w_fnet, w_out, l, row(g_post_mix), mods[l], with_ctx)
        xs = _ffn_call(xs, row(g_pre_ffn), w_ffn_in_b, w_ffn_out_b, l, row(g_post_ffn), mods[l], with_ctx)
    return xs
```

```python
import functools

import numpy as np
import jax
import jax.numpy as jnp
from jax import lax
from jax.experimental import pallas as pl
from jax.experimental.pallas import tpu as pltpu

F32, BF16 = jnp.float32, jnp.bfloat16

D_MODEL = 1024
SEQ = 8192
DEPTH = 2
GRID_W = 64
GRID_ROWS = SEQ // GRID_W
CTX = 256
HEAD_DIM = 64
NA_WIN_R, NA_WIN_C = 8, 16
SWA_WINDOW = 128
ROPE_THETA = 10000.0
EPS = 1e-6
D_FF = 2816
D_IN = 2816

TB = 256
T_ALL = SEQ + CTX
NBLK = T_ALL // TB
CTX_BLK = SEQ // TB
TM_ALL = 768
TM_LAT = 1024
NA_KROWS = 12
NA_KEYS = NA_KROWS * GRID_W
NA_DR_PAD = 4
NA_DR_ROWS = 24
SWA_KEYS = TB + 2 * SWA_WINDOW
CHUNK = 64
NCHUNK = TB // CHUNK
HGRN_FAST_RANGE = 60.0
Q_SCALE = HEAD_DIM ** -0.5
VMEM_LIMIT = 56 * 1024 * 1024

ATT_W, F32_W = 896, 1536


def _dot(a, b):
    return jnp.dot(a, b, preferred_element_type=F32)


def _dot_nt(a, b):
    return lax.dot_general(a, b, (((1,), (1,)), ((), ())), preferred_element_type=F32)


def _dot_tn(a, b):
    return lax.dot_general(a, b, (((0,), (0,)), ((), ())), preferred_element_type=F32)


def _cparams(sem):
    return pltpu.CompilerParams(dimension_semantics=sem, vmem_limit_bytes=VMEM_LIMIT)


def _const_spec(shape):
    nd = len(shape)
    return pl.BlockSpec(shape, lambda *_: (0,) * nd, pipeline_mode=pl.Buffered(1))


def _layer_spec(shape, l):
    nd = len(shape)
    return pl.BlockSpec((None,) + shape, lambda *_: (l,) + (0,) * nd, pipeline_mode=pl.Buffered(1))


@functools.lru_cache(maxsize=None)
def _tables():
    t = {}
    lane = np.arange(128)
    d = lane % 64
    freq = ROPE_THETA ** (-(d % 16) / 16.0)
    pos = np.arange(SEQ)
    p = np.where(d[None, :] < 32, (pos // GRID_W)[:, None], (pos % GRID_W)[:, None]).astype(np.float64)
    ang = p * freq[None, :]
    sign = np.where((d % 32) < 16, -1.0, 1.0)
    cos = np.concatenate([np.cos(ang), np.ones((CTX, 128))], axis=0)
    sin = np.concatenate([np.sin(ang) * sign[None, :], np.zeros((CTX, 128))], axis=0)
    t["rope_cos"], t["rope_sin"] = cos.astype(np.float32), sin.astype(np.float32)

    masks = []
    a, cq = np.arange(4), np.arange(GRID_W)
    rho, ck = np.arange(NA_KROWS), np.arange(GRID_W)
    for r0 in (0, 8, GRID_ROWS - 4):
        kr0 = int(np.clip(r0 - 4, 0, GRID_ROWS - NA_KROWS))
        rq, rk = r0 + a, kr0 + rho
        start = np.clip(rq - NA_WIN_R // 2, 0, GRID_ROWS - NA_WIN_R)
        vr = (rk[None, :] >= start[:, None]) & (rk[None, :] < start[:, None] + NA_WIN_R)
        cs = np.clip(cq - NA_WIN_C // 2, 0, GRID_W - NA_WIN_C)
        vc = (ck[None, :] >= cs[:, None]) & (ck[None, :] < cs[:, None] + NA_WIN_C)
        val = vr[:, None, :, None] & vc[None, :, None, :]
        masks.append(np.where(val, 0.0, -np.inf).reshape(TB, NA_KEYS))
    masks.append(np.full((TB, NA_KEYS), -np.inf))
    t["na_mask"] = np.stack(masks).astype(np.float32)
    n = np.arange(128)
    clipc = lambda m: np.clip(m, -(NA_WIN_C - 1), NA_WIN_C - 1) + NA_WIN_C - 1
    t["na_lane_l"] = clipc(np.where(n < 64, n, n - 128)).astype(np.int32)
    t["na_lane_r"] = clipc(n - 64).astype(np.int32)

    rel = np.arange(SWA_KEYS)[None, :] - np.arange(TB)[:, None]
    swa = [np.where(np.abs(rel + off) <= SWA_WINDOW, 0.0, -np.inf) for off in (0, -SWA_WINDOW, -2 * SWA_WINDOW)]
    swa.append(np.full((TB, SWA_KEYS), -np.inf))
    t["swa_mask"] = np.stack(swa).astype(np.float32)

    def cs_tab(num, den):
        ang = 2.0 * np.pi * (num % den) / den
        return np.cos(ang), np.sin(ang)

    k1, t1 = np.arange(64)[:, None], np.arange(64)[None, :]
    c64, s64 = cs_tab(k1 * t1, 64)
    t["f64"] = np.concatenate([c64, -s64], axis=0).astype(np.float32)
    k2, t2 = np.arange(128)[:, None], np.arange(128)[None, :]
    g = []
    for kk in range(64):
        gc, gs = cs_tab((64 * k2 + kk) * t2, SEQ)
        g.append(np.block([[gc, gs], [-gs, gc]]))
    t["g_stage2"] = np.stack(g).astype(np.float32)
    kc, tc = np.arange(CTX)[:, None], np.arange(CTX)[None, :]
    cc, sc = cs_tab(kc * tc, CTX)
    t["f_ctx"] = np.concatenate([cc, -sc], axis=0).astype(np.float32)
    ch = np.arange(256)
    same = (ch[:, None] // 64) == (ch[None, :] // 64)
    c_ch, s_ch = cs_tab((ch[:, None] % 64) * (ch[None, :] % 64), 64)
    t["chan_cs"] = np.concatenate([np.where(same, c_ch, 0.0), np.where(same, s_ch, 0.0)], axis=0).astype(np.float32)
    t["head_ones"] = same.astype(np.float32)

    balls, lmasks = [], []
    for rev in (False, True):
        tau = (CHUNK - 1 - np.arange(CHUNK)) if rev else np.arange(CHUNK)
        tt, tj = tau[:, None], tau[None, :]
        blocks = [(tj <= tt), (tj > tt)]
        masks = [tt == tj]
        for h in (32, 16, 8, 4, 2, 1):
            up = ((tt // h) % 2) == 1
            mid = (tt // (2 * h)) * (2 * h) + h - 1
            blocks.append(np.where(up, (tj > mid) & (tj <= tt), (tj > tt) & (tj <= mid)))
            up_s = ((tj // h) % 2) == 1
            masks.append((tt // (2 * h) == tj // (2 * h)) & up & ~up_s)
        balls.append(np.concatenate(blocks, axis=0))
        lmasks.append(np.stack([np.tile(m, (4, 1)) for m in masks]))
    t["hgrn_b"] = np.stack(balls).astype(np.float32)
    t["hgrn_mask"] = np.stack(lmasks).astype(np.float32)
    return t


def _ada_kernel(cb_ref, w_ref, b_ref, o_ref, s_ref):
    tn = w_ref.shape[1]
    nrep = tn // 128

    @pl.when((pl.program_id(0) == 0) & (pl.program_id(1) == 0))
    def _():
        cb = cb_ref[...]
        s_ref[...] = cb * jax.nn.sigmoid(cb)

    def body(kc, accs):
        k0 = pl.multiple_of(kc * 8, 8)
        wt = w_ref[pl.ds(k0, 8), :]
        return tuple(accs[r] + wt * jnp.concatenate([s_ref[r, pl.ds(k0, 8), :]] * nrep, axis=1) for r in range(3))

    accs = lax.fori_loop(0, D_MODEL // 8, body, tuple(jnp.zeros((8, tn), F32) for _ in range(3)), unroll=8)
    for r in range(3):
        o_ref[r] = jnp.sum(accs[r], axis=0, keepdims=True) + b_ref[...]


def _ada_call(cb, w_ada, b_ada):
    tn = 1536
    return pl.pallas_call(
        _ada_kernel,
        grid=(DEPTH, 6 * D_MODEL // tn),
        in_specs=[
            pl.BlockSpec((3, D_MODEL, 128), lambda l, j: (0, 0, 0)),
            pl.BlockSpec((None, D_MODEL, tn), lambda l, j: (l, 0, j)),
            pl.BlockSpec((None, 1, tn), lambda l, j: (l, 0, j)),
        ],
        out_specs=pl.BlockSpec((None, 3, 1, tn), lambda l, j: (l, 0, 0, j)),
        out_shape=jax.ShapeDtypeStruct((DEPTH, 3, 1, 6 * D_MODEL), F32),
        scratch_shapes=[pltpu.VMEM((3, D_MODEL, 128), F32)],
        compiler_params=_cparams(("arbitrary", "arbitrary")),
        name="adaln",
    )(cb, w_ada, b_ada.reshape(DEPTH, 1, 6 * D_MODEL))


def _mod_specs(col):
    return [pl.BlockSpec((None, 1, D_MODEL), lambda b, i: (2, 0, col)),
            pl.BlockSpec((None, 1, D_MODEL), lambda b, i: (b, 0, col))]


def _row_parts(tm, with_ctx):
    return [(0, tm - CTX), (tm - CTX, tm)] if with_ctx else [(0, tm // 2), (tm // 2, tm)]


def _mod(mc_ref, mb_ref, part, with_ctx):
    if not with_ctx or part == 0:
        return mb_ref[...]
    last = pl.program_id(1) == pl.num_programs(1) - 1
    return jnp.where(last, mc_ref[...], mb_ref[...])


def _modulated(x, g_ref, sh_refs, sc_refs, part, with_ctx):
    gs = g_ref[...] * (1.0 + _mod(*sc_refs, part, with_ctx))
    return (_rms(x) * gs + _mod(*sh_refs, part, with_ctx)).astype(BF16)


def _gated_residual(x, y, gp_ref, gt_refs, part, with_ctx):
    return x + (_mod(*gt_refs, part, with_ctx) * gp_ref[...]) * _rms(y)


def _rms(x):
    return x * lax.rsqrt(jnp.mean(x * x, axis=-1, keepdims=True) + EPS)


def _x_specs(split):
    if not split:
        return [pl.BlockSpec((None, TM_ALL, D_MODEL), lambda b, i: (b, i, 0))]
    nx = SEQ // TB
    return [pl.BlockSpec((None, TB, D_MODEL), lambda b, i, j=j: (b, jnp.minimum(3 * i + j, nx - 1), 0))
            for j in range(TM_ALL // TB)] + [pl.BlockSpec((None, CTX, D_MODEL), lambda b, i: (b, 0, 0))]


def _x_part(x_refs, part, lo, hi):
    if len(x_refs) == 1:
        return x_refs[0][lo:hi, :]
    xa, xb, xc, cx = x_refs
    if part == 0:
        return jnp.concatenate([xa[...], xb[...]], axis=0)
    last = pl.program_id(1) == pl.num_programs(1) - 1
    return jnp.where(last, cx[...], xc[...])


def _first_step():
    return (pl.program_id(0) == 0) & (pl.program_id(1) == 0)


def _proj_kernel(*refs, nx):
    (g_ref, shc_ref, shb_ref, scc_ref, scb_ref, w32_ref, cos_ref, sin_ref,
     att_ref, f32_ref, knt_ref, kst_ref, w_ref) = refs[nx:]

    @pl.when(_first_step())
    def _():
        w_ref[...] = w32_ref[...].astype(BF16)

    for part, (lo, hi) in enumerate(_row_parts(TM_ALL, True)):
        rows = slice(lo, hi)
        h = _modulated(_x_part(refs[:nx], part, lo, hi), g_ref, (shc_ref, shb_ref), (scc_ref, scb_ref), part, True)
        cos, sin = cos_ref[rows, :], sin_ref[rows, :]
        first = (lax.broadcasted_iota(jnp.int32, (hi - lo, 128), 1) % 32) < 16

        def rope(v):
            return v * cos + jnp.where(first, pltpu.roll(v, 112, 1), pltpu.roll(v, 16, 1)) * sin

        a = _dot(h, w_ref[:, 0:768])
        att_ref[rows, 0:256] = (a[:, 0:256] * Q_SCALE).astype(BF16)
        knt_ref[:, rows] = a[:, 256:512].T.astype(BF16)
        att_ref[rows, 256:512] = a[:, 512:768].astype(BF16)
        f32_ref[rows, 0:256] = _dot(h, w_ref[:, 768:1024])
        s = _dot(h, w_ref[:, 1024:1536])
        att_ref[rows, 512:640] = (rope(s[:, 0:128]) * Q_SCALE).astype(BF16)
        att_ref[rows, 640:768] = (rope(s[:, 128:256]) * Q_SCALE).astype(BF16)
        kst_ref[:, rows] = rope(s[:, 256:384]).T.astype(BF16)
        att_ref[rows, 768:896] = s[:, 384:512].astype(BF16)
        f32_ref[rows, 256:1536] = _dot(h, w_ref[:, 1536:2816])


def _proj_call(x_parts, g, mods, w_in, l, cos, sin):
    b = x_parts[0].shape[0]
    tm = TM_ALL
    split = len(x_parts) == 2
    x_args = (x_parts[0],) * 3 + (x_parts[1],) if split else x_parts
    return pl.pallas_call(
        functools.partial(_proj_kernel, nx=len(x_args)),
        grid=(b, T_ALL // tm),
        in_specs=[
            *_x_specs(split),
            _const_spec((1, D_MODEL)),
            *_mod_specs(0), *_mod_specs(1),
            _layer_spec((D_MODEL, D_IN), l),
            pl.BlockSpec((tm, 128), lambda b, i: (i, 0)),
            pl.BlockSpec((tm, 128), lambda b, i: (i, 0)),
        ],
        out_specs=[
            pl.BlockSpec((None, tm, ATT_W), lambda b, i: (b, i, 0)),
            pl.BlockSpec((None, tm, F32_W), lambda b, i: (b, i, 0)),
            pl.BlockSpec((None, 256, tm), lambda b, i: (b, 0, i)),
            pl.BlockSpec((None, 128, tm), lambda b, i: (b, 0, i)),
        ],
        out_shape=[jax.ShapeDtypeStruct((b, T_ALL, ATT_W), BF16), jax.ShapeDtypeStruct((b, T_ALL, F32_W), F32),
                   jax.ShapeDtypeStruct((b, 256, T_ALL), BF16), jax.ShapeDtypeStruct((b, 128, T_ALL), BF16)],
        scratch_shapes=[pltpu.VMEM((D_MODEL, D_IN), BF16)],
        compiler_params=_cparams(("arbitrary", "arbitrary")),
        name="proj_in",
    )(*x_args, g, mods, mods, mods, mods, w_in, cos, sin)


def _na_bias_kernel(vl_ref, vr_ref, o_ref):
    lane = lax.broadcasted_iota(jnp.int32, (GRID_W, 128), 1)
    for e in range(NA_DR_ROWS - 1):
        left = pltpu.roll(jnp.broadcast_to(vl_ref[e:e + 1, :], (GRID_W, 128)), 0, 1, stride=1, stride_axis=0)
        right = pltpu.roll(jnp.broadcast_to(vr_ref[e + 1:e + 2, :], (GRID_W, 128)), 0, 1, stride=1, stride_axis=0)
        o_ref[e] = jnp.where(lane < 64, left, right)


def _na_bias_call(vl, vr):
    spec = pl.BlockSpec((None, NA_DR_ROWS, 128), lambda h: (h, 0, 0))
    return pl.pallas_call(
        _na_bias_kernel,
        grid=(4,),
        in_specs=[spec, spec],
        out_specs=pl.BlockSpec((None, NA_DR_ROWS - 1, GRID_W, 128), lambda h: (h, 0, 0, 0)),
        out_shape=jax.ShapeDtypeStruct((4, NA_DR_ROWS - 1, GRID_W, 128), F32),
        compiler_params=_cparams(("arbitrary",)),
        name="na_bias",
    )(vl, vr)


AT_ROWS = 16


def _attn_block_variant(b, i):
    return (jnp.where(i == CTX_BLK, 3, jnp.where(i == 0, 0, jnp.where(i == CTX_BLK - 1, 2, 1))), 0, 0)


def _lane_half(shape, j):
    lane = lax.broadcasted_iota(jnp.int32, shape, 1)
    return (lane >= 64 * j) & (lane < 64 * (j + 1))


def _na_window(i):
    is_ctx = i == CTX_BLK
    r0 = i * 4
    kr0 = jnp.where(is_ctx, 0, jnp.clip(r0 - 4, 0, GRID_ROWS - NA_KROWS))
    delta = jnp.where(is_ctx, 3, kr0 - r0 + NA_WIN_R - 1) + NA_DR_PAD
    return pl.multiple_of(kr0 * GRID_W, 128), delta


def _na_scores(i, q_ref, k_ref, s_ref):
    kstart, _ = _na_window(i)
    for hp in range(2):
        cs = slice(hp * 128, (hp + 1) * 128)
        q2 = q_ref[:, cs]
        kw, kc = k_ref[cs, pl.ds(kstart, NA_KEYS)], k_ref[cs, SEQ:T_ALL]
        qm = jnp.concatenate([jnp.where(_lane_half((TB, 128), j), q2, jnp.zeros_like(q2)) for j in range(2)], axis=0)
        s_ref[hp, :, 0:CTX] = _dot(qm, kc)
        s_ref[hp, :, CTX:] = _dot(qm, kw)


def _na_softmax(i, s_ref, mask_ref, t2_ref, p_ref):
    _, delta = _na_window(i)
    for hp in range(2):
        for r in range(2 * TB // AT_ROWS):
            rows = slice(r * AT_ROWS, (r + 1) * AT_ROWS)
            j, rq = divmod(r, TB // AT_ROWS)
            a, sub = divmod(rq, GRID_W // AT_ROWS)
            trow = slice(sub * AT_ROWS, (sub + 1) * AT_ROWS)
            bias = jnp.concatenate(
                [t2_ref[2 * hp + j, delta + 2 * rp - a, trow, :] for rp in range(NA_KROWS // 2)], axis=1)
            s_nb = s_ref[hp, rows, CTX:] + bias + mask_ref[rq * AT_ROWS:(rq + 1) * AT_ROWS, :]
            s_c = s_ref[hp, rows, 0:CTX]
            m = jnp.maximum(jnp.max(s_nb, axis=-1, keepdims=True), jnp.max(s_c, axis=-1, keepdims=True))
            p_ref[hp, rows, 0:CTX] = jnp.exp((s_c - m).astype(BF16))
            p_ref[hp, rows, CTX:] = jnp.exp((s_nb - m).astype(BF16))


def _na_values(i, p_ref, v_ref, o_ref):
    kstart, _ = _na_window(i)
    lane = lax.broadcasted_iota(jnp.int32, (TB, 128), 1)
    for hp in range(2):
        cs = slice(hp * 128, (hp + 1) * 128)
        vw, vc = v_ref[pl.ds(kstart, NA_KEYS), cs], v_ref[SEQ:T_ALL, cs]
        o2 = (_dot(p_ref[hp, :, 0:CTX], jnp.concatenate([vc, jnp.ones_like(vc)], axis=1))
              + _dot(p_ref[hp, :, CTX:], jnp.concatenate([vw, jnp.ones_like(vw)], axis=1)))
        top = o2[0:TB, 0:128] * (1.0 / o2[0:TB, 128:256])
        bot = o2[TB:2 * TB, 0:128] * (1.0 / o2[TB:2 * TB, 128:256])
        o_ref[:, cs] = jnp.where(lane < 64, top, bot).astype(BF16)


def _swa_window(i):
    return pl.multiple_of(jnp.clip(i * TB - SWA_WINDOW, 0, SEQ - SWA_KEYS), SWA_WINDOW)


def _swa_scores(i, q_ref, k_ref, s_ref):
    kw, kc = k_ref[:, pl.ds(_swa_window(i), SWA_KEYS)], k_ref[:, SEQ:T_ALL]
    for g in range(2):
        q2 = q_ref[:, g * 128:(g + 1) * 128].astype(F32)
        qa = jnp.concatenate([jnp.where(_lane_half((TB, 128), g), q2 if j == g else pltpu.roll(q2, 64, 1), 0.0)
                              for j in range(2)], axis=0).astype(BF16)
        s_ref[g, :, 0:CTX] = _dot(qa, kc)
        s_ref[g, :, CTX:] = _dot(qa, kw)


def _swa_softmax(sink_ref, s_ref, mask_ref, p_ref, esink_ref):
    for g in range(2):
        for r in range(2 * TB // AT_ROWS):
            rows = slice(r * AT_ROWS, (r + 1) * AT_ROWS)
            j, rq = divmod(r, TB // AT_ROWS)
            sink = sink_ref[2 * g + j]
            s_l = s_ref[g, rows, CTX:] + mask_ref[rq * AT_ROWS:(rq + 1) * AT_ROWS, :]
            s_c = s_ref[g, rows, 0:CTX]
            m = jnp.maximum(jnp.maximum(jnp.max(s_l, axis=-1, keepdims=True),
                                        jnp.max(s_c, axis=-1, keepdims=True)), sink)
            p_ref[g, rows, 0:CTX] = jnp.exp((s_c - m).astype(BF16))
            p_ref[g, rows, CTX:] = jnp.exp((s_l - m).astype(BF16))
            esink_ref[g, rows, :] = jnp.exp(jnp.broadcast_to(sink - m, (AT_ROWS, 128)))


def _swa_values(i, p_ref, esink_ref, v_ref, o_ref):
    vw, vc = v_ref[pl.ds(_swa_window(i), SWA_KEYS), :], v_ref[SEQ:T_ALL, :]
    lane = lax.broadcasted_iota(jnp.int32, (TB, 128), 1)
    vc2 = jnp.concatenate([vc, jnp.ones_like(vc)], axis=1)
    vw2 = jnp.concatenate([vw, jnp.ones_like(vw)], axis=1)
    for g in range(2):
        o2 = _dot(p_ref[g, :, 0:CTX], vc2) + _dot(p_ref[g, :, CTX:], vw2)
        o = o2[:, 0:128] * (1.0 / (o2[:, 128:256] + esink_ref[g]))
        outs = [o[j * TB:(j + 1) * TB] if j == g else pltpu.roll(o[j * TB:(j + 1) * TB], 64, 1) for j in range(2)]
        o_ref[:, g * 128:(g + 1) * 128] = jnp.where(lane < 64, outs[0], outs[1]).astype(BF16)


def _attn_kernel(sink_ref, qn_ref, kn_ref, vn_ref, maskn_ref, t2_ref, qs_ref, ks_ref, vs_ref, masks_ref,
                 on_ref, os_ref, sn_ref, pn_ref, ss_ref, ps_ref, esink_ref):
    i = pl.program_id(1)
    _na_scores(i, qn_ref, kn_ref, sn_ref)
    _swa_scores(i, qs_ref, ks_ref, ss_ref)
    _na_softmax(i, sn_ref, maskn_ref, t2_ref, pn_ref)
    _swa_softmax(sink_ref, ss_ref, masks_ref, ps_ref, esink_ref)
    _na_values(i, pn_ref, vn_ref, on_ref)
    _swa_values(i, ps_ref, esink_ref, vs_ref, os_ref)


def _attn_call(u_att, kn_t, ks_t, sink, na_mask, t2, swa_mask, with_ctx):
    b = u_att.shape[0]
    nb = NBLK if with_ctx else CTX_BLK
    blk = lambda width, col: pl.BlockSpec((None, TB, width), lambda b, i: (b, i, col))
    full = lambda width, col: pl.BlockSpec((None, T_ALL, width), lambda b, i: (b, 0, col))
    keys = lambda width: pl.BlockSpec((None, width, T_ALL), lambda b, i: (b, 0, 0))
    out = pl.BlockSpec((None, TB, 256), lambda b, i: (b, i, 0))
    return pl.pallas_call(
        _attn_kernel,
        grid=(b, nb),
        in_specs=[
            pl.BlockSpec(memory_space=pltpu.SMEM),
            blk(256, 0), keys(256), full(256, 1),
            pl.BlockSpec((None, TB, NA_KEYS), _attn_block_variant),
            _const_spec((4, NA_DR_ROWS - 1, GRID_W, 128)),
            blk(256, 2), keys(128), full(128, 6),
            pl.BlockSpec((None, TB, SWA_KEYS), _attn_block_variant),
        ],
        out_specs=[out, out],
        out_shape=[jax.ShapeDtypeStruct((b, nb * TB, 256), BF16)] * 2,
        scratch_shapes=[pltpu.VMEM((2, 2 * TB, CTX + NA_KEYS), F32), pltpu.VMEM((2, 2 * TB, CTX + NA_KEYS), BF16),
                        pltpu.VMEM((2, 2 * TB, CTX + SWA_KEYS), F32), pltpu.VMEM((2, 2 * TB, CTX + SWA_KEYS), BF16),
                        pltpu.VMEM((2, 2 * TB, 128), F32)],
        compiler_params=_cparams(("arbitrary", "arbitrary")),
        name="attn",
    )(sink, u_att, kn_t, u_att, na_mask, t2, u_att, ks_t, u_att, swa_mask)


FN_XP = 136
FN_YP = 136
FN_ZP = 72


def _fnet_kernel(x_ref, f64_ref, g_ref, fc_ref, p_ref, q_ref, xz_ref, zq_ref, y_ref):
    sc_lat, sc_ctx = (64.0 * SEQ) ** -0.5, (64.0 * CTX) ** -0.5
    pq = _dot(fc_ref[...], x_ref[SEQ:T_ALL, :].astype(BF16))
    p_ref[SEQ:T_ALL, :] = (pq[0:CTX] * sc_ctx).astype(BF16)
    q_ref[SEQ:T_ALL, :] = (pq[CTX:2 * CTX] * sc_ctx).astype(BF16)
    f64 = f64_ref[...]

    def repitch(t1, carry):
        xz_ref[pl.ds(pl.multiple_of(t1 * FN_XP, 8), 128), :] = x_ref[pl.ds(pl.multiple_of(t1 * 128, 128), 128), :]
        return carry

    lax.fori_loop(0, 64, repitch, 0, unroll=4)

    def stage1(t2, carry):
        xs = xz_ref[pl.ds(t2, 64, stride=FN_XP), :].astype(BF16)
        y_ref[pl.ds(t2, 128, stride=FN_YP), :] = _dot(f64, xs)
        return carry

    lax.fori_loop(0, 128, stage1, 0, unroll=16)

    def stage2(k1, carry):
        yre = y_ref[pl.ds(pl.multiple_of(k1 * FN_YP, 8), 128), :]
        yim = y_ref[pl.ds(pl.multiple_of((64 + k1) * FN_YP, 8), 128), :]
        ycat = jnp.concatenate([yre, yim], axis=0).astype(BF16)
        pq = _dot(g_ref[k1], ycat)
        xz_ref[pl.ds(k1, 128, stride=FN_ZP), :] = pq[0:128] * sc_lat
        zq_ref[pl.ds(k1, 128, stride=FN_ZP), :] = pq[128:256] * sc_lat
        return carry

    lax.fori_loop(0, 64, stage2, 0, unroll=16)

    def emit(k2, carry):
        src = pl.ds(pl.multiple_of(k2 * FN_ZP, 8), 64)
        dst = pl.ds(pl.multiple_of(k2 * 64, 64), 64)
        p_ref[dst, :] = xz_ref[src, :].astype(BF16)
        q_ref[dst, :] = zq_ref[src, :].astype(BF16)
        return carry

    lax.fori_loop(0, 128, emit, 0, unroll=4)


def _fnet_call(u_f32, f64, g2, fc):
    b = u_f32.shape[0]
    spec = pl.BlockSpec((None, T_ALL, 128), lambda b, j: (b, 0, j))
    return pl.pallas_call(
        _fnet_kernel,
        grid=(b, 2),
        in_specs=[spec, _const_spec((128, 64)), _const_spec((64, 256, 256)), _const_spec((2 * CTX, CTX))],
        out_specs=[spec, spec],
        out_shape=[jax.ShapeDtypeStruct((b, T_ALL, 256), BF16)] * 2,
        scratch_shapes=[pltpu.VMEM((128 * FN_ZP, 128), F32), pltpu.VMEM((128 * FN_ZP, 128), F32),
                        pltpu.VMEM((128 * FN_YP, 128), F32)],
        compiler_params=_cparams(("arbitrary", "arbitrary")),
        name="fnet_dft",
    )(u_f32, f64, g2, fc)


def _hgrn_gates(qd, fz, gl, lb_zero):
    q = qd * jax.nn.sigmoid(qd)
    sp = jnp.log(1.0 + jnp.exp(-jnp.abs(fz)))
    ls = jnp.minimum(fz, 0.0) - sp
    lk = jnp.minimum(-fz, 0.0) - sp
    if lb_zero:
        return q, jnp.exp(lk), ls
    b_ = gl[1:2] + ls
    a_ = gl[0:1]
    logf = jnp.maximum(a_, b_) + jnp.log(1.0 + jnp.exp(-jnp.abs(a_ - b_)))
    return q, jnp.exp(gl[1:2] + lk), logf


def _dot01(b01, x):
    h1 = x.astype(BF16)
    r1 = x - h1.astype(F32)
    h2 = r1.astype(BF16)
    h3 = (r1 - h2.astype(F32)).astype(BF16)
    return _dot(b01, h1) + _dot(b01, h2) + _dot(b01, h3)


def _head_masks(rows):
    lane = lax.broadcasted_iota(jnp.int32, (rows, 256), 1)
    return [(lane >= 64 * h) & (lane < 64 * (h + 1)) for h in range(4)]


def _stack_heads(w, hms):
    return jnp.concatenate([jnp.where(hm, w, jnp.zeros_like(w)) for hm in hms], axis=0)


def _state_update(st_ref, st, e_last, vb, ks):
    ktv = _dot_tn(vb, ks)
    r2 = lax.broadcasted_iota(jnp.int32, (256, 256), 0) // 64
    c2 = lax.broadcasted_iota(jnp.int32, (256, 256), 1) // 64
    st_ref[...] = st * e_last + jnp.where(r2 == c2, ktv, 0.0)


def _hgrn_chunk_exact(qd, fz, v, st_ref, gl, ball, lmask_ref, d, rev, lb_zero):
    q, kk, logf = _hgrn_gates(qd, fz, gl, lb_zero)
    e = jnp.exp(_dot01(ball, logf))
    row = lax.broadcasted_iota(jnp.int32, (CHUNK, 256), 0)
    tau = (CHUNK - 1 - row) if rev else row
    hms = _head_masks(CHUNK)
    att = _dot_nt(_stack_heads(q.astype(BF16), hms), kk.astype(BF16)) * lmask_ref[d, 0]
    for li, sh in enumerate((5, 4, 3, 2, 1, 0)):
        up = ((tau >> sh) & 1) == 1
        w = (jnp.where(up, q, kk) * e[128 + 64 * li:192 + 64 * li]).astype(BF16)
        att = att + _dot_nt(_stack_heads(w, hms), w) * lmask_ref[d, 1 + li]
    vb = v.astype(BF16)
    st = st_ref[...]
    o = _dot_nt((q * e[0:64]).astype(BF16), st.astype(BF16))
    for h in range(4):
        o = o + _dot(att[64 * h:64 * (h + 1)].astype(BF16), jnp.where(hms[h], vb, jnp.zeros_like(vb)))
    tl = 0 if rev else CHUNK - 1
    _state_update(st_ref, st, e[tl:tl + 1], vb, (kk * e[64:128]).astype(BF16))
    return o


def _hgrn_chunk_prep(sl, q_ref, f_ref, gl, tri, rev, lb_zero):
    q, kk, logf = _hgrn_gates(q_ref[sl, :], f_ref[sl, :], gl, lb_zero)
    cum = _dot01(tri, logf)
    first_row, mid_row, last_row = (CHUNK - 1, CHUNK // 2, 0) if rev else (0, CHUNK // 2 - 1, CHUNK - 1)
    cum_mid, cum_last = cum[mid_row:mid_row + 1], cum[last_row:last_row + 1]
    a = cum - cum_mid
    a_range = jnp.maximum(jnp.max(a[first_row:first_row + 1]), jnp.max(-a[last_row:last_row + 1]))
    return q * jnp.exp(a), kk * jnp.exp(-a), cum_mid, cum_last, a_range


def _hgrn_chunk_fast(sl, qt, kt, cum_mid, cum_last, v_ref, st_ref, o_ref, rev):
    hms = _head_masks(CHUNK)
    r = lax.broadcasted_iota(jnp.int32, (4 * CHUNK, CHUNK), 0) & (CHUNK - 1)
    s = lax.broadcasted_iota(jnp.int32, (4 * CHUNK, CHUNK), 1)
    causal = (s >= r) if rev else (s <= r)
    vb = v_ref[sl, :].astype(BF16)
    att = jnp.where(causal, _dot_nt(_stack_heads(qt.astype(BF16), hms), kt.astype(BF16)), 0.0)
    st = st_ref[...]
    o = _dot_nt((qt * jnp.exp(cum_mid)).astype(BF16), st.astype(BF16))
    for h in range(4):
        o = o + _dot(att[64 * h:64 * (h + 1)].astype(BF16), jnp.where(hms[h], vb, jnp.zeros_like(vb)))
    o_ref[sl, :] = o
    _state_update(st_ref, st, jnp.exp(cum_last), vb, (kt * jnp.exp(cum_last - cum_mid)).astype(BF16))


def _hgrn_kernel(qf_ref, ff_ref, vf_ref, qb_ref, fb_ref, vb_ref, gl_ref, ball_ref, lmask_ref,
                 of_ref, ob_ref, sf_ref, sb_ref, sf0_ref, sb0_ref, *, lb_zero):
    @pl.when(pl.program_id(0) == 0)
    def _():
        sf_ref[...] = jnp.zeros_like(sf_ref)
        sb_ref[...] = jnp.zeros_like(sb_ref)

    sf0_ref[...] = sf_ref[...]
    sb0_ref[...] = sb_ref[...]
    chains = []
    for b in range(qf_ref.shape[0]):
        chains.append((qf_ref.at[b], ff_ref.at[b], vf_ref.at[b], sf_ref.at[b], of_ref.at[b], False, 0))
        chains.append((qb_ref.at[b], fb_ref.at[b], vb_ref.at[b], sb_ref.at[b], ob_ref.at[b], True, 1))
    rows = lambda rev, s: slice((NCHUNK - 1 - s if rev else s) * CHUNK, (NCHUNK - s if rev else s + 1) * CHUNK)

    def prep(s):
        return [_hgrn_chunk_prep(rows(rev, s), q_ref, f_ref, gl_ref[d], ball_ref[d, 0:CHUNK, :], rev, lb_zero)
                for q_ref, f_ref, v_ref, st_ref, o_ref, rev, d in chains]

    rng, ready = 0.0, prep(0)
    for s in range(NCHUNK):
        nxt = prep(s + 1) if s + 1 < NCHUNK else None
        for (q_ref, f_ref, v_ref, st_ref, o_ref, rev, d), (qt, kt, cum_mid, cum_last, amax) in zip(chains, ready):
            rng = jnp.maximum(rng, amax)
            _hgrn_chunk_fast(rows(rev, s), qt, kt, cum_mid, cum_last, v_ref, st_ref, o_ref, rev)
        ready = nxt

    @pl.when(jnp.logical_not(rng <= HGRN_FAST_RANGE))
    def _():
        sf_ref[...] = sf0_ref[...]
        sb_ref[...] = sb0_ref[...]

        def body(c, carry):
            for q_ref, f_ref, v_ref, st_ref, o_ref, rev, d in chains:
                r = pl.ds(pl.multiple_of((NCHUNK - 1 - c if rev else c) * CHUNK, CHUNK), CHUNK)
                o_ref[r, :] = _hgrn_chunk_exact(q_ref[r, :], f_ref[r, :], v_ref[r, :], st_ref,
                                                gl_ref[d], ball_ref[d], lmask_ref, d, rev, lb_zero)
            return carry

        lax.fori_loop(0, NCHUNK, body, 0)


def _hgrn_call(u_f32, gl, ball, lmask, lb_zero):
    b = u_f32.shape[0]

    def fwd(col):
        return pl.BlockSpec((b, TB, 256), lambda i: (0, jnp.where(i == 0, CTX_BLK, i - 1), col))

    def bwd(col):
        return pl.BlockSpec((b, TB, 256), lambda i: (0, jnp.where(i == 0, CTX_BLK, CTX_BLK - i), col))

    return pl.pallas_call(
        functools.partial(_hgrn_kernel, lb_zero=lb_zero),
        grid=(NBLK,),
        in_specs=[fwd(1), fwd(2), fwd(4), bwd(1), bwd(3), bwd(4),
                  _const_spec((2, 2, 256)), _const_spec((2, 8 * CHUNK, CHUNK)),
                  _const_spec((2, 7, 4 * CHUNK, CHUNK))],
        out_specs=[fwd(0), bwd(0)],
        out_shape=[jax.ShapeDtypeStruct((b, T_ALL, 256), F32)] * 2,
        scratch_shapes=[pltpu.VMEM((b, 256, 256), F32)] * 4,
        compiler_params=_cparams(("arbitrary",)),
        name="hgrn_scan",
    )(u_f32, u_f32, u_f32, u_f32, u_f32, u_f32, gl, ball, lmask)


def _outproj_kernel(*refs, nx, with_ctx):
    (oa_ref, p_ref, q_ref, oc_ref, of_ref, ob_ref, gd_ref, hg_ref, hh_ref, cs_ref,
     wf32_ref, wo32_ref, gp_ref, gtc_ref, gtb_ref, o_ref, wf_ref, wo_ref) = refs[nx:]

    @pl.when(_first_step())
    def _():
        wf_ref[...] = wf32_ref[...].astype(BF16)
        wo_ref[...] = wo32_ref[...].astype(BF16)

    for part, (r0, r1) in enumerate(_row_parts(o_ref.shape[0], with_ctx)):
        rows = slice(r0, r1)
        z = _dot(p_ref[rows, :], cs_ref[0:256, :]) + _dot(q_ref[rows, :], cs_ref[256:512, :])
        o_b = _dot(z.astype(BF16), wf_ref[...])
        od = of_ref[rows, :] + ob_ref[rows, :]
        sq = od * od
        hi = sq.astype(BF16)
        lo = (sq - hi.astype(F32)).astype(BF16)
        ms = (_dot(hi, hh_ref[...]) + _dot(lo, hh_ref[...])) * (1.0 / HEAD_DIM)
        gd = gd_ref[rows, :]
        o_d = od * lax.rsqrt(ms + EPS) * hg_ref[...] * (gd * jax.nn.sigmoid(gd))
        y = (_dot(oa_ref[rows, :], wo_ref[0:256, :]) + _dot(o_b.astype(BF16), wo_ref[256:512, :])
             + _dot(oc_ref[rows, :], wo_ref[512:768, :]) + _dot(o_d.astype(BF16), wo_ref[768:1024, :]))
        x = _x_part(refs[:nx], part, r0, r1)
        o_ref[rows, :] = _gated_residual(x, y, gp_ref, (gtc_ref, gtb_ref), part, with_ctx)


def _outproj_call(x_parts, o_a, p, q, o_c, o_f, o_bk, u_f32, hg, hh, cs, w_fnet, w_out, l, g_post, mods, with_ctx):
    b = x_parts[0].shape[0]
    tm, t = (TM_ALL, T_ALL) if with_ctx else (TM_LAT, SEQ)
    col = lambda c: pl.BlockSpec((None, tm, 256), lambda b, i: (b, i, c))
    split = len(x_parts) == 2
    x_args = (x_parts[0],) * 3 + (x_parts[1],) if split else x_parts
    x_specs = _x_specs(True) if split else [pl.BlockSpec((None, tm, D_MODEL), lambda b, i: (b, i, 0))]
    return pl.pallas_call(
        functools.partial(_outproj_kernel, nx=len(x_args), with_ctx=with_ctx),
        grid=(b, t // tm),
        in_specs=[
            *x_specs,
            col(0), col(0), col(0), col(0), col(0), col(0), col(5),
            _const_spec((1, 256)), _const_spec((256, 256)), _const_spec((512, 256)),
            _layer_spec((256, 256), l), _layer_spec((D_MODEL, D_MODEL), l), _const_spec((1, D_MODEL)),
            *_mod_specs(2),
        ],
        out_specs=pl.BlockSpec((None, tm, D_MODEL), lambda b, i: (b, i, 0)),
        out_shape=jax.ShapeDtypeStruct((b, t, D_MODEL), F32),
        scratch_shapes=[pltpu.VMEM((256, 256), BF16), pltpu.VMEM((D_MODEL, D_MODEL), BF16)],
        compiler_params=_cparams(("arbitrary", "arbitrary")),
        name="out_proj",
    )(*x_args, o_a, p, q, o_c, o_f, o_bk, u_f32, hg, hh, cs, w_fnet, w_out, g_post, mods, mods)


def _ffn_kernel(x_ref, g_ref, shc_ref, shb_ref, scc_ref, scb_ref, w1_ref, w2_32_ref, gp_ref, gtc_ref, gtb_ref,
                o_ref, w2_ref, *, with_ctx):
    fc = 256

    @pl.when(_first_step())
    def _():
        w2_ref[...] = w2_32_ref[...].astype(BF16)

    def swiglu(h):
        acc = jnp.zeros((h.shape[0], D_MODEL), F32)
        for c in range(D_FF // fc):
            a = _dot(h, w1_ref[:, c * fc:(c + 1) * fc])
            g = _dot(h, w1_ref[:, D_FF + c * fc:D_FF + (c + 1) * fc])
            act = (a * jax.nn.sigmoid(a) * g).astype(BF16)
            acc = acc + _dot(act, w2_ref[c * fc:(c + 1) * fc, :])
        return acc

    parts = list(enumerate(_row_parts(x_ref.shape[0], with_ctx)))
    pre = lambda part, r0, r1: _modulated(x_ref[r0:r1, :], g_ref, (shc_ref, shb_ref), (scc_ref, scb_ref), part, with_ctx)
    post = lambda part, r0, r1, acc: _gated_residual(x_ref[r0:r1, :], acc, gp_ref, (gtc_ref, gtb_ref), part, with_ctx)
    if with_ctx:
        acc = swiglu(jnp.concatenate([pre(part, r0, r1) for part, (r0, r1) in parts], axis=0))
        for part, (r0, r1) in parts:
            o_ref[r0:r1, :] = post(part, r0, r1, acc[r0:r1])
    else:
        for part, (r0, r1) in parts:
            o_ref[r0:r1, :] = post(part, r0, r1, swiglu(pre(part, r0, r1)))


def _ffn_call(xs, g_pre, w1, w2, l, g_post, mods, with_ctx):
    b, t, _ = xs.shape
    tm = TM_ALL if with_ctx else TM_LAT
    return pl.pallas_call(
        functools.partial(_ffn_kernel, with_ctx=with_ctx),
        grid=(b, t // tm),
        in_specs=[
            pl.BlockSpec((None, tm, D_MODEL), lambda b, i: (b, i, 0)),
            _const_spec((1, D_MODEL)),
            *_mod_specs(3), *_mod_specs(4),
            _layer_spec((D_MODEL, 2 * D_FF), l), _layer_spec((D_FF, D_MODEL), l), _const_spec((1, D_MODEL)),
            *_mod_specs(5),
        ],
        out_specs=pl.BlockSpec((None, tm, D_MODEL), lambda b, i: (b, i, 0)),
        out_shape=jax.ShapeDtypeStruct(xs.shape, F32),
        scratch_shapes=[pltpu.VMEM((D_FF, D_MODEL), BF16)],
        compiler_params=_cparams(("arbitrary", "arbitrary")),
        name="ffn",
    )(xs, g_pre, mods, mods, mods, mods, w1, w2, g_post, mods, mods)


def kernel(x, c, ctx, c_ctx, w_ada, b_ada, g_pre_mix, g_post_mix, g_pre_ffn, g_post_ffn, w_in, w_out,
           na_rpb, w_fnet, swa_sink, hgrn_lb_logits, hgrn_norm_g, w_ffn_in, w_ffn_out):
    assert x.shape == (2, SEQ, D_MODEL) and ctx.shape == (2, CTX, D_MODEL) and c.shape == (2, D_MODEL)
    assert w_in.shape == (DEPTH, D_MODEL, D_IN) and w_ffn_in.shape == (DEPTH, D_MODEL, 2 * D_FF)
    assert na_rpb.shape == (DEPTH, 4, 2 * NA_WIN_R - 1, 2 * NA_WIN_C - 1) and swa_sink.shape == (DEPTH, 4)
    tb = _tables()
    bf = lambda name: jnp.asarray(tb[name]).astype(BF16)
    cos, sin = jnp.asarray(tb["rope_cos"]), jnp.asarray(tb["rope_sin"])
    f64, g2, fc, chan_cs, hh = bf("f64"), bf("g_stage2"), bf("f_ctx"), bf("chan_cs"), bf("head_ones")
    ball, lmask = bf("hgrn_b"), jnp.asarray(tb["hgrn_mask"])
    na_mask, swa_mask = jnp.asarray(tb["na_mask"]), jnp.asarray(tb["swa_mask"])

    cc = jnp.concatenate([c, c_ctx[None, :]], axis=0)
    mods = _ada_call(jnp.broadcast_to(cc[:, :, None], (3, D_MODEL, 128)), w_ada, b_ada)

    lb = jnp.cumsum(jax.nn.softmax(hgrn_lb_logits.astype(F32), axis=1), axis=1)
    lb = lb - lb[:, :1]
    gl_all = jnp.stack([jnp.log(lb), jnp.log1p(-lb)], axis=2)

    w_ffn_in_b = w_ffn_in.astype(BF16)

    xs = None
    for l in range(DEPTH):
        with_ctx = l < DEPTH - 1
        row = lambda a: a[l][None, :]
        x_all = (x, ctx) if l == 0 else (xs,)
        x_res = x_all if with_ctx else x_all[:1]
        u_att, u_f32, kn_t, ks_t = _proj_call(x_all, row(g_pre_mix), mods[l], w_in, l, cos, sin)

        rpb = jnp.pad(na_rpb[l], ((0, 0), (NA_DR_PAD, NA_DR_ROWS - NA_DR_PAD - (2 * NA_WIN_R - 1)), (0, 0)))
        t2 = _na_bias_call(rpb[:, :, tb["na_lane_l"]], rpb[:, :, tb["na_lane_r"]])
        o_a, o_c = _attn_call(u_att, kn_t, ks_t, swa_sink[l], na_mask, t2, swa_mask, with_ctx)
        p, q = _fnet_call(u_f32, f64, g2, fc)
        o_f, o_bk = _hgrn_call(u_f32, gl_all[:, l], ball, lmask, lb_zero=(l == 0))

        xs = _outproj_call(x_res, o_a, p, q, o_c, o_f, o_bk, u_f32, row(hgrn_norm_g), hh, chan_cs,
                           w_fnet, w_out, l, row(g_post_mix), mods[l], with_ctx)
        xs = _ffn_call(xs, row(g_pre_ffn), w_ffn_in_b, w_ffn_out, l, row(g_post_ffn), mods[l], with_ctx)
    return xs
```

```python
import functools

import numpy as np
import jax
import jax.numpy as jnp
from jax import lax
from jax.experimental import pallas as pl
from jax.experimental.pallas import tpu as pltpu

F32, BF16 = jnp.float32, jnp.bfloat16

D_MODEL = 1024
SEQ = 8192
DEPTH = 2
GRID_W = 64
GRID_ROWS = SEQ // GRID_W
CTX = 256
HEAD_DIM = 64
NA_WIN_R, NA_WIN_C = 8, 16
SWA_WINDOW = 128
ROPE_THETA = 10000.0
EPS = 1e-6
D_FF = 2816
D_IN = 2816

TB = 256
T_ALL = SEQ + CTX
NBLK = T_ALL // TB
CTX_BLK = SEQ // TB
TM_ALL = 768
TM_LAT = 1024
NA_KROWS = 12
NA_KEYS = NA_KROWS * GRID_W
NA_DR_PAD = 4
NA_DR_ROWS = 24
SWA_KEYS = TB + 2 * SWA_WINDOW
CHUNK = 64
NCHUNK = TB // CHUNK
HGRN_FAST_RANGE = 60.0
Q_SCALE = HEAD_DIM ** -0.5
VMEM_LIMIT = 56 * 1024 * 1024

ATT_W, F32_W = 896, 1536


def _dot(a, b):
    return jnp.dot(a, b, preferred_element_type=F32)


def _dot_nt(a, b):
    return lax.dot_general(a, b, (((1,), (1,)), ((), ())), preferred_element_type=F32)


def _dot_tn(a, b):
    return lax.dot_general(a, b, (((0,), (0,)), ((), ())), preferred_element_type=F32)


def _cparams(sem):
    return pltpu.CompilerParams(dimension_semantics=sem, vmem_limit_bytes=VMEM_LIMIT)


def _const_spec(shape):
    nd = len(shape)
    return pl.BlockSpec(shape, lambda *_: (0,) * nd, pipeline_mode=pl.Buffered(1))


def _layer_spec(shape, l):
    nd = len(shape)
    return pl.BlockSpec((None,) + shape, lambda *_: (l,) + (0,) * nd, pipeline_mode=pl.Buffered(1))


@functools.lru_cache(maxsize=None)
def _tables():
    t = {}
    lane = np.arange(128)
    d = lane % 64
    freq = ROPE_THETA ** (-(d % 16) / 16.0)
    pos = np.arange(SEQ)
    p = np.where(d[None, :] < 32, (pos // GRID_W)[:, None], (pos % GRID_W)[:, None]).astype(np.float64)
    ang = p * freq[None, :]
    sign = np.where((d % 32) < 16, -1.0, 1.0)
    cos = np.concatenate([np.cos(ang), np.ones((CTX, 128))], axis=0)
    sin = np.concatenate([np.sin(ang) * sign[None, :], np.zeros((CTX, 128))], axis=0)
    t["rope_cos"], t["rope_sin"] = cos.astype(np.float32), sin.astype(np.float32)

    masks = []
    a, cq = np.arange(4), np.arange(GRID_W)
    rho, ck = np.arange(NA_KROWS), np.arange(GRID_W)
    for r0 in (0, 8, GRID_ROWS - 4):
        kr0 = int(np.clip(r0 - 4, 0, GRID_ROWS - NA_KROWS))
        rq, rk = r0 + a, kr0 + rho
        start = np.clip(rq - NA_WIN_R // 2, 0, GRID_ROWS - NA_WIN_R)
        vr = (rk[None, :] >= start[:, None]) & (rk[None, :] < start[:, None] + NA_WIN_R)
        cs = np.clip(cq - NA_WIN_C // 2, 0, GRID_W - NA_WIN_C)
        vc = (ck[None, :] >= cs[:, None]) & (ck[None, :] < cs[:, None] + NA_WIN_C)
        val = vr[:, None, :, None] & vc[None, :, None, :]
        masks.append(np.where(val, 0.0, -np.inf).reshape(TB, NA_KEYS))
    masks.append(np.full((TB, NA_KEYS), -np.inf))
    t["na_mask"] = np.stack(masks).astype(np.float32)
    n = np.arange(128)
    clipc = lambda m: np.clip(m, -(NA_WIN_C - 1), NA_WIN_C - 1) + NA_WIN_C - 1
    t["na_lane_l"] = clipc(np.where(n < 64, n, n - 128)).astype(np.int32)
    t["na_lane_r"] = clipc(n - 64).astype(np.int32)

    rel = np.arange(SWA_KEYS)[None, :] - np.arange(TB)[:, None]
    swa = [np.where(np.abs(rel + off) <= SWA_WINDOW, 0.0, -np.inf) for off in (0, -SWA_WINDOW, -2 * SWA_WINDOW)]
    swa.append(np.full((TB, SWA_KEYS), -np.inf))
    t["swa_mask"] = np.stack(swa).astype(np.float32)

    def cs_tab(num, den):
        ang = 2.0 * np.pi * (num % den) / den
        return np.cos(ang), np.sin(ang)

    k1, t1 = np.arange(64)[:, None], np.arange(64)[None, :]
    c64, s64 = cs_tab(k1 * t1, 64)
    t["f64"] = np.concatenate([c64, -s64], axis=0).astype(np.float32)
    k2, t2 = np.arange(128)[:, None], np.arange(128)[None, :]
    g = []
    for kk in range(64):
        gc, gs = cs_tab((64 * k2 + kk) * t2, SEQ)
        g.append(np.block([[gc, gs], [-gs, gc]]))
    t["g_stage2"] = np.stack(g).astype(np.float32)
    kc, tc = np.arange(CTX)[:, None], np.arange(CTX)[None, :]
    cc, sc = cs_tab(kc * tc, CTX)
    t["f_ctx"] = np.concatenate([cc, -sc], axis=0).astype(np.float32)
    ch = np.arange(256)
    same = (ch[:, None] // 64) == (ch[None, :] // 64)
    c_ch, s_ch = cs_tab((ch[:, None] % 64) * (ch[None, :] % 64), 64)
    t["chan_cs"] = np.concatenate([np.where(same, c_ch, 0.0), np.where(same, s_ch, 0.0)], axis=0).astype(np.float32)
    t["head_ones"] = same.astype(np.float32)

    balls, lmasks = [], []
    for rev in (False, True):
        tau = (CHUNK - 1 - np.arange(CHUNK)) if rev else np.arange(CHUNK)
        tt, tj = tau[:, None], tau[None, :]
        blocks = [(tj <= tt), (tj > tt)]
        masks = [tt == tj]
        for h in (32, 16, 8, 4, 2, 1):
            up = ((tt // h) % 2) == 1
            mid = (tt // (2 * h)) * (2 * h) + h - 1
            blocks.append(np.where(up, (tj > mid) & (tj <= tt), (tj > tt) & (tj <= mid)))
            up_s = ((tj // h) % 2) == 1
            masks.append((tt // (2 * h) == tj // (2 * h)) & up & ~up_s)
        balls.append(np.concatenate(blocks, axis=0))
        lmasks.append(np.stack([np.tile(m, (4, 1)) for m in masks]))
    t["hgrn_b"] = np.stack(balls).astype(np.float32)
    t["hgrn_mask"] = np.stack(lmasks).astype(np.float32)
    return t


def _ada_kernel(cb_ref, w_ref, b_ref, o_ref, s_ref):
    tl = 1536
    nrep = tl // 128

    @pl.when((pl.program_id(0) == 0) & (pl.program_id(1) == 0))
    def _():
        cb = cb_ref[...]
        s_ref[...] = cb * jax.nn.sigmoid(cb)

    for lo in range(0, w_ref.shape[1], tl):
        def body(kc, accs):
            k0 = pl.multiple_of(kc * 8, 8)
            wt = w_ref[pl.ds(k0, 8), lo:lo + tl]
            return tuple(accs[r] + wt * jnp.concatenate([s_ref[r, pl.ds(k0, 8), :]] * nrep, axis=1) for r in range(3))

        accs = lax.fori_loop(0, D_MODEL // 8, body, tuple(jnp.zeros((8, tl), F32) for _ in range(3)), unroll=8)
        for r in range(3):
            o_ref[r, :, lo:lo + tl] = jnp.sum(accs[r], axis=0, keepdims=True) + b_ref[:, lo:lo + tl]


def _ada_call(cb, w_ada, b_ada):
    tn = 3072
    return pl.pallas_call(
        _ada_kernel,
        grid=(DEPTH, 6 * D_MODEL // tn),
        in_specs=[
            pl.BlockSpec((3, D_MODEL, 128), lambda l, j: (0, 0, 0)),
            pl.BlockSpec((None, D_MODEL, tn), lambda l, j: (l, 0, j)),
            pl.BlockSpec((None, 1, tn), lambda l, j: (l, 0, j)),
        ],
        out_specs=pl.BlockSpec((None, 3, 1, tn), lambda l, j: (l, 0, 0, j)),
        out_shape=jax.ShapeDtypeStruct((DEPTH, 3, 1, 6 * D_MODEL), F32),
        scratch_shapes=[pltpu.VMEM((3, D_MODEL, 128), F32)],
        compiler_params=_cparams(("arbitrary", "arbitrary")),
        name="adaln",
    )(cb, w_ada, b_ada.reshape(DEPTH, 1, 6 * D_MODEL))


def _mod_specs(col):
    return [pl.BlockSpec((None, 1, D_MODEL), lambda b, i: (2, 0, col)),
            pl.BlockSpec((None, 1, D_MODEL), lambda b, i: (b, 0, col))]


def _row_parts(tm, with_ctx):
    return [(0, tm - CTX), (tm - CTX, tm)] if with_ctx else [(0, tm // 2), (tm // 2, tm)]


def _mod(mc_ref, mb_ref, part, with_ctx):
    if not with_ctx or part == 0:
        return mb_ref[...]
    last = pl.program_id(1) == pl.num_programs(1) - 1
    return jnp.where(last, mc_ref[...], mb_ref[...])


def _modulated(x, g_ref, sh_refs, sc_refs, part, with_ctx):
    gs = g_ref[...] * (1.0 + _mod(*sc_refs, part, with_ctx))
    return (_rms(x) * gs + _mod(*sh_refs, part, with_ctx)).astype(BF16)


def _gated_residual(x, y, gp_ref, gt_refs, part, with_ctx):
    return x + (_mod(*gt_refs, part, with_ctx) * gp_ref[...]) * _rms(y)


def _rms(x):
    return x * lax.rsqrt(jnp.mean(x * x, axis=-1, keepdims=True) + EPS)


def _x_specs(split):
    if not split:
        return [pl.BlockSpec((None, TM_ALL, D_MODEL), lambda b, i: (b, i, 0))]
    nx = SEQ // TB
    return [pl.BlockSpec((None, TB, D_MODEL), lambda b, i, j=j: (b, jnp.minimum(3 * i + j, nx - 1), 0))
            for j in range(TM_ALL // TB)] + [pl.BlockSpec((None, CTX, D_MODEL), lambda b, i: (b, 0, 0))]


def _x_part(x_refs, part, lo, hi):
    if len(x_refs) == 1:
        return x_refs[0][lo:hi, :]
    xa, xb, xc, cx = x_refs
    if part == 0:
        return jnp.concatenate([xa[...], xb[...]], axis=0)
    last = pl.program_id(1) == pl.num_programs(1) - 1
    return jnp.where(last, cx[...], xc[...])


def _first_step():
    return (pl.program_id(0) == 0) & (pl.program_id(1) == 0)


def _proj_kernel(*refs, nx):
    (g_ref, shc_ref, shb_ref, scc_ref, scb_ref, w32_ref, cos_ref, sin_ref,
     att_ref, f32_ref, knt_ref, kst_ref, w_ref) = refs[nx:]

    @pl.when(_first_step())
    def _():
        w_ref[...] = w32_ref[...].astype(BF16)

    for part, (lo, hi) in enumerate(_row_parts(TM_ALL, True)):
        rows = slice(lo, hi)
        h = _modulated(_x_part(refs[:nx], part, lo, hi), g_ref, (shc_ref, shb_ref), (scc_ref, scb_ref), part, True)
        cos, sin = cos_ref[rows, :], sin_ref[rows, :]
        first = (lax.broadcasted_iota(jnp.int32, (hi - lo, 128), 1) % 32) < 16

        def rope(v):
            return v * cos + jnp.where(first, pltpu.roll(v, 112, 1), pltpu.roll(v, 16, 1)) * sin

        a = _dot(h, w_ref[:, 0:768])
        att_ref[rows, 0:256] = (a[:, 0:256] * Q_SCALE).astype(BF16)
        knt_ref[:, rows] = a[:, 256:512].T.astype(BF16)
        att_ref[rows, 256:512] = a[:, 512:768].astype(BF16)
        f32_ref[rows, 0:256] = _dot(h, w_ref[:, 768:1024])
        s = _dot(h, w_ref[:, 1024:1536])
        att_ref[rows, 512:640] = (rope(s[:, 0:128]) * Q_SCALE).astype(BF16)
        att_ref[rows, 640:768] = (rope(s[:, 128:256]) * Q_SCALE).astype(BF16)
        kst_ref[:, rows] = rope(s[:, 256:384]).T.astype(BF16)
        att_ref[rows, 768:896] = s[:, 384:512].astype(BF16)
        f32_ref[rows, 256:1536] = _dot(h, w_ref[:, 1536:2816])


def _proj_call(x_parts, g, mods, w_in, l, cos, sin):
    b = x_parts[0].shape[0]
    tm = TM_ALL
    split = len(x_parts) == 2
    x_args = (x_parts[0],) * 3 + (x_parts[1],) if split else x_parts
    return pl.pallas_call(
        functools.partial(_proj_kernel, nx=len(x_args)),
        grid=(b, T_ALL // tm),
        in_specs=[
            *_x_specs(split),
            _const_spec((1, D_MODEL)),
            *_mod_specs(0), *_mod_specs(1),
            _layer_spec((D_MODEL, D_IN), l),
            pl.BlockSpec((tm, 128), lambda b, i: (i, 0)),
            pl.BlockSpec((tm, 128), lambda b, i: (i, 0)),
        ],
        out_specs=[
            pl.BlockSpec((None, tm, ATT_W), lambda b, i: (b, i, 0)),
            pl.BlockSpec((None, tm, F32_W), lambda b, i: (b, i, 0)),
            pl.BlockSpec((None, 256, tm), lambda b, i: (b, 0, i)),
            pl.BlockSpec((None, 128, tm), lambda b, i: (b, 0, i)),
        ],
        out_shape=[jax.ShapeDtypeStruct((b, T_ALL, ATT_W), BF16), jax.ShapeDtypeStruct((b, T_ALL, F32_W), F32),
                   jax.ShapeDtypeStruct((b, 256, T_ALL), BF16), jax.ShapeDtypeStruct((b, 128, T_ALL), BF16)],
        scratch_shapes=[pltpu.VMEM((D_MODEL, D_IN), BF16)],
        compiler_params=_cparams(("arbitrary", "arbitrary")),
        name="proj_in",
    )(*x_args, g, mods, mods, mods, mods, w_in, cos, sin)


def _na_bias_kernel(vl_ref, vr_ref, o_ref):
    lane = lax.broadcasted_iota(jnp.int32, (GRID_W, 128), 1)
    for e in range(NA_DR_ROWS - 1):
        left = pltpu.roll(jnp.broadcast_to(vl_ref[e:e + 1, :], (GRID_W, 128)), 0, 1, stride=1, stride_axis=0)
        right = pltpu.roll(jnp.broadcast_to(vr_ref[e + 1:e + 2, :], (GRID_W, 128)), 0, 1, stride=1, stride_axis=0)
        o_ref[e] = jnp.where(lane < 64, left, right)


def _na_bias_call(vl, vr):
    spec = pl.BlockSpec((None, NA_DR_ROWS, 128), lambda h: (h, 0, 0))
    return pl.pallas_call(
        _na_bias_kernel,
        grid=(4,),
        in_specs=[spec, spec],
        out_specs=pl.BlockSpec((None, NA_DR_ROWS - 1, GRID_W, 128), lambda h: (h, 0, 0, 0)),
        out_shape=jax.ShapeDtypeStruct((4, NA_DR_ROWS - 1, GRID_W, 128), F32),
        compiler_params=_cparams(("arbitrary",)),
        name="na_bias",
    )(vl, vr)


AT_ROWS = 16


def _attn_block_variant(b, i):
    return (jnp.where(i == CTX_BLK, 3, jnp.where(i == 0, 0, jnp.where(i == CTX_BLK - 1, 2, 1))), 0, 0)


def _lane_half(shape, j):
    lane = lax.broadcasted_iota(jnp.int32, shape, 1)
    return (lane >= 64 * j) & (lane < 64 * (j + 1))


def _na_window(i):
    is_ctx = i == CTX_BLK
    r0 = i * 4
    kr0 = jnp.where(is_ctx, 0, jnp.clip(r0 - 4, 0, GRID_ROWS - NA_KROWS))
    delta = jnp.where(is_ctx, 3, kr0 - r0 + NA_WIN_R - 1) + NA_DR_PAD
    return pl.multiple_of(kr0 * GRID_W, 128), delta


def _na_scores(i, q_ref, k_ref, s_ref):
    kstart, _ = _na_window(i)
    for hp in range(2):
        cs = slice(hp * 128, (hp + 1) * 128)
        q2 = q_ref[:, cs]
        kw, kc = k_ref[cs, pl.ds(kstart, NA_KEYS)], k_ref[cs, SEQ:T_ALL]
        qm = jnp.concatenate([jnp.where(_lane_half((TB, 128), j), q2, jnp.zeros_like(q2)) for j in range(2)], axis=0)
        s_ref[hp, :, 0:CTX] = _dot(qm, kc)
        s_ref[hp, :, CTX:] = _dot(qm, kw)


def _na_softmax(i, s_ref, mask_ref, t2_ref, p_ref):
    _, delta = _na_window(i)
    for hp in range(2):
        for r in range(2 * TB // AT_ROWS):
            rows = slice(r * AT_ROWS, (r + 1) * AT_ROWS)
            j, rq = divmod(r, TB // AT_ROWS)
            a, sub = divmod(rq, GRID_W // AT_ROWS)
            trow = slice(sub * AT_ROWS, (sub + 1) * AT_ROWS)
            bias = jnp.concatenate(
                [t2_ref[2 * hp + j, delta + 2 * rp - a, trow, :] for rp in range(NA_KROWS // 2)], axis=1)
            s_nb = s_ref[hp, rows, CTX:] + bias + mask_ref[rq * AT_ROWS:(rq + 1) * AT_ROWS, :]
            s_c = s_ref[hp, rows, 0:CTX]
            m = jnp.maximum(jnp.max(s_nb, axis=-1, keepdims=True), jnp.max(s_c, axis=-1, keepdims=True))
            p_ref[hp, rows, 0:CTX] = jnp.exp((s_c - m).astype(BF16))
            p_ref[hp, rows, CTX:] = jnp.exp((s_nb - m).astype(BF16))


def _na_values(i, p_ref, v_ref, o_ref):
    kstart, _ = _na_window(i)
    lane = lax.broadcasted_iota(jnp.int32, (TB, 128), 1)
    for hp in range(2):
        cs = slice(hp * 128, (hp + 1) * 128)
        vw, vc = v_ref[pl.ds(kstart, NA_KEYS), cs], v_ref[SEQ:T_ALL, cs]
        o2 = (_dot(p_ref[hp, :, 0:CTX], jnp.concatenate([vc, jnp.ones_like(vc)], axis=1))
              + _dot(p_ref[hp, :, CTX:], jnp.concatenate([vw, jnp.ones_like(vw)], axis=1)))
        top = o2[0:TB, 0:128] * (1.0 / o2[0:TB, 128:256])
        bot = o2[TB:2 * TB, 0:128] * (1.0 / o2[TB:2 * TB, 128:256])
        o_ref[:, cs] = jnp.where(lane < 64, top, bot).astype(BF16)


def _swa_window(i):
    return pl.multiple_of(jnp.clip(i * TB - SWA_WINDOW, 0, SEQ - SWA_KEYS), SWA_WINDOW)


def _swa_scores(i, q_ref, k_ref, s_ref):
    kw, kc = k_ref[:, pl.ds(_swa_window(i), SWA_KEYS)], k_ref[:, SEQ:T_ALL]
    for g in range(2):
        q2 = q_ref[:, g * 128:(g + 1) * 128].astype(F32)
        qa = jnp.concatenate([jnp.where(_lane_half((TB, 128), g), q2 if j == g else pltpu.roll(q2, 64, 1), 0.0)
                              for j in range(2)], axis=0).astype(BF16)
        s_ref[g, :, 0:CTX] = _dot(qa, kc)
        s_ref[g, :, CTX:] = _dot(qa, kw)


def _swa_softmax(sink_ref, s_ref, mask_ref, p_ref, esink_ref):
    for g in range(2):
        for r in range(2 * TB // AT_ROWS):
            rows = slice(r * AT_ROWS, (r + 1) * AT_ROWS)
            j, rq = divmod(r, TB // AT_ROWS)
            sink = sink_ref[2 * g + j]
            s_l = s_ref[g, rows, CTX:] + mask_ref[rq * AT_ROWS:(rq + 1) * AT_ROWS, :]
            s_c = s_ref[g, rows, 0:CTX]
            m = jnp.maximum(jnp.maximum(jnp.max(s_l, axis=-1, keepdims=True),
                                        jnp.max(s_c, axis=-1, keepdims=True)), sink)
            p_ref[g, rows, 0:CTX] = jnp.exp((s_c - m).astype(BF16))
            p_ref[g, rows, CTX:] = jnp.exp((s_l - m).astype(BF16))
            esink_ref[g, rows, :] = jnp.exp(jnp.broadcast_to(sink - m, (AT_ROWS, 128)))


def _swa_values(i, p_ref, esink_ref, v_ref, o_ref):
    vw, vc = v_ref[pl.ds(_swa_window(i), SWA_KEYS), :], v_ref[SEQ:T_ALL, :]
    lane = lax.broadcasted_iota(jnp.int32, (TB, 128), 1)
    vc2 = jnp.concatenate([vc, jnp.ones_like(vc)], axis=1)
    vw2 = jnp.concatenate([vw, jnp.ones_like(vw)], axis=1)
    for g in range(2):
        o2 = _dot(p_ref[g, :, 0:CTX], vc2) + _dot(p_ref[g, :, CTX:], vw2)
        o = o2[:, 0:128] * (1.0 / (o2[:, 128:256] + esink_ref[g]))
        outs = [o[j * TB:(j + 1) * TB] if j == g else pltpu.roll(o[j * TB:(j + 1) * TB], 64, 1) for j in range(2)]
        o_ref[:, g * 128:(g + 1) * 128] = jnp.where(lane < 64, outs[0], outs[1]).astype(BF16)


def _attn_kernel(sink_ref, qn_ref, kn_ref, vn_ref, maskn_ref, t2_ref, qs_ref, ks_ref, vs_ref, masks_ref,
                 on_ref, os_ref, sn_ref, pn_ref, ss_ref, ps_ref, esink_ref):
    i = pl.program_id(1)
    _na_scores(i, qn_ref, kn_ref, sn_ref)
    _swa_scores(i, qs_ref, ks_ref, ss_ref)
    _na_softmax(i, sn_ref, maskn_ref, t2_ref, pn_ref)
    _swa_softmax(sink_ref, ss_ref, masks_ref, ps_ref, esink_ref)
    _na_values(i, pn_ref, vn_ref, on_ref)
    _swa_values(i, ps_ref, esink_ref, vs_ref, os_ref)


def _attn_call(u_att, kn_t, ks_t, sink, na_mask, t2, swa_mask, with_ctx):
    b = u_att.shape[0]
    nb = NBLK if with_ctx else CTX_BLK
    blk = lambda width, col: pl.BlockSpec((None, TB, width), lambda b, i: (b, i, col))
    full = lambda width, col: pl.BlockSpec((None, T_ALL, width), lambda b, i: (b, 0, col))
    keys = lambda width: pl.BlockSpec((None, width, T_ALL), lambda b, i: (b, 0, 0))
    out = pl.BlockSpec((None, TB, 256), lambda b, i: (b, i, 0))
    return pl.pallas_call(
        _attn_kernel,
        grid=(b, nb),
        in_specs=[
            pl.BlockSpec(memory_space=pltpu.SMEM),
            blk(256, 0), keys(256), full(256, 1),
            pl.BlockSpec((None, TB, NA_KEYS), _attn_block_variant),
            _const_spec((4, NA_DR_ROWS - 1, GRID_W, 128)),
            blk(256, 2), keys(128), full(128, 6),
            pl.BlockSpec((None, TB, SWA_KEYS), _attn_block_variant),
        ],
        out_specs=[out, out],
        out_shape=[jax.ShapeDtypeStruct((b, nb * TB, 256), BF16)] * 2,
        scratch_shapes=[pltpu.VMEM((2, 2 * TB, CTX + NA_KEYS), F32), pltpu.VMEM((2, 2 * TB, CTX + NA_KEYS), BF16),
                        pltpu.VMEM((2, 2 * TB, CTX + SWA_KEYS), F32), pltpu.VMEM((2, 2 * TB, CTX + SWA_KEYS), BF16),
                        pltpu.VMEM((2, 2 * TB, 128), F32)],
        compiler_params=_cparams(("arbitrary", "arbitrary")),
        name="attn",
    )(sink, u_att, kn_t, u_att, na_mask, t2, u_att, ks_t, u_att, swa_mask)


FN_XP = 136
FN_YP = 136
FN_ZP = 72


def _fnet_kernel(x_ref, f64_ref, g_ref, fc_ref, p_ref, q_ref, xz_ref, zq_ref, y_ref):
    sc_lat, sc_ctx = (64.0 * SEQ) ** -0.5, (64.0 * CTX) ** -0.5
    pq = _dot(fc_ref[...], x_ref[SEQ:T_ALL, :].astype(BF16))
    p_ref[SEQ:T_ALL, :] = (pq[0:CTX] * sc_ctx).astype(BF16)
    q_ref[SEQ:T_ALL, :] = (pq[CTX:2 * CTX] * sc_ctx).astype(BF16)
    f64 = f64_ref[...]

    def repitch(t1, carry):
        xz_ref[pl.ds(pl.multiple_of(t1 * FN_XP, 8), 128), :] = x_ref[pl.ds(pl.multiple_of(t1 * 128, 128), 128), :]
        return carry

    lax.fori_loop(0, 64, repitch, 0, unroll=4)

    def stage1(t2, carry):
        xs = xz_ref[pl.ds(t2, 64, stride=FN_XP), :].astype(BF16)
        y_ref[pl.ds(t2, 128, stride=FN_YP), :] = _dot(f64, xs)
        return carry

    lax.fori_loop(0, 128, stage1, 0, unroll=16)

    def stage2(k1, carry):
        yre = y_ref[pl.ds(pl.multiple_of(k1 * FN_YP, 8), 128), :]
        yim = y_ref[pl.ds(pl.multiple_of((64 + k1) * FN_YP, 8), 128), :]
        ycat = jnp.concatenate([yre, yim], axis=0).astype(BF16)
        pq = _dot(g_ref[k1], ycat)
        xz_ref[pl.ds(k1, 128, stride=FN_ZP), :] = pq[0:128] * sc_lat
        zq_ref[pl.ds(k1, 128, stride=FN_ZP), :] = pq[128:256] * sc_lat
        return carry

    lax.fori_loop(0, 64, stage2, 0, unroll=16)

    def emit(k2, carry):
        src = pl.ds(pl.multiple_of(k2 * FN_ZP, 8), 64)
        dst = pl.ds(pl.multiple_of(k2 * 64, 64), 64)
        p_ref[dst, :] = xz_ref[src, :].astype(BF16)
        q_ref[dst, :] = zq_ref[src, :].astype(BF16)
        return carry

    lax.fori_loop(0, 128, emit, 0, unroll=4)


def _fnet_call(u_f32, f64, g2, fc):
    b = u_f32.shape[0]
    spec = pl.BlockSpec((None, T_ALL, 128), lambda b, j: (b, 0, j))
    return pl.pallas_call(
        _fnet_kernel,
        grid=(b, 2),
        in_specs=[spec, _const_spec((128, 64)), _const_spec((64, 256, 256)), _const_spec((2 * CTX, CTX))],
        out_specs=[spec, spec],
        out_shape=[jax.ShapeDtypeStruct((b, T_ALL, 256), BF16)] * 2,
        scratch_shapes=[pltpu.VMEM((128 * FN_ZP, 128), F32), pltpu.VMEM((128 * FN_ZP, 128), F32),
                        pltpu.VMEM((128 * FN_YP, 128), F32)],
        compiler_params=_cparams(("arbitrary", "arbitrary")),
        name="fnet_dft",
    )(u_f32, f64, g2, fc)


def _hgrn_gates(qd, fz, gl, lb_zero):
    q = qd * jax.nn.sigmoid(qd)
    sp = jnp.log(1.0 + jnp.exp(-jnp.abs(fz)))
    ls = jnp.minimum(fz, 0.0) - sp
    lk = jnp.minimum(-fz, 0.0) - sp
    if lb_zero:
        return q, jnp.exp(lk), ls
    b_ = gl[1:2] + ls
    a_ = gl[0:1]
    logf = jnp.maximum(a_, b_) + jnp.log(1.0 + jnp.exp(-jnp.abs(a_ - b_)))
    return q, jnp.exp(gl[1:2] + lk), logf


def _dot01(b01, x):
    h1 = x.astype(BF16)
    r1 = x - h1.astype(F32)
    h2 = r1.astype(BF16)
    h3 = (r1 - h2.astype(F32)).astype(BF16)
    return _dot(b01, h1) + _dot(b01, h2) + _dot(b01, h3)


def _head_masks(rows):
    lane = lax.broadcasted_iota(jnp.int32, (rows, 256), 1)
    return [(lane >= 64 * h) & (lane < 64 * (h + 1)) for h in range(4)]


def _stack_heads(w, hms):
    return jnp.concatenate([jnp.where(hm, w, jnp.zeros_like(w)) for hm in hms], axis=0)


def _state_update(st_ref, st, e_last, vb, ks):
    ktv = _dot_tn(vb, ks)
    r2 = lax.broadcasted_iota(jnp.int32, (256, 256), 0) // 64
    c2 = lax.broadcasted_iota(jnp.int32, (256, 256), 1) // 64
    st_ref[...] = st * e_last + jnp.where(r2 == c2, ktv, 0.0)


def _hgrn_chunk_exact(qd, fz, v, st_ref, gl, ball, lmask_ref, d, rev, lb_zero):
    q, kk, logf = _hgrn_gates(qd, fz, gl, lb_zero)
    e = jnp.exp(_dot01(ball, logf))
    row = lax.broadcasted_iota(jnp.int32, (CHUNK, 256), 0)
    tau = (CHUNK - 1 - row) if rev else row
    hms = _head_masks(CHUNK)
    att = _dot_nt(_stack_heads(q.astype(BF16), hms), kk.astype(BF16)) * lmask_ref[d, 0]
    for li, sh in enumerate((5, 4, 3, 2, 1, 0)):
        up = ((tau >> sh) & 1) == 1
        w = (jnp.where(up, q, kk) * e[128 + 64 * li:192 + 64 * li]).astype(BF16)
        att = att + _dot_nt(_stack_heads(w, hms), w) * lmask_ref[d, 1 + li]
    vb = v.astype(BF16)
    st = st_ref[...]
    o = _dot_nt((q * e[0:64]).astype(BF16), st.astype(BF16))
    for h in range(4):
        o = o + _dot(att[64 * h:64 * (h + 1)].astype(BF16), jnp.where(hms[h], vb, jnp.zeros_like(vb)))
    tl = 0 if rev else CHUNK - 1
    _state_update(st_ref, st, e[tl:tl + 1], vb, (kk * e[64:128]).astype(BF16))
    return o


def _hgrn_chunk_prep(sl, q_ref, f_ref, gl, tri, rev, lb_zero):
    q, kk, logf = _hgrn_gates(q_ref[sl, :], f_ref[sl, :], gl, lb_zero)
    cum = _dot01(tri, logf)
    first_row, mid_row, last_row = (CHUNK - 1, CHUNK // 2, 0) if rev else (0, CHUNK // 2 - 1, CHUNK - 1)
    cum_mid, cum_last = cum[mid_row:mid_row + 1], cum[last_row:last_row + 1]
    a = cum - cum_mid
    a_range = jnp.maximum(jnp.max(a[first_row:first_row + 1]), jnp.max(-a[last_row:last_row + 1]))
    return q * jnp.exp(a), kk * jnp.exp(-a), cum_mid, cum_last, a_range


def _hgrn_chunk_fast(sl, qt, kt, cum_mid, cum_last, v_ref, st_ref, o_ref, rev):
    hms = _head_masks(CHUNK)
    r = lax.broadcasted_iota(jnp.int32, (4 * CHUNK, CHUNK), 0) & (CHUNK - 1)
    s = lax.broadcasted_iota(jnp.int32, (4 * CHUNK, CHUNK), 1)
    causal = (s >= r) if rev else (s <= r)
    vb = v_ref[sl, :].astype(BF16)
    att = jnp.where(causal, _dot_nt(_stack_heads(qt.astype(BF16), hms), kt.astype(BF16)), 0.0)
    st = st_ref[...]
    o = _dot_nt((qt * jnp.exp(cum_mid)).astype(BF16), st.astype(BF16))
    for h in range(4):
        o = o + _dot(att[64 * h:64 * (h + 1)].astype(BF16), jnp.where(hms[h], vb, jnp.zeros_like(vb)))
    o_ref[sl, :] = o
    _state_update(st_ref, st, jnp.exp(cum_last), vb, (kt * jnp.exp(cum_last - cum_mid)).astype(BF16))


def _hgrn_kernel(qf_ref, ff_ref, vf_ref, qb_ref, fb_ref, vb_ref, gl_ref, ball_ref, lmask_ref,
                 of_ref, ob_ref, sf_ref, sb_ref, sf0_ref, sb0_ref, *, lb_zero):
    @pl.when(pl.program_id(0) == 0)
    def _():
        sf_ref[...] = jnp.zeros_like(sf_ref)
        sb_ref[...] = jnp.zeros_like(sb_ref)

    sf0_ref[...] = sf_ref[...]
    sb0_ref[...] = sb_ref[...]
    chains = []
    for b in range(qf_ref.shape[0]):
        chains.append((qf_ref.at[b], ff_ref.at[b], vf_ref.at[b], sf_ref.at[b], of_ref.at[b], False, 0))
        chains.append((qb_ref.at[b], fb_ref.at[b], vb_ref.at[b], sb_ref.at[b], ob_ref.at[b], True, 1))
    rows = lambda rev, s: slice((NCHUNK - 1 - s if rev else s) * CHUNK, (NCHUNK - s if rev else s + 1) * CHUNK)

    def prep(s):
        return [_hgrn_chunk_prep(rows(rev, s), q_ref, f_ref, gl_ref[d], ball_ref[d, 0:CHUNK, :], rev, lb_zero)
                for q_ref, f_ref, v_ref, st_ref, o_ref, rev, d in chains]

    rng, ready = 0.0, prep(0)
    for s in range(NCHUNK):
        nxt = prep(s + 1) if s + 1 < NCHUNK else None
        for (q_ref, f_ref, v_ref, st_ref, o_ref, rev, d), (qt, kt, cum_mid, cum_last, amax) in zip(chains, ready):
            rng = jnp.maximum(rng, amax)
            _hgrn_chunk_fast(rows(rev, s), qt, kt, cum_mid, cum_last, v_ref, st_ref, o_ref, rev)
        ready = nxt

    @pl.when(jnp.logical_not(rng <= HGRN_FAST_RANGE))
    def _():
        sf_ref[...] = sf0_ref[...]
        sb_ref[...] = sb0_ref[...]

        def body(c, carry):
            for q_ref, f_ref, v_ref, st_ref, o_ref, rev, d in chains:
                r = pl.ds(pl.multiple_of((NCHUNK - 1 - c if rev else c) * CHUNK, CHUNK), CHUNK)
                o_ref[r, :] = _hgrn_chunk_exact(q_ref[r, :], f_ref[r, :], v_ref[r, :], st_ref,
                                                gl_ref[d], ball_ref[d], lmask_ref, d, rev, lb_zero)
            return carry

        lax.fori_loop(0, NCHUNK, body, 0)


def _hgrn_call(u_f32, gl, ball, lmask, lb_zero):
    b = u_f32.shape[0]

    def fwd(col):
        return pl.BlockSpec((b, TB, 256), lambda i: (0, jnp.where(i == 0, CTX_BLK, i - 1), col))

    def bwd(col):
        return pl.BlockSpec((b, TB, 256), lambda i: (0, jnp.where(i == 0, CTX_BLK, CTX_BLK - i), col))

    return pl.pallas_call(
        functools.partial(_hgrn_kernel, lb_zero=lb_zero),
        grid=(NBLK,),
        in_specs=[fwd(1), fwd(2), fwd(4), bwd(1), bwd(3), bwd(4),
                  _const_spec((2, 2, 256)), _const_spec((2, 8 * CHUNK, CHUNK)),
                  _const_spec((2, 7, 4 * CHUNK, CHUNK))],
        out_specs=[fwd(0), bwd(0)],
        out_shape=[jax.ShapeDtypeStruct((b, T_ALL, 256), F32)] * 2,
        scratch_shapes=[pltpu.VMEM((b, 256, 256), F32)] * 4,
        compiler_params=_cparams(("arbitrary",)),
        name="hgrn_scan",
    )(u_f32, u_f32, u_f32, u_f32, u_f32, u_f32, gl, ball, lmask)


def _outproj_kernel(*refs, nx, with_ctx):
    (oa_ref, p_ref, q_ref, oc_ref, of_ref, ob_ref, gd_ref, hg_ref, hh_ref, cs_ref,
     wf32_ref, wo32_ref, gp_ref, gtc_ref, gtb_ref, o_ref, wf_ref, wo_ref) = refs[nx:]

    @pl.when(_first_step())
    def _():
        wf_ref[...] = wf32_ref[...].astype(BF16)
        wo_ref[...] = wo32_ref[...].astype(BF16)

    for part, (r0, r1) in enumerate(_row_parts(o_ref.shape[0], with_ctx)):
        rows = slice(r0, r1)
        z = _dot(p_ref[rows, :], cs_ref[0:256, :]) + _dot(q_ref[rows, :], cs_ref[256:512, :])
        o_b = _dot(z.astype(BF16), wf_ref[...])
        od = of_ref[rows, :] + ob_ref[rows, :]
        sq = od * od
        hi = sq.astype(BF16)
        lo = (sq - hi.astype(F32)).astype(BF16)
        ms = (_dot(hi, hh_ref[...]) + _dot(lo, hh_ref[...])) * (1.0 / HEAD_DIM)
        gd = gd_ref[rows, :]
        o_d = od * lax.rsqrt(ms + EPS) * hg_ref[...] * (gd * jax.nn.sigmoid(gd))
        y = (_dot(oa_ref[rows, :], wo_ref[0:256, :]) + _dot(o_b.astype(BF16), wo_ref[256:512, :])
             + _dot(oc_ref[rows, :], wo_ref[512:768, :]) + _dot(o_d.astype(BF16), wo_ref[768:1024, :]))
        x = _x_part(refs[:nx], part, r0, r1)
        o_ref[rows, :] = _gated_residual(x, y, gp_ref, (gtc_ref, gtb_ref), part, with_ctx)


def _outproj_call(x_parts, o_a, p, q, o_c, o_f, o_bk, u_f32, hg, hh, cs, w_fnet, w_out, l, g_post, mods, with_ctx):
    b = x_parts[0].shape[0]
    tm, t = (TM_ALL, T_ALL) if with_ctx else (TM_LAT, SEQ)
    col = lambda c: pl.BlockSpec((None, tm, 256), lambda b, i: (b, i, c))
    split = len(x_parts) == 2
    x_args = (x_parts[0],) * 3 + (x_parts[1],) if split else x_parts
    x_specs = _x_specs(True) if split else [pl.BlockSpec((None, tm, D_MODEL), lambda b, i: (b, i, 0))]
    return pl.pallas_call(
        functools.partial(_outproj_kernel, nx=len(x_args), with_ctx=with_ctx),
        grid=(b, t // tm),
        in_specs=[
            *x_specs,
            col(0), col(0), col(0), col(0), col(0), col(0), col(5),
            _const_spec((1, 256)), _const_spec((256, 256)), _const_spec((512, 256)),
            _layer_spec((256, 256), l), _layer_spec((D_MODEL, D_MODEL), l), _const_spec((1, D_MODEL)),
            *_mod_specs(2),
        ],
        out_specs=pl.BlockSpec((None, tm, D_MODEL), lambda b, i: (b, i, 0)),
        out_shape=jax.ShapeDtypeStruct((b, t, D_MODEL), F32),
        scratch_shapes=[pltpu.VMEM((256, 256), BF16), pltpu.VMEM((D_MODEL, D_MODEL), BF16)],
        compiler_params=_cparams(("arbitrary", "arbitrary")),
        name="out_proj",
    )(*x_args, o_a, p, q, o_c, o_f, o_bk, u_f32, hg, hh, cs, w_fnet, w_out, g_post, mods, mods)


def _ffn_kernel(x_ref, g_ref, shc_ref, shb_ref, scc_ref, scb_ref, w1_ref, w2_32_ref, gp_ref, gtc_ref, gtb_ref,
                o_ref, w2_ref, *, with_ctx):
    fc = 256

    @pl.when(_first_step())
    def _():
        w2_ref[...] = w2_32_ref[...].astype(BF16)

    def swiglu(h):
        acc = jnp.zeros((h.shape[0], D_MODEL), F32)
        for c in range(D_FF // fc):
            a = _dot(h, w1_ref[:, c * fc:(c + 1) * fc])
            g = _dot(h, w1_ref[:, D_FF + c * fc:D_FF + (c + 1) * fc])
            act = (a * jax.nn.sigmoid(a) * g).astype(BF16)
            acc = acc + _dot(act, w2_ref[c * fc:(c + 1) * fc, :])
        return acc

    parts = list(enumerate(_row_parts(x_ref.shape[0], with_ctx)))
    pre = lambda part, r0, r1: _modulated(x_ref[r0:r1, :], g_ref, (shc_ref, shb_ref), (scc_ref, scb_ref), part, with_ctx)
    post = lambda part, r0, r1, acc: _gated_residual(x_ref[r0:r1, :], acc, gp_ref, (gtc_ref, gtb_ref), part, with_ctx)
    if with_ctx:
        acc = swiglu(jnp.concatenate([pre(part, r0, r1) for part, (r0, r1) in parts], axis=0))
        for part, (r0, r1) in parts:
            o_ref[r0:r1, :] = post(part, r0, r1, acc[r0:r1])
    else:
        for part, (r0, r1) in parts:
            o_ref[r0:r1, :] = post(part, r0, r1, swiglu(pre(part, r0, r1)))


def _ffn_call(xs, g_pre, w1, w2, l, g_post, mods, with_ctx):
    b, t, _ = xs.shape
    tm = TM_ALL if with_ctx else TM_LAT
    return pl.pallas_call(
        functools.partial(_ffn_kernel, with_ctx=with_ctx),
        grid=(b, t // tm),
        in_specs=[
            pl.BlockSpec((None, tm, D_MODEL), lambda b, i: (b, i, 0)),
            _const_spec((1, D_MODEL)),
            *_mod_specs(3), *_mod_specs(4),
            _layer_spec((D_MODEL, 2 * D_FF), l), _layer_spec((D_FF, D_MODEL), l), _const_spec((1, D_MODEL)),
            *_mod_specs(5),
        ],
        out_specs=pl.BlockSpec((None, tm, D_MODEL), lambda b, i: (b, i, 0)),
        out_shape=jax.ShapeDtypeStruct(xs.shape, F32),
        scratch_shapes=[pltpu.VMEM((D_FF, D_MODEL), BF16)],
        compiler_params=_cparams(("arbitrary", "arbitrary")),
        name="ffn",
    )(xs, g_pre, mods, mods, mods, mods, w1, w2, g_post, mods, mods)


def kernel(x, c, ctx, c_ctx, w_ada, b_ada, g_pre_mix, g_post_mix, g_pre_ffn, g_post_ffn, w_in, w_out,
           na_rpb, w_fnet, swa_sink, hgrn_lb_logits, hgrn_norm_g, w_ffn_in, w_ffn_out):
    assert x.shape == (2, SEQ, D_MODEL) and ctx.shape == (2, CTX, D_MODEL) and c.shape == (2, D_MODEL)
    assert w_in.shape == (DEPTH, D_MODEL, D_IN) and w_ffn_in.shape == (DEPTH, D_MODEL, 2 * D_FF)
    assert na_rpb.shape == (DEPTH, 4, 2 * NA_WIN_R - 1, 2 * NA_WIN_C - 1) and swa_sink.shape == (DEPTH, 4)
    tb = _tables()
    bf = lambda name: jnp.asarray(tb[name]).astype(BF16)
    cos, sin = jnp.asarray(tb["rope_cos"]), jnp.asarray(tb["rope_sin"])
    f64, g2, fc, chan_cs, hh = bf("f64"), bf("g_stage2"), bf("f_ctx"), bf("chan_cs"), bf("head_ones")
    ball, lmask = bf("hgrn_b"), jnp.asarray(tb["hgrn_mask"])
    na_mask, swa_mask = jnp.asarray(tb["na_mask"]), jnp.asarray(tb["swa_mask"])

    cc = jnp.concatenate([c, c_ctx[None, :]], axis=0)
    mods = _ada_call(jnp.broadcast_to(cc[:, :, None], (3, D_MODEL, 128)), w_ada, b_ada)

    lb = jnp.cumsum(jax.nn.softmax(hgrn_lb_logits.astype(F32), axis=1), axis=1)
    lb = lb - lb[:, :1]
    gl_all = jnp.stack([jnp.log(lb), jnp.log1p(-lb)], axis=2)

    w_ffn_in_b = w_ffn_in.astype(BF16)

    xs = None
    for l in range(DEPTH):
        with_ctx = l < DEPTH - 1
        row = lambda a: a[l][None, :]
        x_all = (x, ctx) if l == 0 else (xs,)
        x_res = x_all if with_ctx else x_all[:1]
        u_att, u_f32, kn_t, ks_t = _proj_call(x_all, row(g_pre_mix), mods[l], w_in, l, cos, sin)

        rpb = jnp.pad(na_rpb[l], ((0, 0), (NA_DR_PAD, NA_DR_ROWS - NA_DR_PAD - (2 * NA_WIN_R - 1)), (0, 0)))
        t2 = _na_bias_call(rpb[:, :, tb["na_lane_l"]], rpb[:, :, tb["na_lane_r"]])
        o_a, o_c = _attn_call(u_att, kn_t, ks_t, swa_sink[l], na_mask, t2, swa_mask, with_ctx)
        p, q = _fnet_call(u_f32, f64, g2, fc)
        o_f, o_bk = _hgrn_call(u_f32, gl_all[:, l], ball, lmask, lb_zero=(l == 0))

        xs = _outproj_call(x_res, o_a, p, q, o_c, o_f, o_bk, u_f32, row(hgrn_norm_g), hh, chan_cs,
                           w_fnet, w_out, l, row(g_post_mix), mods[l], with_ctx)
        xs = _ffn_call(xs, row(g_pre_ffn), w_ffn_in_b, w_ffn_out, l, row(g_post_ffn), mods[l], with_ctx)
    return xs
```

```python
import functools

import numpy as np
import jax
import jax.numpy as jnp
from jax import lax
from jax.experimental import pallas as pl
from jax.experimental.pallas import tpu as pltpu

F32, BF16 = jnp.float32, jnp.bfloat16

D_MODEL = 1024
SEQ = 8192
DEPTH = 2
GRID_W = 64
GRID_ROWS = SEQ // GRID_W
CTX = 256
HEAD_DIM = 64
NA_WIN_R, NA_WIN_C = 8, 16
SWA_WINDOW = 128
ROPE_THETA = 10000.0
EPS = 1e-6
D_FF = 2816
D_IN = 2816

TB = 256
T_ALL = SEQ + CTX
NBLK = T_ALL // TB
CTX_BLK = SEQ // TB
TM_ALL = 768
TM_LAT = 1024
NA_KROWS = 12
NA_KEYS = NA_KROWS * GRID_W
NA_DR_PAD = 4
NA_DR_ROWS = 24
SWA_KEYS = TB + 2 * SWA_WINDOW
CHUNK = 64
NCHUNK = TB // CHUNK
HGRN_FAST_RANGE = 60.0
Q_SCALE = HEAD_DIM ** -0.5
VMEM_LIMIT = 56 * 1024 * 1024

ATT_W, F32_W = 896, 1536


def _dot(a, b):
    return jnp.dot(a, b, preferred_element_type=F32)


def _dot_nt(a, b):
    return lax.dot_general(a, b, (((1,), (1,)), ((), ())), preferred_element_type=F32)


def _dot_tn(a, b):
    return lax.dot_general(a, b, (((0,), (0,)), ((), ())), preferred_element_type=F32)


def _cparams(sem):
    return pltpu.CompilerParams(dimension_semantics=sem, vmem_limit_bytes=VMEM_LIMIT)


def _const_spec(shape):
    nd = len(shape)
    return pl.BlockSpec(shape, lambda *_: (0,) * nd, pipeline_mode=pl.Buffered(1))


def _layer_spec(shape, l):
    nd = len(shape)
    return pl.BlockSpec((None,) + shape, lambda *_: (l,) + (0,) * nd, pipeline_mode=pl.Buffered(1))


@functools.lru_cache(maxsize=None)
def _tables():
    t = {}
    lane = np.arange(128)
    d = lane % 64
    freq = ROPE_THETA ** (-(d % 16) / 16.0)
    pos = np.arange(SEQ)
    p = np.where(d[None, :] < 32, (pos // GRID_W)[:, None], (pos % GRID_W)[:, None]).astype(np.float64)
    ang = p * freq[None, :]
    sign = np.where((d % 32) < 16, -1.0, 1.0)
    cos = np.concatenate([np.cos(ang), np.ones((CTX, 128))], axis=0)
    sin = np.concatenate([np.sin(ang) * sign[None, :], np.zeros((CTX, 128))], axis=0)
    t["rope_cos"], t["rope_sin"] = cos.astype(np.float32), sin.astype(np.float32)

    masks = []
    a, cq = np.arange(4), np.arange(GRID_W)
    rho, ck = np.arange(NA_KROWS), np.arange(GRID_W)
    for r0 in (0, 8, GRID_ROWS - 4):
        kr0 = int(np.clip(r0 - 4, 0, GRID_ROWS - NA_KROWS))
        rq, rk = r0 + a, kr0 + rho
        start = np.clip(rq - NA_WIN_R // 2, 0, GRID_ROWS - NA_WIN_R)
        vr = (rk[None, :] >= start[:, None]) & (rk[None, :] < start[:, None] + NA_WIN_R)
        cs = np.clip(cq - NA_WIN_C // 2, 0, GRID_W - NA_WIN_C)
        vc = (ck[None, :] >= cs[:, None]) & (ck[None, :] < cs[:, None] + NA_WIN_C)
        val = vr[:, None, :, None] & vc[None, :, None, :]
        masks.append(np.where(val, 0.0, -np.inf).reshape(TB, NA_KEYS))
    masks.append(np.full((TB, NA_KEYS), -np.inf))
    t["na_mask"] = np.stack(masks).astype(np.float32)
    n = np.arange(128)
    clipc = lambda m: np.clip(m, -(NA_WIN_C - 1), NA_WIN_C - 1) + NA_WIN_C - 1
    t["na_lane_l"] = clipc(np.where(n < 64, n, n - 128)).astype(np.int32)
    t["na_lane_r"] = clipc(n - 64).astype(np.int32)

    rel = np.arange(SWA_KEYS)[None, :] - np.arange(TB)[:, None]
    swa = [np.where(np.abs(rel + off) <= SWA_WINDOW, 0.0, -np.inf) for off in (0, -SWA_WINDOW, -2 * SWA_WINDOW)]
    swa.append(np.full((TB, SWA_KEYS), -np.inf))
    t["swa_mask"] = np.stack(swa).astype(np.float32)

    def cs_tab(num, den):
        ang = 2.0 * np.pi * (num % den) / den
        return np.cos(ang), np.sin(ang)

    k1, t1 = np.arange(64)[:, None], np.arange(64)[None, :]
    c64, s64 = cs_tab(k1 * t1, 64)
    t["f64"] = np.concatenate([c64, -s64], axis=0).astype(np.float32)
    k2, t2 = np.arange(128)[:, None], np.arange(128)[None, :]
    g = []
    for kk in range(64):
        gc, gs = cs_tab((64 * k2 + kk) * t2, SEQ)
        g.append(np.block([[gc, gs], [-gs, gc]]))
    t["g_stage2"] = np.stack(g).astype(np.float32)
    kc, tc = np.arange(CTX)[:, None], np.arange(CTX)[None, :]
    cc, sc = cs_tab(kc * tc, CTX)
    t["f_ctx"] = np.concatenate([cc, -sc], axis=0).astype(np.float32)
    ch = np.arange(256)
    same = (ch[:, None] // 64) == (ch[None, :] // 64)
    c_ch, s_ch = cs_tab((ch[:, None] % 64) * (ch[None, :] % 64), 64)
    t["chan_cs"] = np.concatenate([np.where(same, c_ch, 0.0), np.where(same, s_ch, 0.0)], axis=0).astype(np.float32)
    t["head_ones"] = same.astype(np.float32)

    balls, lmasks = [], []
    for rev in (False, True):
        tau = (CHUNK - 1 - np.arange(CHUNK)) if rev else np.arange(CHUNK)
        tt, tj = tau[:, None], tau[None, :]
        blocks = [(tj <= tt), (tj > tt)]
        masks = [tt == tj]
        for h in (32, 16, 8, 4, 2, 1):
            up = ((tt // h) % 2) == 1
            mid = (tt // (2 * h)) * (2 * h) + h - 1
            blocks.append(np.where(up, (tj > mid) & (tj <= tt), (tj > tt) & (tj <= mid)))
            up_s = ((tj // h) % 2) == 1
            masks.append((tt // (2 * h) == tj // (2 * h)) & up & ~up_s)
        balls.append(np.concatenate(blocks, axis=0))
        lmasks.append(np.stack([np.tile(m, (4, 1)) for m in masks]))
    t["hgrn_b"] = np.stack(balls).astype(np.float32)
    t["hgrn_mask"] = np.stack(lmasks).astype(np.float32)
    return t


def _ada_kernel(cb_ref, w_ref, b_ref, o_ref, s_ref):
    tl = 1536
    nrep = tl // 128
    kc = pl.program_id(1)
    tk = w_ref.shape[0]

    @pl.when((pl.program_id(0) == 0) & (kc == 0))
    def _():
        cb = cb_ref[...]
        s_ref[...] = cb * jax.nn.sigmoid(cb)

    @pl.when(kc == 0)
    def _():
        for r in range(3):
            o_ref[r] = b_ref[...]

    for lo in range(0, w_ref.shape[1], tl):
        def body(j, accs):
            k0 = pl.multiple_of(j * 8, 8)
            kg = pl.multiple_of(kc * tk + j * 8, 8)
            wt = w_ref[pl.ds(k0, 8), lo:lo + tl]
            return tuple(accs[r] + wt * jnp.concatenate([s_ref[r, pl.ds(kg, 8), :]] * nrep, axis=1) for r in range(3))

        accs = lax.fori_loop(0, tk // 8, body, tuple(jnp.zeros((8, tl), F32) for _ in range(3)), unroll=8)
        for r in range(3):
            o_ref[r, :, lo:lo + tl] += jnp.sum(accs[r], axis=0, keepdims=True)


def _ada_call(cb, w_ada, b_ada):
    tk, tn = 256, 6 * D_MODEL
    return pl.pallas_call(
        _ada_kernel,
        grid=(DEPTH, D_MODEL // tk),
        in_specs=[
            pl.BlockSpec((3, D_MODEL, 128), lambda l, j: (0, 0, 0)),
            pl.BlockSpec((None, tk, tn), lambda l, j: (l, j, 0)),
            pl.BlockSpec((None, 1, tn), lambda l, j: (l, 0, 0)),
        ],
        out_specs=pl.BlockSpec((None, 3, 1, tn), lambda l, j: (l, 0, 0, 0)),
        out_shape=jax.ShapeDtypeStruct((DEPTH, 3, 1, 6 * D_MODEL), F32),
        scratch_shapes=[pltpu.VMEM((3, D_MODEL, 128), F32)],
        compiler_params=_cparams(("arbitrary", "arbitrary")),
        name="adaln",
    )(cb, w_ada, b_ada.reshape(DEPTH, 1, 6 * D_MODEL))


def _mod_specs(col):
    return [pl.BlockSpec((None, 1, D_MODEL), lambda b, i: (2, 0, col)),
            pl.BlockSpec((None, 1, D_MODEL), lambda b, i: (b, 0, col))]


def _row_parts(tm, with_ctx):
    return [(0, tm - CTX), (tm - CTX, tm)] if with_ctx else [(0, tm // 2), (tm // 2, tm)]


def _mod(mc_ref, mb_ref, part, with_ctx):
    if not with_ctx or part == 0:
        return mb_ref[...]
    last = pl.program_id(1) == pl.num_programs(1) - 1
    return jnp.where(last, mc_ref[...], mb_ref[...])


def _modulated(x, g_ref, sh_refs, sc_refs, part, with_ctx):
    gs = g_ref[...] * (1.0 + _mod(*sc_refs, part, with_ctx))
    return (_rms(x) * gs + _mod(*sh_refs, part, with_ctx)).astype(BF16)


def _gated_residual(x, y, gp_ref, gt_refs, part, with_ctx):
    return x + (_mod(*gt_refs, part, with_ctx) * gp_ref[...]) * _rms(y)


def _rms(x):
    return x * lax.rsqrt(jnp.mean(x * x, axis=-1, keepdims=True) + EPS)


def _x_specs(split):
    if not split:
        return [pl.BlockSpec((None, TM_ALL, D_MODEL), lambda b, i: (b, i, 0))]
    nx = SEQ // TB
    return [pl.BlockSpec((None, TB, D_MODEL), lambda b, i, j=j: (b, jnp.minimum(3 * i + j, nx - 1), 0))
            for j in range(TM_ALL // TB)] + [pl.BlockSpec((None, CTX, D_MODEL), lambda b, i: (b, 0, 0))]


def _x_part(x_refs, part, lo, hi):
    if len(x_refs) == 1:
        return x_refs[0][lo:hi, :]
    xa, xb, xc, cx = x_refs
    if part == 0:
        return jnp.concatenate([xa[...], xb[...]], axis=0)
    last = pl.program_id(1) == pl.num_programs(1) - 1
    return jnp.where(last, cx[...], xc[...])


def _first_step():
    return (pl.program_id(0) == 0) & (pl.program_id(1) == 0)


def _proj_kernel(*refs, nx):
    (g_ref, shc_ref, shb_ref, scc_ref, scb_ref, w32_ref, cos_ref, sin_ref,
     att_ref, f32_ref, knt_ref, kst_ref, w_ref) = refs[nx:]

    @pl.when(_first_step())
    def _():
        w_ref[...] = w32_ref[...].astype(BF16)

    for part, (lo, hi) in enumerate(_row_parts(TM_ALL, True)):
        rows = slice(lo, hi)
        h = _modulated(_x_part(refs[:nx], part, lo, hi), g_ref, (shc_ref, shb_ref), (scc_ref, scb_ref), part, True)
        cos, sin = cos_ref[rows, :], sin_ref[rows, :]
        first = (lax.broadcasted_iota(jnp.int32, (hi - lo, 128), 1) % 32) < 16

        def rope(v):
            return v * cos + jnp.where(first, pltpu.roll(v, 112, 1), pltpu.roll(v, 16, 1)) * sin

        a = _dot(h, w_ref[:, 0:768])
        att_ref[rows, 0:256] = (a[:, 0:256] * Q_SCALE).astype(BF16)
        knt_ref[:, rows] = a[:, 256:512].T.astype(BF16)
        att_ref[rows, 256:512] = a[:, 512:768].astype(BF16)
        f32_ref[rows, 0:256] = _dot(h, w_ref[:, 768:1024])
        s = _dot(h, w_ref[:, 1024:1536])
        att_ref[rows, 512:640] = (rope(s[:, 0:128]) * Q_SCALE).astype(BF16)
        att_ref[rows, 640:768] = (rope(s[:, 128:256]) * Q_SCALE).astype(BF16)
        kst_ref[:, rows] = rope(s[:, 256:384]).T.astype(BF16)
        att_ref[rows, 768:896] = s[:, 384:512].astype(BF16)
        f32_ref[rows, 256:1536] = _dot(h, w_ref[:, 1536:2816])


def _proj_call(x_parts, g, mods, w_in, l, cos, sin):
    b = x_parts[0].shape[0]
    tm = TM_ALL
    split = len(x_parts) == 2
    x_args = (x_parts[0],) * 3 + (x_parts[1],) if split else x_parts
    return pl.pallas_call(
        functools.partial(_proj_kernel, nx=len(x_args)),
        grid=(b, T_ALL // tm),
        in_specs=[
            *_x_specs(split),
            _const_spec((1, D_MODEL)),
            *_mod_specs(0), *_mod_specs(1),
            _layer_spec((D_MODEL, D_IN), l),
            pl.BlockSpec((tm, 128), lambda b, i: (i, 0)),
            pl.BlockSpec((tm, 128), lambda b, i: (i, 0)),
        ],
        out_specs=[
            pl.BlockSpec((None, tm, ATT_W), lambda b, i: (b, i, 0)),
            pl.BlockSpec((None, tm, F32_W), lambda b, i: (b, i, 0)),
            pl.BlockSpec((None, 256, tm), lambda b, i: (b, 0, i)),
            pl.BlockSpec((None, 128, tm), lambda b, i: (b, 0, i)),
        ],
        out_shape=[jax.ShapeDtypeStruct((b, T_ALL, ATT_W), BF16), jax.ShapeDtypeStruct((b, T_ALL, F32_W), F32),
                   jax.ShapeDtypeStruct((b, 256, T_ALL), BF16), jax.ShapeDtypeStruct((b, 128, T_ALL), BF16)],
        scratch_shapes=[pltpu.VMEM((D_MODEL, D_IN), BF16)],
        compiler_params=_cparams(("arbitrary", "arbitrary")),
        name="proj_in",
    )(*x_args, g, mods, mods, mods, mods, w_in, cos, sin)


def _na_bias_kernel(vl_ref, vr_ref, o_ref):
    lane = lax.broadcasted_iota(jnp.int32, (GRID_W, 128), 1)
    for e in range(NA_DR_ROWS - 1):
        left = pltpu.roll(jnp.broadcast_to(vl_ref[e:e + 1, :], (GRID_W, 128)), 0, 1, stride=1, stride_axis=0)
        right = pltpu.roll(jnp.broadcast_to(vr_ref[e + 1:e + 2, :], (GRID_W, 128)), 0, 1, stride=1, stride_axis=0)
        o_ref[e] = jnp.where(lane < 64, left, right)


def _na_bias_call(vl, vr):
    spec = pl.BlockSpec((None, NA_DR_ROWS, 128), lambda h: (h, 0, 0))
    return pl.pallas_call(
        _na_bias_kernel,
        grid=(4,),
        in_specs=[spec, spec],
        out_specs=pl.BlockSpec((None, NA_DR_ROWS - 1, GRID_W, 128), lambda h: (h, 0, 0, 0)),
        out_shape=jax.ShapeDtypeStruct((4, NA_DR_ROWS - 1, GRID_W, 128), F32),
        compiler_params=_cparams(("arbitrary",)),
        name="na_bias",
    )(vl, vr)


AT_ROWS = 16


def _attn_block_variant(b, i):
    return (jnp.where(i == CTX_BLK, 3, jnp.where(i == 0, 0, jnp.where(i == CTX_BLK - 1, 2, 1))), 0, 0)


def _lane_half(shape, j):
    lane = lax.broadcasted_iota(jnp.int32, shape, 1)
    return (lane >= 64 * j) & (lane < 64 * (j + 1))


def _na_window(i):
    is_ctx = i == CTX_BLK
    r0 = i * 4
    kr0 = jnp.where(is_ctx, 0, jnp.clip(r0 - 4, 0, GRID_ROWS - NA_KROWS))
    delta = jnp.where(is_ctx, 3, kr0 - r0 + NA_WIN_R - 1) + NA_DR_PAD
    return pl.multiple_of(kr0 * GRID_W, 128), delta


def _na_scores(i, q_ref, k_ref, s_ref):
    kstart, _ = _na_window(i)
    for hp in range(2):
        cs = slice(hp * 128, (hp + 1) * 128)
        q2 = q_ref[:, cs]
        kw, kc = k_ref[cs, pl.ds(kstart, NA_KEYS)], k_ref[cs, SEQ:T_ALL]
        qm = jnp.concatenate([jnp.where(_lane_half((TB, 128), j), q2, jnp.zeros_like(q2)) for j in range(2)], axis=0)
        s_ref[hp, :, 0:CTX] = _dot(qm, kc)
        s_ref[hp, :, CTX:] = _dot(qm, kw)


def _na_softmax(i, s_ref, mask_ref, t2_ref, p_ref):
    _, delta = _na_window(i)
    for hp in range(2):
        for r in range(2 * TB // AT_ROWS):
            rows = slice(r * AT_ROWS, (r + 1) * AT_ROWS)
            j, rq = divmod(r, TB // AT_ROWS)
            a, sub = divmod(rq, GRID_W // AT_ROWS)
            trow = slice(sub * AT_ROWS, (sub + 1) * AT_ROWS)
            bias = jnp.concatenate(
                [t2_ref[2 * hp + j, delta + 2 * rp - a, trow, :] for rp in range(NA_KROWS // 2)], axis=1)
            s_nb = s_ref[hp, rows, CTX:] + bias + mask_ref[rq * AT_ROWS:(rq + 1) * AT_ROWS, :]
            s_c = s_ref[hp, rows, 0:CTX]
            m = jnp.maximum(jnp.max(s_nb, axis=-1, keepdims=True), jnp.max(s_c, axis=-1, keepdims=True))
            p_ref[hp, rows, 0:CTX] = jnp.exp((s_c - m).astype(BF16))
            p_ref[hp, rows, CTX:] = jnp.exp((s_nb - m).astype(BF16))


def _na_values(i, p_ref, v_ref, o_ref):
    kstart, _ = _na_window(i)
    lane = lax.broadcasted_iota(jnp.int32, (TB, 128), 1)
    for hp in range(2):
        cs = slice(hp * 128, (hp + 1) * 128)
        vw, vc = v_ref[pl.ds(kstart, NA_KEYS), cs], v_ref[SEQ:T_ALL, cs]
        o2 = (_dot(p_ref[hp, :, 0:CTX], jnp.concatenate([vc, jnp.ones_like(vc)], axis=1))
              + _dot(p_ref[hp, :, CTX:], jnp.concatenate([vw, jnp.ones_like(vw)], axis=1)))
        top = o2[0:TB, 0:128] * (1.0 / o2[0:TB, 128:256])
        bot = o2[TB:2 * TB, 0:128] * (1.0 / o2[TB:2 * TB, 128:256])
        o_ref[:, cs] = jnp.where(lane < 64, top, bot).astype(BF16)


def _swa_window(i):
    return pl.multiple_of(jnp.clip(i * TB - SWA_WINDOW, 0, SEQ - SWA_KEYS), SWA_WINDOW)


def _swa_scores(i, q_ref, k_ref, s_ref):
    kw, kc = k_ref[:, pl.ds(_swa_window(i), SWA_KEYS)], k_ref[:, SEQ:T_ALL]
    for g in range(2):
        q2 = q_ref[:, g * 128:(g + 1) * 128].astype(F32)
        qa = jnp.concatenate([jnp.where(_lane_half((TB, 128), g), q2 if j == g else pltpu.roll(q2, 64, 1), 0.0)
                              for j in range(2)], axis=0).astype(BF16)
        s_ref[g, :, 0:CTX] = _dot(qa, kc)
        s_ref[g, :, CTX:] = _dot(qa, kw)


def _swa_softmax(sink_ref, s_ref, mask_ref, p_ref, esink_ref):
    for g in range(2):
        for r in range(2 * TB // AT_ROWS):
            rows = slice(r * AT_ROWS, (r + 1) * AT_ROWS)
            j, rq = divmod(r, TB // AT_ROWS)
            sink = sink_ref[2 * g + j]
            s_l = s_ref[g, rows, CTX:] + mask_ref[rq * AT_ROWS:(rq + 1) * AT_ROWS, :]
            s_c = s_ref[g, rows, 0:CTX]
            m = jnp.maximum(jnp.maximum(jnp.max(s_l, axis=-1, keepdims=True),
                                        jnp.max(s_c, axis=-1, keepdims=True)), sink)
            p_ref[g, rows, 0:CTX] = jnp.exp((s_c - m).astype(BF16))
            p_ref[g, rows, CTX:] = jnp.exp((s_l - m).astype(BF16))
            esink_ref[g, rows, :] = jnp.exp(jnp.broadcast_to(sink - m, (AT_ROWS, 128)))


def _swa_values(i, p_ref, esink_ref, v_ref, o_ref):
    vw, vc = v_ref[pl.ds(_swa_window(i), SWA_KEYS), :], v_ref[SEQ:T_ALL, :]
    lane = lax.broadcasted_iota(jnp.int32, (TB, 128), 1)
    vc2 = jnp.concatenate([vc, jnp.ones_like(vc)], axis=1)
    vw2 = jnp.concatenate([vw, jnp.ones_like(vw)], axis=1)
    for g in range(2):
        o2 = _dot(p_ref[g, :, 0:CTX], vc2) + _dot(p_ref[g, :, CTX:], vw2)
        o = o2[:, 0:128] * (1.0 / (o2[:, 128:256] + esink_ref[g]))
        outs = [o[j * TB:(j + 1) * TB] if j == g else pltpu.roll(o[j * TB:(j + 1) * TB], 64, 1) for j in range(2)]
        o_ref[:, g * 128:(g + 1) * 128] = jnp.where(lane < 64, outs[0], outs[1]).astype(BF16)


def _attn_kernel(sink_ref, qn_ref, kn_ref, vn_ref, maskn_ref, t2_ref, qs_ref, ks_ref, vs_ref, masks_ref,
                 on_ref, os_ref, sn_ref, pn_ref, ss_ref, ps_ref, esink_ref):
    i = pl.program_id(1)
    _na_scores(i, qn_ref, kn_ref, sn_ref)
    _swa_scores(i, qs_ref, ks_ref, ss_ref)
    _na_softmax(i, sn_ref, maskn_ref, t2_ref, pn_ref)
    _swa_softmax(sink_ref, ss_ref, masks_ref, ps_ref, esink_ref)
    _na_values(i, pn_ref, vn_ref, on_ref)
    _swa_values(i, ps_ref, esink_ref, vs_ref, os_ref)


def _attn_call(u_att, kn_t, ks_t, sink, na_mask, t2, swa_mask, with_ctx):
    b = u_att.shape[0]
    nb = NBLK if with_ctx else CTX_BLK
    blk = lambda width, col: pl.BlockSpec((None, TB, width), lambda b, i: (b, i, col))
    full = lambda width, col: pl.BlockSpec((None, T_ALL, width), lambda b, i: (b, 0, col))
    keys = lambda width: pl.BlockSpec((None, width, T_ALL), lambda b, i: (b, 0, 0))
    out = pl.BlockSpec((None, TB, 256), lambda b, i: (b, i, 0))
    return pl.pallas_call(
        _attn_kernel,
        grid=(b, nb),
        in_specs=[
            pl.BlockSpec(memory_space=pltpu.SMEM),
            blk(256, 0), keys(256), full(256, 1),
            pl.BlockSpec((None, TB, NA_KEYS), _attn_block_variant),
            _const_spec((4, NA_DR_ROWS - 1, GRID_W, 128)),
            blk(256, 2), keys(128), full(128, 6),
            pl.BlockSpec((None, TB, SWA_KEYS), _attn_block_variant),
        ],
        out_specs=[out, out],
        out_shape=[jax.ShapeDtypeStruct((b, nb * TB, 256), BF16)] * 2,
        scratch_shapes=[pltpu.VMEM((2, 2 * TB, CTX + NA_KEYS), F32), pltpu.VMEM((2, 2 * TB, CTX + NA_KEYS), BF16),
                        pltpu.VMEM((2, 2 * TB, CTX + SWA_KEYS), F32), pltpu.VMEM((2, 2 * TB, CTX + SWA_KEYS), BF16),
                        pltpu.VMEM((2, 2 * TB, 128), F32)],
        compiler_params=_cparams(("arbitrary", "arbitrary")),
        name="attn",
    )(sink, u_att, kn_t, u_att, na_mask, t2, u_att, ks_t, u_att, swa_mask)


FN_XP = 136
FN_YP = 136
FN_ZP = 72


def _fnet_kernel(x_ref, f64_ref, g_ref, fc_ref, p_ref, q_ref, xz_ref, zq_ref, y_ref):
    sc_lat, sc_ctx = (64.0 * SEQ) ** -0.5, (64.0 * CTX) ** -0.5
    pq = _dot(fc_ref[...], x_ref[SEQ:T_ALL, :].astype(BF16))
    p_ref[SEQ:T_ALL, :] = (pq[0:CTX] * sc_ctx).astype(BF16)
    q_ref[SEQ:T_ALL, :] = (pq[CTX:2 * CTX] * sc_ctx).astype(BF16)
    f64 = f64_ref[...]

    def repitch(t1, carry):
        xz_ref[pl.ds(pl.multiple_of(t1 * FN_XP, 8), 128), :] = x_ref[pl.ds(pl.multiple_of(t1 * 128, 128), 128), :]
        return carry

    lax.fori_loop(0, 64, repitch, 0, unroll=4)

    def stage1(t2, carry):
        xs = xz_ref[pl.ds(t2, 64, stride=FN_XP), :].astype(BF16)
        y_ref[pl.ds(t2, 128, stride=FN_YP), :] = _dot(f64, xs)
        return carry

    lax.fori_loop(0, 128, stage1, 0, unroll=16)

    def stage2(k1, carry):
        yre = y_ref[pl.ds(pl.multiple_of(k1 * FN_YP, 8), 128), :]
        yim = y_ref[pl.ds(pl.multiple_of((64 + k1) * FN_YP, 8), 128), :]
        ycat = jnp.concatenate([yre, yim], axis=0).astype(BF16)
        pq = _dot(g_ref[k1], ycat)
        xz_ref[pl.ds(k1, 128, stride=FN_ZP), :] = pq[0:128] * sc_lat
        zq_ref[pl.ds(k1, 128, stride=FN_ZP), :] = pq[128:256] * sc_lat
        return carry

    lax.fori_loop(0, 64, stage2, 0, unroll=16)

    def emit(k2, carry):
        src = pl.ds(pl.multiple_of(k2 * FN_ZP, 8), 64)
        dst = pl.ds(pl.multiple_of(k2 * 64, 64), 64)
        p_ref[dst, :] = xz_ref[src, :].astype(BF16)
        q_ref[dst, :] = zq_ref[src, :].astype(BF16)
        return carry

    lax.fori_loop(0, 128, emit, 0, unroll=4)


def _fnet_call(u_f32, f64, g2, fc):
    b = u_f32.shape[0]
    spec = pl.BlockSpec((None, T_ALL, 128), lambda b, j: (b, 0, j))
    return pl.pallas_call(
        _fnet_kernel,
        grid=(b, 2),
        in_specs=[spec, _const_spec((128, 64)), _const_spec((64, 256, 256)), _const_spec((2 * CTX, CTX))],
        out_specs=[spec, spec],
        out_shape=[jax.ShapeDtypeStruct((b, T_ALL, 256), BF16)] * 2,
        scratch_shapes=[pltpu.VMEM((128 * FN_ZP, 128), F32), pltpu.VMEM((128 * FN_ZP, 128), F32),
                        pltpu.VMEM((128 * FN_YP, 128), F32)],
        compiler_params=_cparams(("arbitrary", "arbitrary")),
        name="fnet_dft",
    )(u_f32, f64, g2, fc)


def _hgrn_gates(qd, fz, gl, lb_zero):
    q = qd * jax.nn.sigmoid(qd)
    sp = jnp.log(1.0 + jnp.exp(-jnp.abs(fz)))
    ls = jnp.minimum(fz, 0.0) - sp
    lk = jnp.minimum(-fz, 0.0) - sp
    if lb_zero:
        return q, jnp.exp(lk), ls
    b_ = gl[1:2] + ls
    a_ = gl[0:1]
    logf = jnp.maximum(a_, b_) + jnp.log(1.0 + jnp.exp(-jnp.abs(a_ - b_)))
    return q, jnp.exp(gl[1:2] + lk), logf


def _dot01(b01, x):
    h1 = x.astype(BF16)
    r1 = x - h1.astype(F32)
    h2 = r1.astype(BF16)
    h3 = (r1 - h2.astype(F32)).astype(BF16)
    return _dot(b01, h1) + _dot(b01, h2) + _dot(b01, h3)


def _head_masks(rows):
    lane = lax.broadcasted_iota(jnp.int32, (rows, 256), 1)
    return [(lane >= 64 * h) & (lane < 64 * (h + 1)) for h in range(4)]


def _stack_heads(w, hms):
    return jnp.concatenate([jnp.where(hm, w, jnp.zeros_like(w)) for hm in hms], axis=0)


def _state_update(st_ref, st, e_last, vb, ks):
    ktv = _dot_tn(vb, ks)
    r2 = lax.broadcasted_iota(jnp.int32, (256, 256), 0) // 64
    c2 = lax.broadcasted_iota(jnp.int32, (256, 256), 1) // 64
    st_ref[...] = st * e_last + jnp.where(r2 == c2, ktv, 0.0)


def _hgrn_chunk_exact(qd, fz, v, st_ref, gl, ball, lmask_ref, d, rev, lb_zero):
    q, kk, logf = _hgrn_gates(qd, fz, gl, lb_zero)
    e = jnp.exp(_dot01(ball, logf))
    row = lax.broadcasted_iota(jnp.int32, (CHUNK, 256), 0)
    tau = (CHUNK - 1 - row) if rev else row
    hms = _head_masks(CHUNK)
    att = _dot_nt(_stack_heads(q.astype(BF16), hms), kk.astype(BF16)) * lmask_ref[d, 0]
    for li, sh in enumerate((5, 4, 3, 2, 1, 0)):
        up = ((tau >> sh) & 1) == 1
        w = (jnp.where(up, q, kk) * e[128 + 64 * li:192 + 64 * li]).astype(BF16)
        att = att + _dot_nt(_stack_heads(w, hms), w) * lmask_ref[d, 1 + li]
    vb = v.astype(BF16)
    st = st_ref[...]
    o = _dot_nt((q * e[0:64]).astype(BF16), st.astype(BF16))
    for h in range(4):
        o = o + _dot(att[64 * h:64 * (h + 1)].astype(BF16), jnp.where(hms[h], vb, jnp.zeros_like(vb)))
    tl = 0 if rev else CHUNK - 1
    _state_update(st_ref, st, e[tl:tl + 1], vb, (kk * e[64:128]).astype(BF16))
    return o


def _hgrn_chunk_prep(sl, q_ref, f_ref, gl, tri, rev, lb_zero):
    q, kk, logf = _hgrn_gates(q_ref[sl, :], f_ref[sl, :], gl, lb_zero)
    cum = _dot01(tri, logf)
    first_row, mid_row, last_row = (CHUNK - 1, CHUNK // 2, 0) if rev else (0, CHUNK // 2 - 1, CHUNK - 1)
    cum_mid, cum_last = cum[mid_row:mid_row + 1], cum[last_row:last_row + 1]
    a = cum - cum_mid
    a_range = jnp.maximum(jnp.max(a[first_row:first_row + 1]), jnp.max(-a[last_row:last_row + 1]))
    return q * jnp.exp(a), kk * jnp.exp(-a), cum_mid, cum_last, a_range


def _hgrn_chunk_fast(sl, qt, kt, cum_mid, cum_last, v_ref, st_ref, o_ref, rev):
    hms = _head_masks(CHUNK)
    r = lax.broadcasted_iota(jnp.int32, (4 * CHUNK, CHUNK), 0) & (CHUNK - 1)
    s = lax.broadcasted_iota(jnp.int32, (4 * CHUNK, CHUNK), 1)
    causal = (s >= r) if rev else (s <= r)
    vb = v_ref[sl, :].astype(BF16)
    att = jnp.where(causal, _dot_nt(_stack_heads(qt.astype(BF16), hms), kt.astype(BF16)), 0.0)
    st = st_ref[...]
    o = _dot_nt((qt * jnp.exp(cum_mid)).astype(BF16), st.astype(BF16))
    for h in range(4):
        o = o + _dot(att[64 * h:64 * (h + 1)].astype(BF16), jnp.where(hms[h], vb, jnp.zeros_like(vb)))
    o_ref[sl, :] = o
    _state_update(st_ref, st, jnp.exp(cum_last), vb, (kt * jnp.exp(cum_last - cum_mid)).astype(BF16))


def _hgrn_kernel(qf_ref, ff_ref, vf_ref, qb_ref, fb_ref, vb_ref, gl_ref, ball_ref, lmask_ref,
                 of_ref, ob_ref, sf_ref, sb_ref, sf0_ref, sb0_ref, *, lb_zero):
    @pl.when(pl.program_id(0) == 0)
    def _():
        sf_ref[...] = jnp.zeros_like(sf_ref)
        sb_ref[...] = jnp.zeros_like(sb_ref)

    sf0_ref[...] = sf_ref[...]
    sb0_ref[...] = sb_ref[...]
    chains = []
    for b in range(qf_ref.shape[0]):
        chains.append((qf_ref.at[b], ff_ref.at[b], vf_ref.at[b], sf_ref.at[b], of_ref.at[b], False, 0))
        chains.append((qb_ref.at[b], fb_ref.at[b], vb_ref.at[b], sb_ref.at[b], ob_ref.at[b], True, 1))
    rows = lambda rev, s: slice((NCHUNK - 1 - s if rev else s) * CHUNK, (NCHUNK - s if rev else s + 1) * CHUNK)

    def prep(s):
        return [_hgrn_chunk_prep(rows(rev, s), q_ref, f_ref, gl_ref[d], ball_ref[d, 0:CHUNK, :], rev, lb_zero)
                for q_ref, f_ref, v_ref, st_ref, o_ref, rev, d in chains]

    rng, ready = 0.0, prep(0)
    for s in range(NCHUNK):
        nxt = prep(s + 1) if s + 1 < NCHUNK else None
        for (q_ref, f_ref, v_ref, st_ref, o_ref, rev, d), (qt, kt, cum_mid, cum_last, amax) in zip(chains, ready):
            rng = jnp.maximum(rng, amax)
            _hgrn_chunk_fast(rows(rev, s), qt, kt, cum_mid, cum_last, v_ref, st_ref, o_ref, rev)
        ready = nxt

    @pl.when(jnp.logical_not(rng <= HGRN_FAST_RANGE))
    def _():
        sf_ref[...] = sf0_ref[...]
        sb_ref[...] = sb0_ref[...]

        def body(c, carry):
            for q_ref, f_ref, v_ref, st_ref, o_ref, rev, d in chains:
                r = pl.ds(pl.multiple_of((NCHUNK - 1 - c if rev else c) * CHUNK, CHUNK), CHUNK)
                o_ref[r, :] = _hgrn_chunk_exact(q_ref[r, :], f_ref[r, :], v_ref[r, :], st_ref,
                                                gl_ref[d], ball_ref[d], lmask_ref, d, rev, lb_zero)
            return carry

        lax.fori_loop(0, NCHUNK, body, 0)


def _hgrn_call(u_f32, gl, ball, lmask, lb_zero):
    b = u_f32.shape[0]

    def fwd(col):
        return pl.BlockSpec((b, TB, 256), lambda i: (0, jnp.where(i == 0, CTX_BLK, i - 1), col))

    def bwd(col):
        return pl.BlockSpec((b, TB, 256), lambda i: (0, jnp.where(i == 0, CTX_BLK, CTX_BLK - i), col))

    return pl.pallas_call(
        functools.partial(_hgrn_kernel, lb_zero=lb_zero),
        grid=(NBLK,),
        in_specs=[fwd(1), fwd(2), fwd(4), bwd(1), bwd(3), bwd(4),
                  _const_spec((2, 2, 256)), _const_spec((2, 8 * CHUNK, CHUNK)),
                  _const_spec((2, 7, 4 * CHUNK, CHUNK))],
        out_specs=[fwd(0), bwd(0)],
        out_shape=[jax.ShapeDtypeStruct((b, T_ALL, 256), F32)] * 2,
        scratch_shapes=[pltpu.VMEM((b, 256, 256), F32)] * 4,
        compiler_params=_cparams(("arbitrary",)),
        name="hgrn_scan",
    )(u_f32, u_f32, u_f32, u_f32, u_f32, u_f32, gl, ball, lmask)


def _outproj_kernel(*refs, nx, with_ctx):
    (oa_ref, p_ref, q_ref, oc_ref, of_ref, ob_ref, gd_ref, hg_ref, hh_ref, cs_ref,
     wf32_ref, wo32_ref, gp_ref, gtc_ref, gtb_ref, o_ref, wf_ref, wo_ref) = refs[nx:]

    @pl.when(_first_step())
    def _():
        wf_ref[...] = wf32_ref[...].astype(BF16)
        wo_ref[...] = wo32_ref[...].astype(BF16)

    for part, (r0, r1) in enumerate(_row_parts(o_ref.shape[0], with_ctx)):
        rows = slice(r0, r1)
        z = _dot(p_ref[rows, :], cs_ref[0:256, :]) + _dot(q_ref[rows, :], cs_ref[256:512, :])
        o_b = _dot(z.astype(BF16), wf_ref[...])
        od = of_ref[rows, :] + ob_ref[rows, :]
        sq = od * od
        hi = sq.astype(BF16)
        lo = (sq - hi.astype(F32)).astype(BF16)
        ms = (_dot(hi, hh_ref[...]) + _dot(lo, hh_ref[...])) * (1.0 / HEAD_DIM)
        gd = gd_ref[rows, :]
        o_d = od * lax.rsqrt(ms + EPS) * hg_ref[...] * (gd * jax.nn.sigmoid(gd))
        y = (_dot(oa_ref[rows, :], wo_ref[0:256, :]) + _dot(o_b.astype(BF16), wo_ref[256:512, :])
             + _dot(oc_ref[rows, :], wo_ref[512:768, :]) + _dot(o_d.astype(BF16), wo_ref[768:1024, :]))
        x = _x_part(refs[:nx], part, r0, r1)
        o_ref[rows, :] = _gated_residual(x, y, gp_ref, (gtc_ref, gtb_ref), part, with_ctx)


def _outproj_call(x_parts, o_a, p, q, o_c, o_f, o_bk, u_f32, hg, hh, cs, w_fnet, w_out, l, g_post, mods, with_ctx):
    b = x_parts[0].shape[0]
    tm, t = (TM_ALL, T_ALL) if with_ctx else (TM_LAT, SEQ)
    col = lambda c: pl.BlockSpec((None, tm, 256), lambda b, i: (b, i, c))
    split = len(x_parts) == 2
    x_args = (x_parts[0],) * 3 + (x_parts[1],) if split else x_parts
    x_specs = _x_specs(True) if split else [pl.BlockSpec((None, tm, D_MODEL), lambda b, i: (b, i, 0))]
    return pl.pallas_call(
        functools.partial(_outproj_kernel, nx=len(x_args), with_ctx=with_ctx),
        grid=(b, t // tm),
        in_specs=[
            *x_specs,
            col(0), col(0), col(0), col(0), col(0), col(0), col(5),
            _const_spec((1, 256)), _const_spec((256, 256)), _const_spec((512, 256)),
            _layer_spec((256, 256), l), _layer_spec((D_MODEL, D_MODEL), l), _const_spec((1, D_MODEL)),
            *_mod_specs(2),
        ],
        out_specs=pl.BlockSpec((None, tm, D_MODEL), lambda b, i: (b, i, 0)),
        out_shape=jax.ShapeDtypeStruct((b, t, D_MODEL), F32),
        scratch_shapes=[pltpu.VMEM((256, 256), BF16), pltpu.VMEM((D_MODEL, D_MODEL), BF16)],
        compiler_params=_cparams(("arbitrary", "arbitrary")),
        name="out_proj",
    )(*x_args, o_a, p, q, o_c, o_f, o_bk, u_f32, hg, hh, cs, w_fnet, w_out, g_post, mods, mods)


def _ffn_kernel(x_ref, g_ref, shc_ref, shb_ref, scc_ref, scb_ref, w1_ref, w2_32_ref, gp_ref, gtc_ref, gtb_ref,
                o_ref, w2_ref, *, with_ctx):
    fc = 256

    @pl.when(_first_step())
    def _():
        w2_ref[...] = w2_32_ref[...].astype(BF16)

    def swiglu(h):
        acc = jnp.zeros((h.shape[0], D_MODEL), F32)
        for c in range(D_FF // fc):
            a = _dot(h, w1_ref[:, c * fc:(c + 1) * fc])
            g = _dot(h, w1_ref[:, D_FF + c * fc:D_FF + (c + 1) * fc])
            act = (a * jax.nn.sigmoid(a) * g).astype(BF16)
            acc = acc + _dot(act, w2_ref[c * fc:(c + 1) * fc, :])
        return acc

    parts = list(enumerate(_row_parts(x_ref.shape[0], with_ctx)))
    pre = lambda part, r0, r1: _modulated(x_ref[r0:r1, :], g_ref, (shc_ref, shb_ref), (scc_ref, scb_ref), part, with_ctx)
    post = lambda part, r0, r1, acc: _gated_residual(x_ref[r0:r1, :], acc, gp_ref, (gtc_ref, gtb_ref), part, with_ctx)
    if with_ctx:
        acc = swiglu(jnp.concatenate([pre(part, r0, r1) for part, (r0, r1) in parts], axis=0))
        for part, (r0, r1) in parts:
            o_ref[r0:r1, :] = post(part, r0, r1, acc[r0:r1])
    else:
        for part, (r0, r1) in parts:
            o_ref[r0:r1, :] = post(part, r0, r1, swiglu(pre(part, r0, r1)))


def _ffn_call(xs, g_pre, w1, w2, l, g_post, mods, with_ctx):
    b, t, _ = xs.shape
    tm = TM_ALL if with_ctx else TM_LAT
    return pl.pallas_call(
        functools.partial(_ffn_kernel, with_ctx=with_ctx),
        grid=(b, t // tm),
        in_specs=[
            pl.BlockSpec((None, tm, D_MODEL), lambda b, i: (b, i, 0)),
            _const_spec((1, D_MODEL)),
            *_mod_specs(3), *_mod_specs(4),
            _layer_spec((D_MODEL, 2 * D_FF), l), _layer_spec((D_FF, D_MODEL), l), _const_spec((1, D_MODEL)),
            *_mod_specs(5),
        ],
        out_specs=pl.BlockSpec((None, tm, D_MODEL), lambda b, i: (b, i, 0)),
        out_shape=jax.ShapeDtypeStruct(xs.shape, F32),
        scratch_shapes=[pltpu.VMEM((D_FF, D_MODEL), BF16)],
        compiler_params=_cparams(("arbitrary", "arbitrary")),
        name="ffn",
    )(xs, g_pre, mods, mods, mods, mods, w1, w2, g_post, mods, mods)


def kernel(x, c, ctx, c_ctx, w_ada, b_ada, g_pre_mix, g_post_mix, g_pre_ffn, g_post_ffn, w_in, w_out,
           na_rpb, w_fnet, swa_sink, hgrn_lb_logits, hgrn_norm_g, w_ffn_in, w_ffn_out):
    assert x.shape == (2, SEQ, D_MODEL) and ctx.shape == (2, CTX, D_MODEL) and c.shape == (2, D_MODEL)
    assert w_in.shape == (DEPTH, D_MODEL, D_IN) and w_ffn_in.shape == (DEPTH, D_MODEL, 2 * D_FF)
    assert na_rpb.shape == (DEPTH, 4, 2 * NA_WIN_R - 1, 2 * NA_WIN_C - 1) and swa_sink.shape == (DEPTH, 4)
    tb = _tables()
    bf = lambda name: jnp.asarray(tb[name]).astype(BF16)
    cos, sin = jnp.asarray(tb["rope_cos"]), jnp.asarray(tb["rope_sin"])
    f64, g2, fc, chan_cs, hh = bf("f64"), bf("g_stage2"), bf("f_ctx"), bf("chan_cs"), bf("head_ones")
    ball, lmask = bf("hgrn_b"), jnp.asarray(tb["hgrn_mask"])
    na_mask, swa_mask = jnp.asarray(tb["na_mask"]), jnp.asarray(tb["swa_mask"])

    cc = jnp.concatenate([c, c_ctx[None, :]], axis=0)
    mods = _ada_call(jnp.broadcast_to(cc[:, :, None], (3, D_MODEL, 128)), w_ada, b_ada)

    lb = jnp.cumsum(jax.nn.softmax(hgrn_lb_logits.astype(F32), axis=1), axis=1)
    lb = lb - lb[:, :1]
    gl_all = jnp.stack([jnp.log(lb), jnp.log1p(-lb)], axis=2)

    w_ffn_in_b = w_ffn_in.astype(BF16)

    xs = None
    for l in range(DEPTH):
        with_ctx = l < DEPTH - 1
        row = lambda a: a[l][None, :]
        x_all = (x, ctx) if l == 0 else (xs,)
        x_res = x_all if with_ctx else x_all[:1]
        u_att, u_f32, kn_t, ks_t = _proj_call(x_all, row(g_pre_mix), mods[l], w_in, l, cos, sin)

        rpb = jnp.pad(na_rpb[l], ((0, 0), (NA_DR_PAD, NA_DR_ROWS - NA_DR_PAD - (2 * NA_WIN_R - 1)), (0, 0)))
        t2 = _na_bias_call(rpb[:, :, tb["na_lane_l"]], rpb[:, :, tb["na_lane_r"]])
        o_a, o_c = _attn_call(u_att, kn_t, ks_t, swa_sink[l], na_mask, t2, swa_mask, with_ctx)
        p, q = _fnet_call(u_f32, f64, g2, fc)
        o_f, o_bk = _hgrn_call(u_f32, gl_all[:, l], ball, lmask, lb_zero=(l == 0))

        xs = _outproj_call(x_res, o_a, p, q, o_c, o_f, o_bk, u_f32, row(hgrn_norm_g), hh, chan_cs,
                           w_fnet, w_out, l, row(g_post_mix), mods[l], with_ctx)
        xs = _ffn_call(xs, row(g_pre_ffn), w_ffn_in_b, w_ffn_out, l, row(g_post_ffn), mods[l], with_ctx)
    return xs
```
